```python
import math
import jax, jax.numpy as jnp
from jax import lax
import numpy as np

D_MODEL = 1024
BATCH = 8
SEQ = 8192
DEPTH = 4

N_MIXERS = 2
N_HEADS = 8
HEAD_DIM = D_MODEL // (2 * N_HEADS)
V_HEAD_DIM = 2 * HEAD_DIM
ROPE_THETA = 10000.0
Q_BLOCK = 128
SSM_GROUP = 16
N_GROUPS = D_MODEL // SSM_GROUP
SSM_STATE = 64
DT_MIN = 1e-3
DT_MAX = 1e-1
D_FF = 4 * D_MODEL
RMS_EPS = 1e-6
N_ATTN_LAYERS = (DEPTH + 1) // 2
N_SSM_LAYERS = DEPTH // 2

kernel_name = "hybrid_diffattn_s5_encoder"


def rms_norm(x, gain):
    xf = x.astype(jnp.float32)
    y = xf * lax.rsqrt(jnp.mean(xf * xf, axis=-1, keepdims=True) + RMS_EPS)
    return (y * gain.astype(jnp.float32)).astype(x.dtype)


def rope_tables(seq):
    pos = jnp.arange(seq, dtype=jnp.float32)
    inv_freq = ROPE_THETA ** (-jnp.arange(0, HEAD_DIM, 2, dtype=jnp.float32) / HEAD_DIM)
    ang = pos[:, None] * inv_freq[None, :]
    return jnp.cos(ang), jnp.sin(ang)


def apply_rope(t, cos, sin):
    tf = t.astype(jnp.float32)
    t1, t2 = jnp.split(tf, 2, axis=-1)
    c = cos[None, :, None, None, :]
    s = sin[None, :, None, None, :]
    out = jnp.concatenate([t1 * c - t2 * s, t2 * c + t1 * s], axis=-1)
    return out.astype(t.dtype)


def lambda_init(layer_idx):
    return 0.8 - 0.6 * math.exp(-0.3 * layer_idx)


def diff_attention(u, w_qkv, q_gain, k_gain, lam_vecs, subln_gain, w_o, lam_init, cos, sin):
    b, s, _ = u.shape
    qkv = u @ w_qkv
    q, k, v = jnp.split(qkv, 3, axis=-1)
    q = q.reshape(b, s, N_HEADS, 2, HEAD_DIM)
    k = k.reshape(b, s, N_HEADS, 2, HEAD_DIM)
    v = v.reshape(b, s, N_HEADS, V_HEAD_DIM)
    q = apply_rope(rms_norm(q, q_gain), cos, sin) * (HEAD_DIM ** -0.5)
    k = apply_rope(rms_norm(k, k_gain), cos, sin)
    lv = lam_vecs.astype(jnp.float32)
    lam = jnp.exp(jnp.sum(lv[0] * lv[1])) - jnp.exp(jnp.sum(lv[2] * lv[3])) + lam_init
    n_blocks = s // Q_BLOCK
    qb = q.reshape(b, n_blocks, Q_BLOCK, N_HEADS, 2, HEAD_DIM).transpose(1, 0, 2, 3, 4, 5)

    def block(q_blk):
        scores = jnp.einsum('bqhcd,bkhcd->bhcqk', q_blk, k,
                            preferred_element_type=jnp.float32)
        p = jax.nn.softmax(scores, axis=-1)
        attn = (p[:, :, 0] - lam * p[:, :, 1]).astype(v.dtype)
        return jnp.einsum('bhqk,bkhe->bqhe', attn, v)

    o = lax.map(block, qb)
    o = o.transpose(1, 0, 2, 3, 4).reshape(b, s, N_HEADS, V_HEAD_DIM)
    o = rms_norm(o, subln_gain) * (1.0 - lam_init)
    return o.reshape(b, s, D_MODEL) @ w_o


def _ssm_combine(e1, e2):
    a1, h1 = e1
    a2, h2 = e2
    return a1 * a2, a2 * h1 + h2


def s5_layer(u, a_re, a_im, log_dt, b_re, b_im, c_re, c_im, d_skip, w_glu):
    b, s, _ = u.shape
    f32 = jnp.float32
    lam = lax.complex(a_re.astype(f32), a_im.astype(f32))
    dt = jnp.exp(log_dt.astype(f32))[..., None]
    a_bar = jnp.exp(lam * dt)
    b_mat = lax.complex(b_re.astype(f32), b_im.astype(f32))
    b_bar = ((a_bar - 1.0) / lam)[..., None] * b_mat
    c_mat = lax.complex(c_re.astype(f32), c_im.astype(f32))
    ug = u.astype(f32).reshape(b, s, N_GROUPS, SSM_GROUP)

    def scan_dir(u_row, r, reverse):
        bu = jnp.einsum('sgh,gph->sgp', u_row.astype(jnp.complex64), b_bar[r])
        a = jnp.broadcast_to(a_bar[r], bu.shape)
        _, states = lax.associative_scan(_ssm_combine, (a, bu), reverse=reverse, axis=0)
        return jnp.einsum('sgp,ghp->sgh', states, c_mat[r]).real

    def row(u_row):
        return scan_dir(u_row, 0, False) + scan_dir(u_row, 1, True)

    y = lax.map(row, ug)
    y = y.reshape(b, s, D_MODEL) + d_skip.astype(f32) * u.astype(f32)
    g = jax.nn.gelu(y).astype(u.dtype)
    val, gate = jnp.split(g @ w_glu, 2, axis=-1)
    return val * jax.nn.sigmoid(gate)


def squared_relu_mlp(u, w_up, w_down):
    return jnp.square(jax.nn.relu(u @ w_up)) @ w_down


def setup_inputs(seed: int = 0) -> dict:
    key = jax.random.key(seed)
    ks = jax.random.split(key, 24)
    f32 = jnp.float32
    D, NA, NS = D_MODEL, N_ATTN_LAYERS, N_SSM_LAYERS
    G, P, HG = N_GROUPS, SSM_STATE, SSM_GROUP
    nrm = lambda k, shape, scale: jax.random.normal(k, shape, f32) * scale
    gain = lambda k, shape: 1.0 + 0.02 * jax.random.normal(k, shape, f32)
    x = jax.random.normal(ks[0], (BATCH, SEQ, D), f32)
    a_re = -0.5 * jnp.exp(0.02 * jax.random.normal(ks[9], (NS, 2, G, P), f32))
    a_im = (math.pi * jnp.arange(P, dtype=f32))[None, None, None, :] + \
        0.01 * jax.random.normal(ks[10], (NS, 2, G, P), f32)
    log_dt = jax.random.uniform(ks[11], (NS, 2, G), f32,
                                minval=math.log(DT_MIN), maxval=math.log(DT_MAX))
    return {
        "x": x,
        "norm_mix": gain(ks[1], (DEPTH, D)),
        "norm_ffn": gain(ks[2], (DEPTH, D)),
        "attn_w_qkv": nrm(ks[3], (NA, D, 3 * D), D ** -0.5),
        "attn_q_gain": gain(ks[4], (NA, HEAD_DIM)),
        "attn_k_gain": gain(ks[5], (NA, HEAD_DIM)),
        "attn_lambda": nrm(ks[6], (NA, 4, HEAD_DIM), 0.1),
        "attn_subln": gain(ks[7], (NA, V_HEAD_DIM)),
        "attn_w_o": nrm(ks[8], (NA, D, D), D ** -0.5),
        "ssm_a_re": a_re,
        "ssm_a_im": a_im,
        "ssm_log_dt": log_dt,
        "ssm_b_re": nrm(ks[12], (NS, 2, G, P, HG), (2 * HG) ** -0.5),
        "ssm_b_im": nrm(ks[13], (NS, 2, G, P, HG), (2 * HG) ** -0.5),
        "ssm_c_re": nrm(ks[14], (NS, 2, G, HG, P), P ** -0.5),
        "ssm_c_im": nrm(ks[15], (NS, 2, G, HG, P), P ** -0.5),
        "ssm_d": nrm(ks[16], (NS, D), 1.0),
        "ssm_w_glu": nrm(ks[17], (NS, D, 2 * D), D ** -0.5),
        "ffn_w_up": nrm(ks[18], (DEPTH, D, D_FF), D ** -0.5),
        "ffn_w_down": nrm(ks[19], (DEPTH, D_FF, D), D_FF ** -0.5),
    }


def reference(x, norm_mix, norm_ffn, attn_w_qkv, attn_q_gain, attn_k_gain, attn_lambda,
              attn_subln, attn_w_o, ssm_a_re, ssm_a_im, ssm_log_dt, ssm_b_re, ssm_b_im,
              ssm_c_re, ssm_c_im, ssm_d, ssm_w_glu, ffn_w_up, ffn_w_down):
    cos, sin = rope_tables(x.shape[1])
    h = x
    for i in range(DEPTH):
        u = rms_norm(h, norm_mix[i])
        j = i // N_MIXERS
        if i % N_MIXERS == 0:
            mix = diff_attention(u, attn_w_qkv[j], attn_q_gain[j], attn_k_gain[j],
                                 attn_lambda[j], attn_subln[j], attn_w_o[j],
                                 lambda_init(i), cos, sin)
        else:
            mix = s5_layer(u, ssm_a_re[j], ssm_a_im[j], ssm_log_dt[j], ssm_b_re[j],
                           ssm_b_im[j], ssm_c_re[j], ssm_c_im[j], ssm_d[j], ssm_w_glu[j])
        h = h + mix.astype(h.dtype)
        h = h + squared_relu_mlp(rms_norm(h, norm_ffn[i]), ffn_w_up[i], ffn_w_down[i]).astype(h.dtype)
    return h
```

```python
import functools
import math

import jax
import jax.numpy as jnp
from jax import lax
from jax.experimental import pallas as pl
from jax.experimental.pallas import tpu as pltpu

F32 = jnp.float32
BF16 = jnp.bfloat16

D_MODEL = 1024
N_HEADS = 8
HEAD_DIM = 64
V_HEAD_DIM = 2 * HEAD_DIM
ROPE_THETA = 10000.0
SSM_GROUP = 16
N_GROUPS = D_MODEL // SSM_GROUP
SSM_STATE = 64
D_FF = 4 * D_MODEL
RMS_EPS = 1e-6
N_MIXERS = 2

LANES = 128
SSM_CHUNK = 64
VMEM_LIMIT = 48 * 1024 * 1024


def _params(*sem):
    return pltpu.CompilerParams(dimension_semantics=sem, vmem_limit_bytes=VMEM_LIMIT)


def _rms(x, gain):
    ms = jnp.mean(x * x, axis=-1, keepdims=True)
    return x * lax.rsqrt(ms + RMS_EPS) * gain


def _dot(a, b):
    return jnp.dot(a, b, preferred_element_type=F32)


def _qkv_kernel(x_ref, g_ref, w_ref, qg_ref, kg_ref, cos_ref, sin_ref,
                q_ref, k_ref, v_ref):
    tm = x_ref.shape[0]
    xn = _rms(x_ref[...], g_ref[...]).astype(BF16)
    row = lax.broadcasted_iota(jnp.int32, (LANES, LANES), 0)
    col = lax.broadcasted_iota(jnp.int32, (LANES, LANES), 1)
    seg_mean = jnp.where(row // HEAD_DIM == col // HEAD_DIM, 1.0 / HEAD_DIM, 0.0).astype(BF16)
    lane = lax.broadcasted_iota(jnp.int32, (tm, LANES), 1)
    first_half = (lane % HEAD_DIM) < (HEAD_DIM // 2)
    cos = cos_ref[...]
    sin = sin_ref[...]

    def norm_rope(y, gain, scale):
        sq = y * y
        hi = sq.astype(BF16)
        lo = (sq - hi.astype(F32)).astype(BF16)
        ms = _dot(hi, seg_mean) + _dot(lo, seg_mean)
        yn = y * lax.rsqrt(ms + RMS_EPS) * gain
        partner = jnp.where(first_half,
                            pltpu.roll(yn, LANES - HEAD_DIM // 2, 1),
                            pltpu.roll(yn, HEAD_DIM // 2, 1))
        return (yn * cos + partner * sin) * scale

    yq = _dot(xn, w_ref[:, 0:D_MODEL])
    for t in range(D_MODEL // LANES):
        sl = slice(t * LANES, (t + 1) * LANES)
        q_ref[:, sl] = norm_rope(yq[:, sl], qg_ref[...], HEAD_DIM ** -0.5).astype(BF16)
    yk = _dot(xn, w_ref[:, D_MODEL:2 * D_MODEL])
    for t in range(D_MODEL // LANES):
        sl = slice(t * LANES, (t + 1) * LANES)
        k_ref[:, sl] = norm_rope(yk[:, sl], kg_ref[...], 1.0).astype(BF16)
    v_ref[...] = _dot(xn, w_ref[:, 2 * D_MODEL:3 * D_MODEL]).astype(BF16)


def _qkv(h2, gain, w_qkv, q_gain, k_gain, cos_t, sin_t, seq, tm=512):
    tokens = h2.shape[0]
    tm = min(tm, seq)
    assert tokens % tm == 0 and seq % tm == 0
    n_pos_blocks = seq // tm
    tok = lambda i: (i, 0)
    full = lambda i: (0, 0)
    pos = lambda i: (i % n_pos_blocks, 0)
    out = jax.ShapeDtypeStruct((tokens, D_MODEL), BF16)
    return pl.pallas_call(
        _qkv_kernel,
        grid=(tokens // tm,),
        in_specs=[
            pl.BlockSpec((tm, D_MODEL), tok),
            pl.BlockSpec((1, D_MODEL), full),
            pl.BlockSpec((D_MODEL, 3 * D_MODEL), full),
            pl.BlockSpec((1, LANES), full),
            pl.BlockSpec((1, LANES), full),
            pl.BlockSpec((tm, LANES), pos),
            pl.BlockSpec((tm, LANES), pos),
        ],
        out_specs=[pl.BlockSpec((tm, D_MODEL), tok)] * 3,
        out_shape=[out, out, out],
        compiler_params=_params("parallel"),
        name="qkv_norm_rope",
    )(h2, gain, w_qkv, q_gain, k_gain, cos_t, sin_t)


def _attn_kernel(q_ref, k_ref, v_ref, lam_ref, sg_ref, o_ref,
                 qs_ref, m_ref, l_ref, acc_ref, *, lam_init):
    tq = q_ref.shape[0]
    ki = pl.program_id(3)

    @pl.when(ki == 0)
    def _():
        q = q_ref[...]
        lane = lax.broadcasted_iota(jnp.int32, q.shape, 1)
        zero = jnp.zeros_like(q)
        qs_ref[0:tq, :] = jnp.where(lane < HEAD_DIM, q, zero)
        qs_ref[tq:2 * tq, :] = jnp.where(lane >= HEAD_DIM, q, zero)
        m_ref[...] = jnp.full(m_ref.shape, -jnp.inf, F32)
        l_ref[...] = jnp.zeros(l_ref.shape, F32)
        acc_ref[...] = jnp.zeros(acc_ref.shape, F32)

    s = lax.dot_general(qs_ref[...], k_ref[...], (((1,), (1,)), ((), ())),
                        preferred_element_type=F32)
    m_prev = m_ref[...]
    m_new = jnp.maximum(m_prev, jnp.max(s, axis=1, keepdims=True))
    alpha = jnp.exp(m_prev - m_new)
    p = jnp.exp(s - m_new[:, 0:1])
    l_ref[...] = alpha * l_ref[...] + jnp.sum(p, axis=1, keepdims=True)
    acc_ref[...] = alpha * acc_ref[...] + _dot(p.astype(BF16), v_ref[...])
    m_ref[...] = m_new

    @pl.when(ki == pl.num_programs(3) - 1)
    def _():
        lv = lam_ref[...]
        lam = (jnp.exp(jnp.sum(lv[0:1] * lv[1:2], axis=1, keepdims=True))
               - jnp.exp(jnp.sum(lv[2:3] * lv[3:4], axis=1, keepdims=True)) + lam_init)
        o1 = acc_ref[0:tq, :] / l_ref[0:tq, :]
        o2 = acc_ref[tq:2 * tq, :] / l_ref[tq:2 * tq, :]
        o = o1 - lam * o2
        o_ref[...] = (_rms(o, sg_ref[...]) * (1.0 - lam_init)).astype(BF16)


def _attention(q, k, v, lam_vecs, subln, lam_init, tq=512, tk=512):
    b, s, _ = q.shape
    tq, tk = min(tq, s), min(tk, s)
    assert s % tq == 0 and s % tk == 0
    qmap = lambda bi, hi, qi, ki: (bi, qi, hi)
    kmap = lambda bi, hi, qi, ki: (bi, ki, hi)
    full = lambda bi, hi, qi, ki: (0, 0)
    return pl.pallas_call(
        functools.partial(_attn_kernel, lam_init=lam_init),
        grid=(b, N_HEADS, s // tq, s // tk),
        in_specs=[
            pl.BlockSpec((None, tq, LANES), qmap),
            pl.BlockSpec((None, tk, LANES), kmap),
            pl.BlockSpec((None, tk, LANES), kmap),
            pl.BlockSpec((4, HEAD_DIM), full),
            pl.BlockSpec((1, V_HEAD_DIM), full),
        ],
        out_specs=pl.BlockSpec((None, tq, LANES), qmap),
        out_shape=jax.ShapeDtypeStruct((b, s, D_MODEL), BF16),
        scratch_shapes=[
            pltpu.VMEM((2 * tq, LANES), BF16),
            pltpu.VMEM((2 * tq, LANES), F32),
            pltpu.VMEM((2 * tq, LANES), F32),
            pltpu.VMEM((2 * tq, LANES), F32),
        ],
        compiler_params=_params("parallel", "parallel", "parallel", "arbitrary"),
        name="diff_flash_attention",
    )(q, k, v, lam_vecs, subln)


def _proj_kernel(h_ref, o_ref, w_ref, out_ref):
    out_ref[...] = h_ref[...] + _dot(o_ref[...], w_ref[...])


def _proj_residual(h2, o2, w_o, tm=512):
    tokens = h2.shape[0]
    tok = lambda i: (i, 0)
    return pl.pallas_call(
        _proj_kernel,
        grid=(tokens // tm,),
        in_specs=[
            pl.BlockSpec((tm, D_MODEL), tok),
            pl.BlockSpec((tm, D_MODEL), tok),
            pl.BlockSpec((D_MODEL, D_MODEL), lambda i: (0, 0)),
        ],
        out_specs=pl.BlockSpec((tm, D_MODEL), tok),
        out_shape=jax.ShapeDtypeStruct((tokens, D_MODEL), F32),
        compiler_params=_params("parallel"),
        name="attn_out_proj",
    )(h2, o2, w_o)


def _mlp_kernel(h_ref, g_ref, wu_ref, wd_ref, out_ref, xn_ref, acc_ref):
    f = pl.program_id(1)

    @pl.when(f == 0)
    def _():
        h = h_ref[...]
        xn_ref[...] = _rms(h, g_ref[...]).astype(BF16)
        acc_ref[...] = h

    a = jnp.maximum(_dot(xn_ref[...], wu_ref[...]), 0.0)
    acc_ref[...] += _dot((a * a).astype(BF16), wd_ref[...])

    @pl.when(f == pl.num_programs(1) - 1)
    def _():
        out_ref[...] = acc_ref[...]


def _mlp(h2, gain, w_up, w_down, tm=512, tf=1024):
    tokens = h2.shape[0]
    tok = lambda i, f: (i, 0)
    return pl.pallas_call(
        _mlp_kernel,
        grid=(tokens // tm, D_FF // tf),
        in_specs=[
            pl.BlockSpec((tm, D_MODEL), tok),
            pl.BlockSpec((1, D_MODEL), lambda i, f: (0, 0)),
            pl.BlockSpec((D_MODEL, tf), lambda i, f: (0, f)),
            pl.BlockSpec((tf, D_MODEL), lambda i, f: (f, 0)),
        ],
        out_specs=pl.BlockSpec((tm, D_MODEL), tok),
        out_shape=jax.ShapeDtypeStruct((tokens, D_MODEL), F32),
        scratch_shapes=[pltpu.VMEM((tm, D_MODEL), BF16), pltpu.VMEM((tm, D_MODEL), F32)],
        compiler_params=_params("parallel", "arbitrary"),
        name="relu2_mlp",
    )(h2, gain, w_up, w_down)


def _norm_kernel(h_ref, g_ref, u_ref):
    u_ref[...] = _rms(h_ref[...], g_ref[...]).astype(BF16)


def _norm(h2, gain, tm=1024):
    tokens = h2.shape[0]
    tok = lambda i: (i, 0)
    return pl.pallas_call(
        _norm_kernel,
        grid=(tokens // tm,),
        in_specs=[pl.BlockSpec((tm, D_MODEL), tok), pl.BlockSpec((1, D_MODEL), lambda i: (0, 0))],
        out_specs=pl.BlockSpec((tm, D_MODEL), tok),
        out_shape=jax.ShapeDtypeStruct((tokens, D_MODEL), BF16),
        compiler_params=_params("parallel"),
        name="ssm_pre_norm",
    )(h2, gain)


def _ssm_kernel(u_ref, e_ref, win_ref, wout_ref, al_ref, y_ref,
                t_ref, zs_ref, sfr_ref, sbr_ref, sfi_ref, sbi_ref, *, batch):
    rows, width = u_ref.shape
    n_chunks = rows // batch
    chunk = width // SSM_GROUP
    p = SSM_STATE

    e = e_ref[...]
    for j in range(chunk):
        off = SSM_GROUP * (chunk - 1 - j)
        t_ref[SSM_GROUP * j:SSM_GROUP * (j + 1), :] = e[:, off:off + width].astype(BF16)

    u = u_ref[...]
    zs_ref[...] = _dot(u, win_ref[...])

    ar = al_ref[0:1, :]
    ai = al_ref[1:2, :]
    is_fwd = lax.broadcasted_iota(jnp.int32, (batch, 2 * p), 1) < p

    def step(kk, carry):
        xr, xi = carry
        rf = pl.multiple_of(kk * batch, batch)
        rb = pl.multiple_of((n_chunks - 1 - kk) * batch, batch)
        lr = jnp.where(is_fwd, zs_ref[pl.ds(rf, batch), 0:2 * p], zs_ref[pl.ds(rb, batch), 0:2 * p])
        li = jnp.where(is_fwd, zs_ref[pl.ds(rf, batch), 2 * p:4 * p],
                       zs_ref[pl.ds(rb, batch), 2 * p:4 * p])
        sfr_ref[pl.ds(rf, batch), :] = xr
        sbr_ref[pl.ds(rb, batch), :] = xr
        sfi_ref[pl.ds(rf, batch), :] = xi
        sbi_ref[pl.ds(rb, batch), :] = xi
        return ar * xr - ai * xi + lr, ar * xi + ai * xr + li

    zero = jnp.zeros((batch, 2 * p), F32)
    lax.fori_loop(0, n_chunks, step, (zero, zero))

    fwd_lane = lax.broadcasted_iota(jnp.int32, (rows, 2 * p), 1) < p
    xin = jnp.concatenate(
        [jnp.where(fwd_lane, sfr_ref[...], sbr_ref[...]),
         jnp.where(fwd_lane, sfi_ref[...], sbi_ref[...])], axis=1).astype(BF16)
    y_ref[...] = _dot(u, t_ref[...]) + _dot(xin, wout_ref[...])


def _ssm(ug, e_rows, w_in, w_out, a_chunk, batch):
    g, rows, width = ug.shape
    p = SSM_STATE
    grp = lambda i: (i, 0, 0)
    return pl.pallas_call(
        functools.partial(_ssm_kernel, batch=batch),
        grid=(g,),
        in_specs=[
            pl.BlockSpec((None, rows, width), grp),
            pl.BlockSpec((None, SSM_GROUP, 2 * width), grp),
            pl.BlockSpec((None, width, 4 * p), grp),
            pl.BlockSpec((None, 4 * p, width), grp),
            pl.BlockSpec((None, 2, 2 * p), grp),
        ],
        out_specs=pl.BlockSpec((None, rows, width), grp),
        out_shape=jax.ShapeDtypeStruct((g, rows, width), F32),
        scratch_shapes=[
            pltpu.VMEM((width, width), BF16),
            pltpu.VMEM((rows, 4 * p), F32),
            pltpu.VMEM((rows, 2 * p), F32),
            pltpu.VMEM((rows, 2 * p), F32),
            pltpu.VMEM((rows, 2 * p), F32),
            pltpu.VMEM((rows, 2 * p), F32),
        ],
        compiler_params=_params("parallel"),
        name="s5_chunked_scan",
    )(ug, e_rows, w_in, w_out, a_chunk)


def _glu_kernel(h_ref, y_ref, g_ref, d_ref, w_ref, out_ref):
    h = h_ref[...]
    u = _rms(h, g_ref[...])
    gact = jax.nn.gelu(y_ref[...] + d_ref[...] * u).astype(BF16)
    z = _dot(gact, w_ref[...])
    out_ref[...] = h + z[:, 0:D_MODEL] * jax.nn.sigmoid(z[:, D_MODEL:2 * D_MODEL])


def _glu(h2, y2, gain, d_skip, w_glu, tm=512):
    tokens = h2.shape[0]
    tok = lambda i: (i, 0)
    vec = lambda i: (0, 0)
    return pl.pallas_call(
        _glu_kernel,
        grid=(tokens // tm,),
        in_specs=[
            pl.BlockSpec((tm, D_MODEL), tok),
            pl.BlockSpec((tm, D_MODEL), tok),
            pl.BlockSpec((1, D_MODEL), vec),
            pl.BlockSpec((1, D_MODEL), vec),
            pl.BlockSpec((D_MODEL, 2 * D_MODEL), vec),
        ],
        out_specs=pl.BlockSpec((tm, D_MODEL), tok),
        out_shape=jax.ShapeDtypeStruct((tokens, D_MODEL), F32),
        compiler_params=_params("parallel"),
        name="ssm_gelu_glu",
    )(h2, y2, gain, d_skip, w_glu)


def _ssm_operands(a_re, a_im, log_dt, b_re, b_im, c_re, c_im, chunk):
    hp = lax.Precision.HIGHEST
    dt = jnp.exp(log_dt)[..., None]
    zr, zi = a_re * dt, a_im * dt
    t = jnp.arange(chunk + 1, dtype=F32)[None, None, :, None]
    mag = jnp.exp(zr[:, :, None, :] * t)
    ang = zi[:, :, None, :] * t
    pr, pi = mag * jnp.cos(ang), mag * jnp.sin(ang)
    nr, ni = pr[:, :, 1, :] - 1.0, pi[:, :, 1, :]
    den = a_re * a_re + a_im * a_im
    qr = (nr * a_re + ni * a_im) / den
    qi = (ni * a_re - nr * a_im) / den
    bbr = qr[..., None] * b_re - qi[..., None] * b_im
    bbi = qr[..., None] * b_im + qi[..., None] * b_re

    car = c_re[:, :, None] * pr[:, :, :chunk, None, :] - c_im[:, :, None] * pi[:, :, :chunk, None, :]
    cai = c_re[:, :, None] * pi[:, :, :chunk, None, :] + c_im[:, :, None] * pr[:, :, :chunk, None, :]
    kern = (jnp.einsum('dgtop,dgpi->dgtoi', car, bbr, precision=hp)
            - jnp.einsum('dgtop,dgpi->dgtoi', cai, bbi, precision=hp))
    g = a_re.shape[1]
    krow = kern.transpose(0, 1, 4, 2, 3)
    center = (krow[0, :, :, 0] + krow[1, :, :, 0])[:, :, None]
    e_rows = jnp.concatenate(
        [krow[1, :, :, :0:-1], center, krow[0, :, :, 1:],
         jnp.zeros((g, SSM_GROUP, 1, SSM_GROUP), F32)], axis=2)
    e_rows = e_rows.reshape(g, SSM_GROUP, 2 * chunk * SSM_GROUP)

    pw_r = jnp.stack([pr[0, :, chunk - 1::-1][:, :chunk], pr[1, :, :chunk]])
    pw_i = jnp.stack([pi[0, :, chunk - 1::-1][:, :chunk], pi[1, :, :chunk]])
    bt_r = bbr.transpose(0, 1, 3, 2)[:, :, None]
    bt_i = bbi.transpose(0, 1, 3, 2)[:, :, None]
    wi_r = pw_r[:, :, :, None, :] * bt_r - pw_i[:, :, :, None, :] * bt_i
    wi_i = pw_r[:, :, :, None, :] * bt_i + pw_i[:, :, :, None, :] * bt_r
    w_in = jnp.concatenate([wi_r[0], wi_r[1], wi_i[0], wi_i[1]], axis=-1)
    w_in = w_in.reshape(g, chunk * SSM_GROUP, 4 * SSM_STATE)

    qw_r = jnp.stack([pr[0, :, 1:], pr[1, :, :0:-1]])
    qw_i = jnp.stack([pi[0, :, 1:], pi[1, :, :0:-1]])
    wo_r = c_re[:, :, None] * qw_r[:, :, :, None, :] - c_im[:, :, None] * qw_i[:, :, :, None, :]
    wo_i = c_re[:, :, None] * qw_i[:, :, :, None, :] + c_im[:, :, None] * qw_r[:, :, :, None, :]
    w_out = jnp.concatenate([wo_r[0], wo_r[1], -wo_i[0], -wo_i[1]], axis=-1)
    w_out = w_out.reshape(g, chunk * SSM_GROUP, 4 * SSM_STATE).transpose(0, 2, 1)

    a_chunk = jnp.stack([jnp.concatenate([pr[0, :, chunk], pr[1, :, chunk]], axis=-1),
                         jnp.concatenate([pi[0, :, chunk], pi[1, :, chunk]], axis=-1)], axis=1)
    return e_rows, w_in.astype(BF16), w_out.astype(BF16), a_chunk


def _ssm_layer(h2, batch, seq, gain, a_re, a_im, log_dt, b_re, b_im, c_re, c_im, d_skip, w_glu):
    chunk = SSM_CHUNK
    n_chunks = seq // chunk
    e_rows, w_in, w_out, a_chunk = _ssm_operands(a_re, a_im, log_dt, b_re, b_im, c_re, c_im, chunk)
    u = _norm(h2, gain)
    ug = u.reshape(batch, n_chunks, chunk, N_GROUPS, SSM_GROUP).transpose(3, 1, 0, 2, 4)
    ug = ug.reshape(N_GROUPS, n_chunks * batch, chunk * SSM_GROUP)
    yg = _ssm(ug, e_rows, w_in, w_out, a_chunk, batch)
    y = yg.reshape(N_GROUPS, n_chunks, batch, chunk, SSM_GROUP).transpose(2, 1, 3, 0, 4)
    y2 = y.reshape(batch * seq, D_MODEL)
    return _glu(h2, y2, gain, d_skip, w_glu)


def _rope_tables(seq):
    pos = jnp.arange(seq, dtype=F32)
    inv_freq = ROPE_THETA ** (-jnp.arange(0, HEAD_DIM, 2, dtype=F32) / HEAD_DIM)
    ang = pos[:, None] * inv_freq[None, :]
    cos, sin = jnp.cos(ang), jnp.sin(ang)
    reps = LANES // HEAD_DIM
    cos_t = jnp.tile(jnp.concatenate([cos, cos], axis=-1), (1, reps))
    sin_t = jnp.tile(jnp.concatenate([-sin, sin], axis=-1), (1, reps))
    return cos_t, sin_t


def _lambda_init(layer_idx):
    return 0.8 - 0.6 * math.exp(-0.3 * layer_idx)


def kernel(x, norm_mix, norm_ffn, attn_w_qkv, attn_q_gain, attn_k_gain, attn_lambda,
           attn_subln, attn_w_o, ssm_a_re, ssm_a_im, ssm_log_dt, ssm_b_re, ssm_b_im,
           ssm_c_re, ssm_c_im, ssm_d, ssm_w_glu, ffn_w_up, ffn_w_down):
    batch, seq, d = x.shape
    assert d == D_MODEL
    depth = norm_mix.shape[0]
    cos_t, sin_t = _rope_tables(seq)
    reps = LANES // HEAD_DIM
    h2 = x.reshape(batch * seq, D_MODEL)
    for i in range(depth):
        j = i // N_MIXERS
        gain = norm_mix[i][None, :]
        if i % N_MIXERS == 0:
            q, k, v = _qkv(h2, gain, attn_w_qkv[j].astype(BF16),
                           jnp.tile(attn_q_gain[j], reps)[None, :],
                           jnp.tile(attn_k_gain[j], reps)[None, :], cos_t, sin_t, seq)
            shape3 = (batch, seq, D_MODEL)
            o = _attention(q.reshape(shape3), k.reshape(shape3), v.reshape(shape3),
                           attn_lambda[j], attn_subln[j][None, :], _lambda_init(i))
            h2 = _proj_residual(h2, o.reshape(batch * seq, D_MODEL), attn_w_o[j].astype(BF16))
        else:
            h2 = _ssm_layer(h2, batch, seq, gain, ssm_a_re[j], ssm_a_im[j], ssm_log_dt[j],
                            ssm_b_re[j], ssm_b_im[j], ssm_c_re[j], ssm_c_im[j],
                            ssm_d[j][None, :], ssm_w_glu[j].astype(BF16))
        h2 = _mlp(h2, norm_ffn[i][None, :], ffn_w_up[i].astype(BF16), ffn_w_down[i].astype(BF16))
    return h2.reshape(batch, seq, D_MODEL)
```

```python
import functools
import math

import jax
import jax.numpy as jnp
from jax import lax
from jax.experimental import pallas as pl
from jax.experimental.pallas import tpu as pltpu

F32 = jnp.float32
BF16 = jnp.bfloat16

D_MODEL = 1024
N_HEADS = 8
HEAD_DIM = 64
V_HEAD_DIM = 2 * HEAD_DIM
ROPE_THETA = 10000.0
SSM_GROUP = 16
N_GROUPS = D_MODEL // SSM_GROUP
SSM_STATE = 64
D_FF = 4 * D_MODEL
RMS_EPS = 1e-6
Q_SCALE = HEAD_DIM ** -0.5 * math.log2(math.e)
SAFE_LOG2_SCORE = 80.0
N_MIXERS = 2

LANES = 128
SSM_CHUNK = 64
VMEM_LIMIT = 48 * 1024 * 1024


def _params(*sem):
    return pltpu.CompilerParams(dimension_semantics=sem, vmem_limit_bytes=VMEM_LIMIT)


def _rms(x, gain):
    ms = jnp.mean(x * x, axis=-1, keepdims=True)
    return x * lax.rsqrt(ms + RMS_EPS) * gain


def _dot(a, b):
    return jnp.dot(a, b, preferred_element_type=F32)


def _qkv_kernel(x_ref, g_ref, w_ref, qg_ref, kg_ref, cos_ref, sin_ref,
                q_ref, k_ref, v_ref):
    tm = x_ref.shape[0]
    xn = _rms(x_ref[...], g_ref[...]).astype(BF16)
    row = lax.broadcasted_iota(jnp.int32, (LANES, LANES), 0)
    col = lax.broadcasted_iota(jnp.int32, (LANES, LANES), 1)
    seg_mean = jnp.where(row // HEAD_DIM == col // HEAD_DIM, 1.0 / HEAD_DIM, 0.0).astype(BF16)
    lane = lax.broadcasted_iota(jnp.int32, (tm, LANES), 1)
    first_half = (lane % HEAD_DIM) < (HEAD_DIM // 2)
    cos = cos_ref[...]
    sin = sin_ref[...]

    def norm_rope(y, gain, scale):
        sq = y * y
        hi = sq.astype(BF16)
        lo = (sq - hi.astype(F32)).astype(BF16)
        ms = _dot(hi, seg_mean) + _dot(lo, seg_mean)
        yn = y * lax.rsqrt(ms + RMS_EPS) * gain
        partner = jnp.where(first_half,
                            pltpu.roll(yn, LANES - HEAD_DIM // 2, 1),
                            pltpu.roll(yn, HEAD_DIM // 2, 1))
        return (yn * cos + partner * sin) * scale

    yq = _dot(xn, w_ref[:, 0:D_MODEL])
    for t in range(D_MODEL // LANES):
        sl = slice(t * LANES, (t + 1) * LANES)
        q_ref[:, sl] = norm_rope(yq[:, sl], qg_ref[...], Q_SCALE).astype(BF16)
    yk = _dot(xn, w_ref[:, D_MODEL:2 * D_MODEL])
    for t in range(D_MODEL // LANES):
        sl = slice(t * LANES, (t + 1) * LANES)
        k_ref[:, sl] = norm_rope(yk[:, sl], kg_ref[...], 1.0).astype(BF16)
    yv = _dot(xn, w_ref[:, 2 * D_MODEL:3 * D_MODEL]).astype(BF16)
    ones_col = jnp.where(lane == 0, 1.0, 0.0).astype(BF16)
    for t in range(N_HEADS):
        v_ref[:, 2 * t * LANES:(2 * t + 1) * LANES] = yv[:, t * LANES:(t + 1) * LANES]
        v_ref[:, (2 * t + 1) * LANES:(2 * t + 2) * LANES] = ones_col


def _qkv(h2, gain, w_qkv, q_gain, k_gain, cos_t, sin_t, seq, tm=512):
    tokens = h2.shape[0]
    tm = min(tm, seq)
    assert tokens % tm == 0 and seq % tm == 0
    n_pos_blocks = seq // tm
    tok = lambda i: (i, 0)
    full = lambda i: (0, 0)
    pos = lambda i: (i % n_pos_blocks, 0)
    out = jax.ShapeDtypeStruct((tokens, D_MODEL), BF16)
    return pl.pallas_call(
        _qkv_kernel,
        grid=(tokens // tm,),
        in_specs=[
            pl.BlockSpec((tm, D_MODEL), tok),
            pl.BlockSpec((1, D_MODEL), full),
            pl.BlockSpec((D_MODEL, 3 * D_MODEL), full),
            pl.BlockSpec((1, LANES), full),
            pl.BlockSpec((1, LANES), full),
            pl.BlockSpec((tm, LANES), pos),
            pl.BlockSpec((tm, LANES), pos),
        ],
        out_specs=[pl.BlockSpec((tm, D_MODEL), tok), pl.BlockSpec((tm, D_MODEL), tok),
                   pl.BlockSpec((tm, 2 * D_MODEL), tok)],
        out_shape=[out, out, jax.ShapeDtypeStruct((tokens, 2 * D_MODEL), BF16)],
        compiler_params=_params("parallel"),
        name="qkv_norm_rope",
    )(h2, gain, w_qkv, q_gain, k_gain, cos_t, sin_t)


def _stack_q(q_ref, qs_ref):
    tq = q_ref.shape[0]
    q = q_ref[...]
    lane = lax.broadcasted_iota(jnp.int32, q.shape, 1)
    zero = jnp.zeros_like(q)
    qs_ref[0:tq, :] = jnp.where(lane < HEAD_DIM, q, zero)
    qs_ref[tq:2 * tq, :] = jnp.where(lane >= HEAD_DIM, q, zero)


def _attn_finish(o1, o2, lam_ref, sg_ref, o_ref, lam_init):
    lv = lam_ref[...]
    lam = (jnp.exp(jnp.sum(lv[0:1] * lv[1:2], axis=1, keepdims=True))
           - jnp.exp(jnp.sum(lv[2:3] * lv[3:4], axis=1, keepdims=True)) + lam_init)
    o = o1 - lam * o2
    o_ref[...] = (_rms(o, sg_ref[...]) * (1.0 - lam_init)).astype(BF16)


def _attn_bounded_kernel(q_ref, k_ref, v_ref, lam_ref, sg_ref, o_ref, qs_ref, acc_ref,
                         *, lam_init, tk):
    tq = q_ref.shape[0]
    n_chunks = k_ref.shape[0] // tk
    _stack_q(q_ref, qs_ref)
    acc_ref[...] = jnp.zeros(acc_ref.shape, F32)

    def chunk(c, carry):
        off = pl.multiple_of(c * tk, tk)
        s = lax.dot_general(qs_ref[...], k_ref[pl.ds(off, tk), :], (((1,), (1,)), ((), ())),
                            preferred_element_type=F32)
        acc_ref[...] += _dot(jnp.exp2(s).astype(BF16), v_ref[pl.ds(off, tk), :])
        return carry

    lax.fori_loop(0, n_chunks, chunk, 0, unroll=4)
    o = acc_ref[:, 0:LANES] / acc_ref[:, LANES:LANES + 1]
    _attn_finish(o[0:tq], o[tq:2 * tq], lam_ref, sg_ref, o_ref, lam_init)


def _attn_online_kernel(q_ref, k_ref, v_ref, lam_ref, sg_ref, o_ref,
                        qs_ref, m_ref, l_ref, acc_ref, *, lam_init):
    tq = q_ref.shape[0]
    ki = pl.program_id(3)

    @pl.when(ki == 0)
    def _():
        _stack_q(q_ref, qs_ref)
        m_ref[...] = jnp.full(m_ref.shape, -jnp.inf, F32)
        l_ref[...] = jnp.zeros(l_ref.shape, F32)
        acc_ref[...] = jnp.zeros(acc_ref.shape, F32)

    s = lax.dot_general(qs_ref[...], k_ref[...], (((1,), (1,)), ((), ())),
                        preferred_element_type=F32)
    m_prev = m_ref[...]
    m_new = jnp.maximum(m_prev, jnp.max(s, axis=1, keepdims=True))
    alpha = jnp.exp2(m_prev - m_new)
    p = jnp.exp2(s - m_new[:, 0:1])
    l_ref[...] = alpha * l_ref[...] + jnp.sum(p, axis=1, keepdims=True)
    acc_ref[...] = alpha * acc_ref[...] + _dot(p.astype(BF16), v_ref[...])
    m_ref[...] = m_new

    @pl.when(ki == pl.num_programs(3) - 1)
    def _():
        o = acc_ref[...] / l_ref[...]
        _attn_finish(o[0:tq], o[tq:2 * tq], lam_ref, sg_ref, o_ref, lam_init)


def _attention_bounded(q, k, v_aug, lam_vecs, subln, lam_init, tq=512, tk=512):
    b, s, _ = q.shape
    tq, tk = min(tq, s), min(tk, s)
    assert s % tq == 0 and s % tk == 0
    qmap = lambda bi, hi, qi: (bi, qi, hi)
    kvmap = lambda bi, hi, qi: (bi, 0, hi)
    full = lambda bi, hi, qi: (0, 0)
    return pl.pallas_call(
        functools.partial(_attn_bounded_kernel, lam_init=lam_init, tk=tk),
        grid=(b, N_HEADS, s // tq),
        in_specs=[
            pl.BlockSpec((None, tq, LANES), qmap),
            pl.BlockSpec((None, s, LANES), kvmap),
            pl.BlockSpec((None, s, 2 * LANES), kvmap),
            pl.BlockSpec((4, HEAD_DIM), full),
            pl.BlockSpec((1, V_HEAD_DIM), full),
        ],
        out_specs=pl.BlockSpec((None, tq, LANES), qmap),
        out_shape=jax.ShapeDtypeStruct((b, s, D_MODEL), BF16),
        scratch_shapes=[
            pltpu.VMEM((2 * tq, LANES), BF16),
            pltpu.VMEM((2 * tq, 2 * LANES), F32),
        ],
        compiler_params=_params("parallel", "parallel", "parallel"),
        name="diff_attention_bounded",
    )(q, k, v_aug, lam_vecs, subln)


def _attention_online(q, k, v_aug, lam_vecs, subln, lam_init, tq=512, tk=512):
    b, s, _ = q.shape
    tq, tk = min(tq, s), min(tk, s)
    assert s % tq == 0 and s % tk == 0
    qmap = lambda bi, hi, qi, ki: (bi, qi, hi)
    kmap = lambda bi, hi, qi, ki: (bi, ki, hi)
    vmap = lambda bi, hi, qi, ki: (bi, ki, 2 * hi)
    full = lambda bi, hi, qi, ki: (0, 0)
    return pl.pallas_call(
        functools.partial(_attn_online_kernel, lam_init=lam_init),
        grid=(b, N_HEADS, s // tq, s // tk),
        in_specs=[
            pl.BlockSpec((None, tq, LANES), qmap),
            pl.BlockSpec((None, tk, LANES), kmap),
            pl.BlockSpec((None, tk, LANES), vmap),
            pl.BlockSpec((4, HEAD_DIM), full),
            pl.BlockSpec((1, V_HEAD_DIM), full),
        ],
        out_specs=pl.BlockSpec((None, tq, LANES), qmap),
        out_shape=jax.ShapeDtypeStruct((b, s, D_MODEL), BF16),
        scratch_shapes=[
            pltpu.VMEM((2 * tq, LANES), BF16),
            pltpu.VMEM((2 * tq, LANES), F32),
            pltpu.VMEM((2 * tq, LANES), F32),
            pltpu.VMEM((2 * tq, LANES), F32),
        ],
        compiler_params=_params("parallel", "parallel", "parallel", "arbitrary"),
        name="diff_attention_online",
    )(q, k, v_aug, lam_vecs, subln)


def _attention(q, k, v_aug, q_gain, k_gain, lam_vecs, subln, lam_init):
    score_bound = (HEAD_DIM * Q_SCALE) * jnp.max(jnp.abs(q_gain)) * jnp.max(jnp.abs(k_gain))
    return lax.cond(score_bound < SAFE_LOG2_SCORE,
                    lambda *a: _attention_bounded(*a, lam_init),
                    lambda *a: _attention_online(*a, lam_init),
                    q, k, v_aug, lam_vecs, subln)


def _proj_kernel(h_ref, o_ref, w_ref, out_ref):
    out_ref[...] = h_ref[...] + _dot(o_ref[...], w_ref[...])


def _proj_residual(h2, o2, w_o, tm=512):
    tokens = h2.shape[0]
    tok = lambda i: (i, 0)
    return pl.pallas_call(
        _proj_kernel,
        grid=(tokens // tm,),
        in_specs=[
            pl.BlockSpec((tm, D_MODEL), tok),
            pl.BlockSpec((tm, D_MODEL), tok),
            pl.BlockSpec((D_MODEL, D_MODEL), lambda i: (0, 0)),
        ],
        out_specs=pl.BlockSpec((tm, D_MODEL), tok),
        out_shape=jax.ShapeDtypeStruct((tokens, D_MODEL), F32),
        compiler_params=_params("parallel"),
        name="attn_out_proj",
    )(h2, o2, w_o)


def _mlp_kernel(h_ref, g_ref, wu_ref, wd_ref, out_ref, xn_ref, acc_ref):
    f = pl.program_id(1)

    @pl.when(f == 0)
    def _():
        h = h_ref[...]
        xn_ref[...] = _rms(h, g_ref[...]).astype(BF16)
        acc_ref[...] = h

    a = jnp.maximum(_dot(xn_ref[...], wu_ref[...]), 0.0)
    acc_ref[...] += _dot((a * a).astype(BF16), wd_ref[...])

    @pl.when(f == pl.num_programs(1) - 1)
    def _():
        out_ref[...] = acc_ref[...]


def _mlp(h2, gain, w_up, w_down, tm=512, tf=1024):
    tokens = h2.shape[0]
    tok = lambda i, f: (i, 0)
    return pl.pallas_call(
        _mlp_kernel,
        grid=(tokens // tm, D_FF // tf),
        in_specs=[
            pl.BlockSpec((tm, D_MODEL), tok),
            pl.BlockSpec((1, D_MODEL), lambda i, f: (0, 0)),
            pl.BlockSpec((D_MODEL, tf), lambda i, f: (0, f)),
            pl.BlockSpec((tf, D_MODEL), lambda i, f: (f, 0)),
        ],
        out_specs=pl.BlockSpec((tm, D_MODEL), tok),
        out_shape=jax.ShapeDtypeStruct((tokens, D_MODEL), F32),
        scratch_shapes=[pltpu.VMEM((tm, D_MODEL), BF16), pltpu.VMEM((tm, D_MODEL), F32)],
        compiler_params=_params("parallel", "arbitrary"),
        name="relu2_mlp",
    )(h2, gain, w_up, w_down)


def _norm_kernel(h_ref, g_ref, u_ref):
    u_ref[...] = _rms(h_ref[...], g_ref[...]).astype(BF16)


def _norm(h2, gain, tm=1024):
    tokens = h2.shape[0]
    tok = lambda i: (i, 0)
    return pl.pallas_call(
        _norm_kernel,
        grid=(tokens // tm,),
        in_specs=[pl.BlockSpec((tm, D_MODEL), tok), pl.BlockSpec((1, D_MODEL), lambda i: (0, 0))],
        out_specs=pl.BlockSpec((tm, D_MODEL), tok),
        out_shape=jax.ShapeDtypeStruct((tokens, D_MODEL), BF16),
        compiler_params=_params("parallel"),
        name="ssm_pre_norm",
    )(h2, gain)


def _ssm_kernel(u_ref, e_ref, win_ref, wout_ref, al_ref, y_ref,
                t_ref, zs_ref, sfr_ref, sbr_ref, sfi_ref, sbi_ref, *, batch):
    rows, width = u_ref.shape
    n_chunks = rows // batch
    chunk = width // SSM_GROUP
    p = SSM_STATE

    e = e_ref[...]
    for j in range(chunk):
        off = SSM_GROUP * (chunk - 1 - j)
        t_ref[SSM_GROUP * j:SSM_GROUP * (j + 1), :] = e[:, off:off + width].astype(BF16)

    u = u_ref[...]
    zs_ref[...] = _dot(u, win_ref[...])

    ar = al_ref[0:1, :]
    ai = al_ref[1:2, :]
    is_fwd = lax.broadcasted_iota(jnp.int32, (batch, 2 * p), 1) < p

    def step(kk, carry):
        xr, xi = carry
        rf = pl.multiple_of(kk * batch, batch)
        rb = pl.multiple_of((n_chunks - 1 - kk) * batch, batch)
        lr = jnp.where(is_fwd, zs_ref[pl.ds(rf, batch), 0:2 * p], zs_ref[pl.ds(rb, batch), 0:2 * p])
        li = jnp.where(is_fwd, zs_ref[pl.ds(rf, batch), 2 * p:4 * p],
                       zs_ref[pl.ds(rb, batch), 2 * p:4 * p])
        sfr_ref[pl.ds(rf, batch), :] = xr
        sbr_ref[pl.ds(rb, batch), :] = xr
        sfi_ref[pl.ds(rf, batch), :] = xi
        sbi_ref[pl.ds(rb, batch), :] = xi
        return ar * xr - ai * xi + lr, ar * xi + ai * xr + li

    zero = jnp.zeros((batch, 2 * p), F32)
    lax.fori_loop(0, n_chunks, step, (zero, zero))

    fwd_lane = lax.broadcasted_iota(jnp.int32, (rows, 2 * p), 1) < p
    xin = jnp.concatenate(
        [jnp.where(fwd_lane, sfr_ref[...], sbr_ref[...]),
         jnp.where(fwd_lane, sfi_ref[...], sbi_ref[...])], axis=1).astype(BF16)
    y_ref[...] = _dot(u, t_ref[...]) + _dot(xin, wout_ref[...])


def _ssm(ug, e_rows, w_in, w_out, a_chunk, batch):
    g, rows, width = ug.shape
    p = SSM_STATE
    grp = lambda i: (i, 0, 0)
    return pl.pallas_call(
        functools.partial(_ssm_kernel, batch=batch),
        grid=(g,),
        in_specs=[
            pl.BlockSpec((None, rows, width), grp),
            pl.BlockSpec((None, SSM_GROUP, 2 * width), grp),
            pl.BlockSpec((None, width, 4 * p), grp),
            pl.BlockSpec((None, 4 * p, width), grp),
            pl.BlockSpec((None, 2, 2 * p), grp),
        ],
        out_specs=pl.BlockSpec((None, rows, width), grp),
        out_shape=jax.ShapeDtypeStruct((g, rows, width), F32),
        scratch_shapes=[
            pltpu.VMEM((width, width), BF16),
            pltpu.VMEM((rows, 4 * p), F32),
            pltpu.VMEM((rows, 2 * p), F32),
            pltpu.VMEM((rows, 2 * p), F32),
            pltpu.VMEM((rows, 2 * p), F32),
            pltpu.VMEM((rows, 2 * p), F32),
        ],
        compiler_params=_params("parallel"),
        name="s5_chunked_scan",
    )(ug, e_rows, w_in, w_out, a_chunk)


def _glu_kernel(h_ref, y_ref, g_ref, d_ref, w_ref, out_ref):
    h = h_ref[...]
    u = _rms(h, g_ref[...])
    gact = jax.nn.gelu(y_ref[...] + d_ref[...] * u).astype(BF16)
    z = _dot(gact, w_ref[...])
    out_ref[...] = h + z[:, 0:D_MODEL] * jax.nn.sigmoid(z[:, D_MODEL:2 * D_MODEL])


def _glu(h2, y2, gain, d_skip, w_glu, tm=512):
    tokens = h2.shape[0]
    tok = lambda i: (i, 0)
    vec = lambda i: (0, 0)
    return pl.pallas_call(
        _glu_kernel,
        grid=(tokens // tm,),
        in_specs=[
            pl.BlockSpec((tm, D_MODEL), tok),
            pl.BlockSpec((tm, D_MODEL), tok),
            pl.BlockSpec((1, D_MODEL), vec),
            pl.BlockSpec((1, D_MODEL), vec),
            pl.BlockSpec((D_MODEL, 2 * D_MODEL), vec),
        ],
        out_specs=pl.BlockSpec((tm, D_MODEL), tok),
        out_shape=jax.ShapeDtypeStruct((tokens, D_MODEL), F32),
        compiler_params=_params("parallel"),
        name="ssm_gelu_glu",
    )(h2, y2, gain, d_skip, w_glu)


def _ssm_operands(a_re, a_im, log_dt, b_re, b_im, c_re, c_im, chunk):
    hp = lax.Precision.HIGHEST
    dt = jnp.exp(log_dt)[..., None]
    zr, zi = a_re * dt, a_im * dt
    t = jnp.arange(chunk + 1, dtype=F32)[None, None, :, None]
    mag = jnp.exp(zr[:, :, None, :] * t)
    ang = zi[:, :, None, :] * t
    pr, pi = mag * jnp.cos(ang), mag * jnp.sin(ang)
    nr, ni = pr[:, :, 1, :] - 1.0, pi[:, :, 1, :]
    den = a_re * a_re + a_im * a_im
    qr = (nr * a_re + ni * a_im) / den
    qi = (ni * a_re - nr * a_im) / den
    bbr = qr[..., None] * b_re - qi[..., None] * b_im
    bbi = qr[..., None] * b_im + qi[..., None] * b_re

    car = c_re[:, :, None] * pr[:, :, :chunk, None, :] - c_im[:, :, None] * pi[:, :, :chunk, None, :]
    cai = c_re[:, :, None] * pi[:, :, :chunk, None, :] + c_im[:, :, None] * pr[:, :, :chunk, None, :]
    kern = (jnp.einsum('dgtop,dgpi->dgtoi', car, bbr, precision=hp)
            - jnp.einsum('dgtop,dgpi->dgtoi', cai, bbi, precision=hp))
    g = a_re.shape[1]
    krow = kern.transpose(0, 1, 4, 2, 3)
    center = (krow[0, :, :, 0] + krow[1, :, :, 0])[:, :, None]
    e_rows = jnp.concatenate(
        [krow[1, :, :, :0:-1], center, krow[0, :, :, 1:],
         jnp.zeros((g, SSM_GROUP, 1, SSM_GROUP), F32)], axis=2)
    e_rows = e_rows.reshape(g, SSM_GROUP, 2 * chunk * SSM_GROUP)

    pw_r = jnp.stack([pr[0, :, chunk - 1::-1][:, :chunk], pr[1, :, :chunk]])
    pw_i = jnp.stack([pi[0, :, chunk - 1::-1][:, :chunk], pi[1, :, :chunk]])
    bt_r = bbr.transpose(0, 1, 3, 2)[:, :, None]
    bt_i = bbi.transpose(0, 1, 3, 2)[:, :, None]
    wi_r = pw_r[:, :, :, None, :] * bt_r - pw_i[:, :, :, None, :] * bt_i
    wi_i = pw_r[:, :, :, None, :] * bt_i + pw_i[:, :, :, None, :] * bt_r
    w_in = jnp.concatenate([wi_r[0], wi_r[1], wi_i[0], wi_i[1]], axis=-1)
    w_in = w_in.reshape(g, chunk * SSM_GROUP, 4 * SSM_STATE)

    qw_r = jnp.stack([pr[0, :, 1:], pr[1, :, :0:-1]])
    qw_i = jnp.stack([pi[0, :, 1:], pi[1, :, :0:-1]])
    wo_r = c_re[:, :, None] * qw_r[:, :, :, None, :] - c_im[:, :, None] * qw_i[:, :, :, None, :]
    wo_i = c_re[:, :, None] * qw_i[:, :, :, None, :] + c_im[:, :, None] * qw_r[:, :, :, None, :]
    w_out = jnp.concatenate([wo_r[0], wo_r[1], -wo_i[0], -wo_i[1]], axis=-1)
    w_out = w_out.reshape(g, chunk * SSM_GROUP, 4 * SSM_STATE).transpose(0, 2, 1)

    a_chunk = jnp.stack([jnp.concatenate([pr[0, :, chunk], pr[1, :, chunk]], axis=-1),
                         jnp.concatenate([pi[0, :, chunk], pi[1, :, chunk]], axis=-1)], axis=1)
    return e_rows, w_in.astype(BF16), w_out.astype(BF16), a_chunk


def _ssm_layer(h2, batch, seq, gain, a_re, a_im, log_dt, b_re, b_im, c_re, c_im, d_skip, w_glu):
    chunk = SSM_CHUNK
    n_chunks = seq // chunk
    e_rows, w_in, w_out, a_chunk = _ssm_operands(a_re, a_im, log_dt, b_re, b_im, c_re, c_im, chunk)
    u = _norm(h2, gain)
    ug = u.reshape(batch, n_chunks, chunk, N_GROUPS, SSM_GROUP).transpose(3, 1, 0, 2, 4)
    ug = ug.reshape(N_GROUPS, n_chunks * batch, chunk * SSM_GROUP)
    yg = _ssm(ug, e_rows, w_in, w_out, a_chunk, batch)
    y = yg.reshape(N_GROUPS, n_chunks, batch, chunk, SSM_GROUP).transpose(2, 1, 3, 0, 4)
    y2 = y.reshape(batch * seq, D_MODEL)
    return _glu(h2, y2, gain, d_skip, w_glu)


def _rope_tables(seq):
    pos = jnp.arange(seq, dtype=F32)
    inv_freq = ROPE_THETA ** (-jnp.arange(0, HEAD_DIM, 2, dtype=F32) / HEAD_DIM)
    ang = pos[:, None] * inv_freq[None, :]
    cos, sin = jnp.cos(ang), jnp.sin(ang)
    reps = LANES // HEAD_DIM
    cos_t = jnp.tile(jnp.concatenate([cos, cos], axis=-1), (1, reps))
    sin_t = jnp.tile(jnp.concatenate([-sin, sin], axis=-1), (1, reps))
    return cos_t, sin_t


def _lambda_init(layer_idx):
    return 0.8 - 0.6 * math.exp(-0.3 * layer_idx)


def kernel(x, norm_mix, norm_ffn, attn_w_qkv, attn_q_gain, attn_k_gain, attn_lambda,
           attn_subln, attn_w_o, ssm_a_re, ssm_a_im, ssm_log_dt, ssm_b_re, ssm_b_im,
           ssm_c_re, ssm_c_im, ssm_d, ssm_w_glu, ffn_w_up, ffn_w_down):
    batch, seq, d = x.shape
    assert d == D_MODEL
    depth = norm_mix.shape[0]
    cos_t, sin_t = _rope_tables(seq)
    reps = LANES // HEAD_DIM
    h2 = x.reshape(batch * seq, D_MODEL)
    for i in range(depth):
        j = i // N_MIXERS
        gain = norm_mix[i][None, :]
        if i % N_MIXERS == 0:
            qg, kg = attn_q_gain[j], attn_k_gain[j]
            q, k, v_aug = _qkv(h2, gain, attn_w_qkv[j].astype(BF16), jnp.tile(qg, reps)[None, :],
                               jnp.tile(kg, reps)[None, :], cos_t, sin_t, seq)
            o = _attention(q.reshape(batch, seq, D_MODEL), k.reshape(batch, seq, D_MODEL),
                           v_aug.reshape(batch, seq, 2 * D_MODEL), qg, kg,
                           attn_lambda[j], attn_subln[j][None, :], _lambda_init(i))
            h2 = _proj_residual(h2, o.reshape(batch * seq, D_MODEL), attn_w_o[j].astype(BF16))
        else:
            h2 = _ssm_layer(h2, batch, seq, gain, ssm_a_re[j], ssm_a_im[j], ssm_log_dt[j],
                            ssm_b_re[j], ssm_b_im[j], ssm_c_re[j], ssm_c_im[j],
                            ssm_d[j][None, :], ssm_w_glu[j].astype(BF16))
        h2 = _mlp(h2, norm_ffn[i][None, :], ffn_w_up[i].astype(BF16), ffn_w_down[i].astype(BF16))
    return h2.reshape(batch, seq, D_MODEL)
```

```python
import functools
import math

import jax
import jax.numpy as jnp
from jax import lax
from jax.experimental import pallas as pl
from jax.experimental.pallas import tpu as pltpu

F32 = jnp.float32
BF16 = jnp.bfloat16

D_MODEL = 1024
N_HEADS = 8
HEAD_DIM = 64
V_HEAD_DIM = 2 * HEAD_DIM
V_ROWS = V_HEAD_DIM + 16
ROPE_THETA = 10000.0
SSM_GROUP = 16
N_GROUPS = D_MODEL // SSM_GROUP
SSM_STATE = 64
D_FF = 4 * D_MODEL
RMS_EPS = 1e-6
Q_SCALE = HEAD_DIM ** -0.5 * math.log2(math.e)
SAFE_LOG2_SCORE = 80.0
N_MIXERS = 2

LANES = 128
SSM_CHUNK = 64
VMEM_LIMIT = 48 * 1024 * 1024


def _params(*sem):
    return pltpu.CompilerParams(dimension_semantics=sem, vmem_limit_bytes=VMEM_LIMIT)


def _rms(x, gain):
    ms = jnp.mean(x * x, axis=-1, keepdims=True)
    return x * lax.rsqrt(ms + RMS_EPS) * gain


def _dot(a, b):
    return jnp.dot(a, b, preferred_element_type=F32)


def _qkv_kernel(x_ref, g_ref, wqk_ref, wvt_ref, qg_ref, kg_ref, cos_ref, sin_ref,
                q_ref, k_ref, vt_ref):
    tm = x_ref.shape[0]
    xn = _rms(x_ref[...], g_ref[...]).astype(BF16)
    row = lax.broadcasted_iota(jnp.int32, (LANES, LANES), 0)
    col = lax.broadcasted_iota(jnp.int32, (LANES, LANES), 1)
    seg_mean = jnp.where(row // HEAD_DIM == col // HEAD_DIM, 1.0 / HEAD_DIM, 0.0).astype(BF16)
    lane = lax.broadcasted_iota(jnp.int32, (tm, LANES), 1)
    first_half = (lane % HEAD_DIM) < (HEAD_DIM // 2)
    cos = cos_ref[...]
    sin = sin_ref[...]

    def norm_rope(y, gain, scale):
        sq = y * y
        hi = sq.astype(BF16)
        lo = (sq - hi.astype(F32)).astype(BF16)
        ms = _dot(hi, seg_mean) + _dot(lo, seg_mean)
        yn = y * lax.rsqrt(ms + RMS_EPS) * gain
        partner = jnp.where(first_half,
                            pltpu.roll(yn, LANES - HEAD_DIM // 2, 1),
                            pltpu.roll(yn, HEAD_DIM // 2, 1))
        return (yn * cos + partner * sin) * scale

    yq = _dot(xn, wqk_ref[:, 0:D_MODEL])
    for t in range(D_MODEL // LANES):
        sl = slice(t * LANES, (t + 1) * LANES)
        q_ref[:, sl] = norm_rope(yq[:, sl], qg_ref[...], Q_SCALE).astype(BF16)
    yk = _dot(xn, wqk_ref[:, D_MODEL:2 * D_MODEL])
    for t in range(D_MODEL // LANES):
        sl = slice(t * LANES, (t + 1) * LANES)
        k_ref[:, sl] = norm_rope(yk[:, sl], kg_ref[...], 1.0).astype(BF16)
    yvt = lax.dot_general(wvt_ref[...], xn, (((1,), (1,)), ((), ())),
                          preferred_element_type=F32)
    pad_rows = lax.broadcasted_iota(jnp.int32, (V_ROWS - V_HEAD_DIM, tm), 0)
    ones_row = jnp.where(pad_rows == 0, 1.0, 0.0).astype(BF16)
    for t in range(N_HEADS):
        vt_ref[t, 0:V_HEAD_DIM, :] = yvt[t * V_HEAD_DIM:(t + 1) * V_HEAD_DIM, :].astype(BF16)
        vt_ref[t, V_HEAD_DIM:V_ROWS, :] = ones_row


def _qkv(h2, gain, w_qk, w_v_t, q_gain, k_gain, cos_t, sin_t, seq, tm=512):
    tokens = h2.shape[0]
    tm = min(tm, seq)
    assert tokens % tm == 0 and seq % tm == 0
    n_pos_blocks = seq // tm
    tok = lambda i: (i, 0)
    full = lambda i: (0, 0)
    pos = lambda i: (i % n_pos_blocks, 0)
    vt_map = lambda i: (i // n_pos_blocks, 0, i % n_pos_blocks, 0, 0)
    out = jax.ShapeDtypeStruct((tokens, D_MODEL), BF16)
    vt_shape = (tokens // seq, N_HEADS, n_pos_blocks, V_ROWS, tm)
    return pl.pallas_call(
        _qkv_kernel,
        grid=(tokens // tm,),
        in_specs=[
            pl.BlockSpec((tm, D_MODEL), tok),
            pl.BlockSpec((1, D_MODEL), full),
            pl.BlockSpec((D_MODEL, 2 * D_MODEL), full),
            pl.BlockSpec((D_MODEL, D_MODEL), full),
            pl.BlockSpec((1, LANES), full),
            pl.BlockSpec((1, LANES), full),
            pl.BlockSpec((tm, LANES), pos),
            pl.BlockSpec((tm, LANES), pos),
        ],
        out_specs=[pl.BlockSpec((tm, D_MODEL), tok), pl.BlockSpec((tm, D_MODEL), tok),
                   pl.BlockSpec((None, N_HEADS, None, V_ROWS, tm), vt_map)],
        out_shape=[out, out, jax.ShapeDtypeStruct(vt_shape, BF16)],
        compiler_params=_params("parallel"),
        name="qkv_norm_rope",
    )(h2, gain, w_qk, w_v_t, q_gain, k_gain, cos_t, sin_t)


def _stack_q(q_ref, qs_ref):
    tq = q_ref.shape[0]
    q = q_ref[...]
    lane = lax.broadcasted_iota(jnp.int32, q.shape, 1)
    zero = jnp.zeros_like(q)
    qs_ref[0:tq, :] = jnp.where(lane < HEAD_DIM, q, zero)
    qs_ref[tq:2 * tq, :] = jnp.where(lane >= HEAD_DIM, q, zero)


def _attn_kernel(q_ref, k_ref, vt_ref, lam_ref, sg_ref, o_ref, qs_ref, acc_ref, *m_scratch,
                 lam_init, subtract_max):
    tq = q_ref.shape[0]
    n_chunks, _, tk = vt_ref.shape
    _stack_q(q_ref, qs_ref)
    acc_ref[...] = jnp.zeros(acc_ref.shape, F32)
    if subtract_max:
        m_ref, = m_scratch
        m_ref[...] = jnp.full(m_ref.shape, -jnp.inf, F32)

    def chunk(c, carry):
        off = pl.multiple_of(c * tk, tk)
        st = lax.dot_general(k_ref[pl.ds(off, tk), :], qs_ref[...], (((1,), (1,)), ((), ())),
                             preferred_element_type=F32)
        if subtract_max:
            m_prev = m_ref[...]
            m_new = jnp.maximum(m_prev, jnp.max(st, axis=0, keepdims=True))
            acc_ref[...] = acc_ref[...] * jnp.exp2(m_prev - m_new)
            m_ref[...] = m_new
            st = st - m_new
        acc_ref[...] += _dot(vt_ref[c], jnp.exp2(st).astype(BF16))
        return carry

    lax.fori_loop(0, n_chunks, chunk, 0, unroll=1 if subtract_max else 8)

    ot = acc_ref[0:V_HEAD_DIM, :] / acc_ref[V_HEAD_DIM:V_HEAD_DIM + 1, :]
    lv = lam_ref[...]
    lam = (jnp.exp(jnp.sum(lv[0:1] * lv[1:2], axis=1, keepdims=True))
           - jnp.exp(jnp.sum(lv[2:3] * lv[3:4], axis=1, keepdims=True)) + lam_init)
    od = ot[:, 0:tq] - lam * ot[:, tq:2 * tq]
    ms = jnp.mean(od * od, axis=0, keepdims=True)
    on = od * lax.rsqrt(ms + RMS_EPS) * sg_ref[...] * (1.0 - lam_init)
    o_ref[...] = on.T.astype(BF16)


def _attention_call(q, k, vt, lam_vecs, subln_col, lam_init, subtract_max, tq=512):
    b, s, _ = q.shape
    tq = min(tq, s)
    assert s % tq == 0
    n_chunks, tk = vt.shape[2], vt.shape[4]
    qmap = lambda bi, hi, qi: (bi, qi, hi)
    full = lambda bi, hi, qi: (0, 0)
    scratch = [pltpu.VMEM((2 * tq, LANES), BF16), pltpu.VMEM((V_ROWS, 2 * tq), F32)]
    if subtract_max:
        scratch.append(pltpu.VMEM((1, 2 * tq), F32))
    return pl.pallas_call(
        functools.partial(_attn_kernel, lam_init=lam_init, subtract_max=subtract_max),
        grid=(b, N_HEADS, s // tq),
        in_specs=[
            pl.BlockSpec((None, tq, LANES), qmap),
            pl.BlockSpec((None, s, LANES), lambda bi, hi, qi: (bi, 0, hi)),
            pl.BlockSpec((None, None, n_chunks, V_ROWS, tk), lambda bi, hi, qi: (bi, hi, 0, 0, 0)),
            pl.BlockSpec((4, HEAD_DIM), full),
            pl.BlockSpec((V_HEAD_DIM, 1), full),
        ],
        out_specs=pl.BlockSpec((None, tq, LANES), qmap),
        out_shape=jax.ShapeDtypeStruct((b, s, D_MODEL), BF16),
        scratch_shapes=scratch,
        compiler_params=_params("parallel", "parallel", "parallel"),
        name="diff_attention_online" if subtract_max else "diff_attention_bounded",
    )(q, k, vt, lam_vecs, subln_col)


def _attention(q, k, vt, q_gain, k_gain, lam_vecs, subln_col, lam_init):
    score_bound = (HEAD_DIM * Q_SCALE) * jnp.max(jnp.abs(q_gain)) * jnp.max(jnp.abs(k_gain))
    return lax.cond(score_bound < SAFE_LOG2_SCORE,
                    lambda *a: _attention_call(*a, lam_init, False),
                    lambda *a: _attention_call(*a, lam_init, True),
                    q, k, vt, lam_vecs, subln_col)


def _proj_kernel(h_ref, o_ref, w_ref, out_ref):
    out_ref[...] = h_ref[...] + _dot(o_ref[...], w_ref[...])


def _proj_residual(h2, o2, w_o, tm=512):
    tokens = h2.shape[0]
    tok = lambda i: (i, 0)
    return pl.pallas_call(
        _proj_kernel,
        grid=(tokens // tm,),
        in_specs=[
            pl.BlockSpec((tm, D_MODEL), tok),
            pl.BlockSpec((tm, D_MODEL), tok),
            pl.BlockSpec((D_MODEL, D_MODEL), lambda i: (0, 0)),
        ],
        out_specs=pl.BlockSpec((tm, D_MODEL), tok),
        out_shape=jax.ShapeDtypeStruct((tokens, D_MODEL), F32),
        compiler_params=_params("parallel"),
        name="attn_out_proj",
    )(h2, o2, w_o)


def _mlp_kernel(h_ref, g_ref, wu_ref, wd_ref, out_ref, xn_ref, acc_ref):
    f = pl.program_id(1)

    @pl.when(f == 0)
    def _():
        h = h_ref[...]
        xn_ref[...] = _rms(h, g_ref[...]).astype(BF16)
        acc_ref[...] = h

    a = jnp.maximum(_dot(xn_ref[...], wu_ref[...]), 0.0)
    acc_ref[...] += _dot((a * a).astype(BF16), wd_ref[...])

    @pl.when(f == pl.num_programs(1) - 1)
    def _():
        out_ref[...] = acc_ref[...]


def _mlp(h2, gain, w_up, w_down, tm=512, tf=1024):
    tokens = h2.shape[0]
    tok = lambda i, f: (i, 0)
    return pl.pallas_call(
        _mlp_kernel,
        grid=(tokens // tm, D_FF // tf),
        in_specs=[
            pl.BlockSpec((tm, D_MODEL), tok),
            pl.BlockSpec((1, D_MODEL), lambda i, f: (0, 0)),
            pl.BlockSpec((D_MODEL, tf), lambda i, f: (0, f)),
            pl.BlockSpec((tf, D_MODEL), lambda i, f: (f, 0)),
        ],
        out_specs=pl.BlockSpec((tm, D_MODEL), tok),
        out_shape=jax.ShapeDtypeStruct((tokens, D_MODEL), F32),
        scratch_shapes=[pltpu.VMEM((tm, D_MODEL), BF16), pltpu.VMEM((tm, D_MODEL), F32)],
        compiler_params=_params("parallel", "arbitrary"),
        name="relu2_mlp",
    )(h2, gain, w_up, w_down)


def _norm_kernel(h_ref, g_ref, u_ref):
    u_ref[...] = _rms(h_ref[...], g_ref[...]).astype(BF16)


def _norm(h2, gain, tm=1024):
    tokens = h2.shape[0]
    tok = lambda i: (i, 0)
    return pl.pallas_call(
        _norm_kernel,
        grid=(tokens // tm,),
        in_specs=[pl.BlockSpec((tm, D_MODEL), tok), pl.BlockSpec((1, D_MODEL), lambda i: (0, 0))],
        out_specs=pl.BlockSpec((tm, D_MODEL), tok),
        out_shape=jax.ShapeDtypeStruct((tokens, D_MODEL), BF16),
        compiler_params=_params("parallel"),
        name="ssm_pre_norm",
    )(h2, gain)


def _ssm_kernel(u_ref, e_ref, win_ref, wout_ref, al_ref, y_ref,
                t_ref, zs_ref, sfr_ref, sbr_ref, sfi_ref, sbi_ref, *, batch):
    rows, width = u_ref.shape
    n_chunks = rows // batch
    chunk = width // SSM_GROUP
    p = SSM_STATE

    e = e_ref[...]
    for j in range(chunk):
        off = SSM_GROUP * (chunk - 1 - j)
        t_ref[SSM_GROUP * j:SSM_GROUP * (j + 1), :] = e[:, off:off + width].astype(BF16)

    u = u_ref[...]
    zs_ref[...] = _dot(u, win_ref[...])

    ar = al_ref[0:1, :]
    ai = al_ref[1:2, :]
    is_fwd = lax.broadcasted_iota(jnp.int32, (batch, 2 * p), 1) < p

    def step(kk, carry):
        xr, xi = carry
        rf = pl.multiple_of(kk * batch, batch)
        rb = pl.multiple_of((n_chunks - 1 - kk) * batch, batch)
        lr = jnp.where(is_fwd, zs_ref[pl.ds(rf, batch), 0:2 * p], zs_ref[pl.ds(rb, batch), 0:2 * p])
        li = jnp.where(is_fwd, zs_ref[pl.ds(rf, batch), 2 * p:4 * p],
                       zs_ref[pl.ds(rb, batch), 2 * p:4 * p])
        sfr_ref[pl.ds(rf, batch), :] = xr
        sbr_ref[pl.ds(rb, batch), :] = xr
        sfi_ref[pl.ds(rf, batch), :] = xi
        sbi_ref[pl.ds(rb, batch), :] = xi
        return ar * xr - ai * xi + lr, ar * xi + ai * xr + li

    zero = jnp.zeros((batch, 2 * p), F32)
    lax.fori_loop(0, n_chunks, step, (zero, zero))

    fwd_lane = lax.broadcasted_iota(jnp.int32, (rows, 2 * p), 1) < p
    xin = jnp.concatenate(
        [jnp.where(fwd_lane, sfr_ref[...], sbr_ref[...]),
         jnp.where(fwd_lane, sfi_ref[...], sbi_ref[...])], axis=1).astype(BF16)
    y_ref[...] = _dot(u, t_ref[...]) + _dot(xin, wout_ref[...])


def _ssm(ug, e_rows, w_in, w_out, a_chunk, batch):
    g, rows, width = ug.shape
    p = SSM_STATE
    grp = lambda i: (i, 0, 0)
    return pl.pallas_call(
        functools.partial(_ssm_kernel, batch=batch),
        grid=(g,),
        in_specs=[
            pl.BlockSpec((None, rows, width), grp),
            pl.BlockSpec((None, SSM_GROUP, 2 * width), grp),
            pl.BlockSpec((None, width, 4 * p), grp),
            pl.BlockSpec((None, 4 * p, width), grp),
            pl.BlockSpec((None, 2, 2 * p), grp),
        ],
        out_specs=pl.BlockSpec((None, rows, width), grp),
        out_shape=jax.ShapeDtypeStruct((g, rows, width), F32),
        scratch_shapes=[
            pltpu.VMEM((width, width), BF16),
            pltpu.VMEM((rows, 4 * p), F32),
            pltpu.VMEM((rows, 2 * p), F32),
            pltpu.VMEM((rows, 2 * p), F32),
            pltpu.VMEM((rows, 2 * p), F32),
            pltpu.VMEM((rows, 2 * p), F32),
        ],
        compiler_params=_params("parallel"),
        name="s5_chunked_scan",
    )(ug, e_rows, w_in, w_out, a_chunk)


def _glu_kernel(h_ref, y_ref, g_ref, d_ref, w_ref, out_ref):
    h = h_ref[...]
    u = _rms(h, g_ref[...])
    gact = jax.nn.gelu(y_ref[...] + d_ref[...] * u).astype(BF16)
    z = _dot(gact, w_ref[...])
    out_ref[...] = h + z[:, 0:D_MODEL] * jax.nn.sigmoid(z[:, D_MODEL:2 * D_MODEL])


def _glu(h2, y2, gain, d_skip, w_glu, tm=512):
    tokens = h2.shape[0]
    tok = lambda i: (i, 0)
    vec = lambda i: (0, 0)
    return pl.pallas_call(
        _glu_kernel,
        grid=(tokens // tm,),
        in_specs=[
            pl.BlockSpec((tm, D_MODEL), tok),
            pl.BlockSpec((tm, D_MODEL), tok),
            pl.BlockSpec((1, D_MODEL), vec),
            pl.BlockSpec((1, D_MODEL), vec),
            pl.BlockSpec((D_MODEL, 2 * D_MODEL), vec),
        ],
        out_specs=pl.BlockSpec((tm, D_MODEL), tok),
        out_shape=jax.ShapeDtypeStruct((tokens, D_MODEL), F32),
        compiler_params=_params("parallel"),
        name="ssm_gelu_glu",
    )(h2, y2, gain, d_skip, w_glu)


def _ssm_operands(a_re, a_im, log_dt, b_re, b_im, c_re, c_im, chunk):
    hp = lax.Precision.HIGHEST
    dt = jnp.exp(log_dt)[..., None]
    zr, zi = a_re * dt, a_im * dt
    t = jnp.arange(chunk + 1, dtype=F32)[None, None, :, None]
    mag = jnp.exp(zr[:, :, None, :] * t)
    ang = zi[:, :, None, :] * t
    pr, pi = mag * jnp.cos(ang), mag * jnp.sin(ang)
    nr, ni = pr[:, :, 1, :] - 1.0, pi[:, :, 1, :]
    den = a_re * a_re + a_im * a_im
    qr = (nr * a_re + ni * a_im) / den
    qi = (ni * a_re - nr * a_im) / den
    bbr = qr[..., None] * b_re - qi[..., None] * b_im
    bbi = qr[..., None] * b_im + qi[..., None] * b_re

    car = c_re[:, :, None] * pr[:, :, :chunk, None, :] - c_im[:, :, None] * pi[:, :, :chunk, None, :]
    cai = c_re[:, :, None] * pi[:, :, :chunk, None, :] + c_im[:, :, None] * pr[:, :, :chunk, None, :]
    kern = (jnp.einsum('dgtop,dgpi->dgtoi', car, bbr, precision=hp)
            - jnp.einsum('dgtop,dgpi->dgtoi', cai, bbi, precision=hp))
    g = a_re.shape[1]
    krow = kern.transpose(0, 1, 4, 2, 3)
    center = (krow[0, :, :, 0] + krow[1, :, :, 0])[:, :, None]
    e_rows = jnp.concatenate(
        [krow[1, :, :, :0:-1], center, krow[0, :, :, 1:],
         jnp.zeros((g, SSM_GROUP, 1, SSM_GROUP), F32)], axis=2)
    e_rows = e_rows.reshape(g, SSM_GROUP, 2 * chunk * SSM_GROUP)

    pw_r = jnp.stack([pr[0, :, chunk - 1::-1][:, :chunk], pr[1, :, :chunk]])
    pw_i = jnp.stack([pi[0, :, chunk - 1::-1][:, :chunk], pi[1, :, :chunk]])
    bt_r = bbr.transpose(0, 1, 3, 2)[:, :, None]
    bt_i = bbi.transpose(0, 1, 3, 2)[:, :, None]
    wi_r = pw_r[:, :, :, None, :] * bt_r - pw_i[:, :, :, None, :] * bt_i
    wi_i = pw_r[:, :, :, None, :] * bt_i + pw_i[:, :, :, None, :] * bt_r
    w_in = jnp.concatenate([wi_r[0], wi_r[1], wi_i[0], wi_i[1]], axis=-1)
    w_in = w_in.reshape(g, chunk * SSM_GROUP, 4 * SSM_STATE)

    qw_r = jnp.stack([pr[0, :, 1:], pr[1, :, :0:-1]])
    qw_i = jnp.stack([pi[0, :, 1:], pi[1, :, :0:-1]])
    wo_r = c_re[:, :, None] * qw_r[:, :, :, None, :] - c_im[:, :, None] * qw_i[:, :, :, None, :]
    wo_i = c_re[:, :, None] * qw_i[:, :, :, None, :] + c_im[:, :, None] * qw_r[:, :, :, None, :]
    w_out = jnp.concatenate([wo_r[0], wo_r[1], -wo_i[0], -wo_i[1]], axis=-1)
    w_out = w_out.reshape(g, chunk * SSM_GROUP, 4 * SSM_STATE).transpose(0, 2, 1)

    a_chunk = jnp.stack([jnp.concatenate([pr[0, :, chunk], pr[1, :, chunk]], axis=-1),
                         jnp.concatenate([pi[0, :, chunk], pi[1, :, chunk]], axis=-1)], axis=1)
    return e_rows, w_in.astype(BF16), w_out.astype(BF16), a_chunk


def _ssm_layer(h2, batch, seq, gain, a_re, a_im, log_dt, b_re, b_im, c_re, c_im, d_skip, w_glu):
    chunk = SSM_CHUNK
    n_chunks = seq // chunk
    e_rows, w_in, w_out, a_chunk = _ssm_operands(a_re, a_im, log_dt, b_re, b_im, c_re, c_im, chunk)
    u = _norm(h2, gain)
    ug = u.reshape(batch, n_chunks, chunk, N_GROUPS, SSM_GROUP).transpose(3, 1, 0, 2, 4)
    ug = ug.reshape(N_GROUPS, n_chunks * batch, chunk * SSM_GROUP)
    yg = _ssm(ug, e_rows, w_in, w_out, a_chunk, batch)
    y = yg.reshape(N_GROUPS, n_chunks, batch, chunk, SSM_GROUP).transpose(2, 1, 3, 0, 4)
    y2 = y.reshape(batch * seq, D_MODEL)
    return _glu(h2, y2, gain, d_skip, w_glu)


def _rope_tables(seq):
    pos = jnp.arange(seq, dtype=F32)
    inv_freq = ROPE_THETA ** (-jnp.arange(0, HEAD_DIM, 2, dtype=F32) / HEAD_DIM)
    ang = pos[:, None] * inv_freq[None, :]
    cos, sin = jnp.cos(ang), jnp.sin(ang)
    reps = LANES // HEAD_DIM
    cos_t = jnp.tile(jnp.concatenate([cos, cos], axis=-1), (1, reps))
    sin_t = jnp.tile(jnp.concatenate([-sin, sin], axis=-1), (1, reps))
    return cos_t, sin_t


def _lambda_init(layer_idx):
    return 0.8 - 0.6 * math.exp(-0.3 * layer_idx)


def kernel(x, norm_mix, norm_ffn, attn_w_qkv, attn_q_gain, attn_k_gain, attn_lambda,
           attn_subln, attn_w_o, ssm_a_re, ssm_a_im, ssm_log_dt, ssm_b_re, ssm_b_im,
           ssm_c_re, ssm_c_im, ssm_d, ssm_w_glu, ffn_w_up, ffn_w_down):
    batch, seq, d = x.shape
    assert d == D_MODEL
    depth = norm_mix.shape[0]
    cos_t, sin_t = _rope_tables(seq)
    reps = LANES // HEAD_DIM
    h2 = x.reshape(batch * seq, D_MODEL)
    for i in range(depth):
        j = i // N_MIXERS
        gain = norm_mix[i][None, :]
        if i % N_MIXERS == 0:
            qg, kg = attn_q_gain[j], attn_k_gain[j]
            w_qk = attn_w_qkv[j][:, :2 * D_MODEL].astype(BF16)
            w_v_t = attn_w_qkv[j][:, 2 * D_MODEL:].T.astype(BF16)
            q, k, vt = _qkv(h2, gain, w_qk, w_v_t, jnp.tile(qg, reps)[None, :],
                            jnp.tile(kg, reps)[None, :], cos_t, sin_t, seq)
            o = _attention(q.reshape(batch, seq, D_MODEL), k.reshape(batch, seq, D_MODEL), vt, qg, kg,
                           attn_lambda[j], attn_subln[j][:, None], _lambda_init(i))
            h2 = _proj_residual(h2, o.reshape(batch * seq, D_MODEL), attn_w_o[j].astype(BF16))
        else:
            h2 = _ssm_layer(h2, batch, seq, gain, ssm_a_re[j], ssm_a_im[j], ssm_log_dt[j],
                            ssm_b_re[j], ssm_b_im[j], ssm_c_re[j], ssm_c_im[j],
                            ssm_d[j][None, :], ssm_w_glu[j].astype(BF16))
        h2 = _mlp(h2, norm_ffn[i][None, :], ffn_w_up[i].astype(BF16), ffn_w_down[i].astype(BF16))
    return h2.reshape(batch, seq, D_MODEL)
```

```python
import functools
import math

import jax
import jax.numpy as jnp
from jax import lax
from jax.experimental import pallas as pl
from jax.experimental.pallas import tpu as pltpu

F32 = jnp.float32
BF16 = jnp.bfloat16

D_MODEL = 1024
N_HEADS = 8
HEAD_DIM = 64
V_HEAD_DIM = 2 * HEAD_DIM
V_ROWS = V_HEAD_DIM + 16
ROPE_THETA = 10000.0
SSM_GROUP = 16
N_GROUPS = D_MODEL // SSM_GROUP
SSM_STATE = 64
D_FF = 4 * D_MODEL
RMS_EPS = 1e-6
Q_SCALE = HEAD_DIM ** -0.5 * math.log2(math.e)
SAFE_LOG2_SCORE = 80.0
N_MIXERS = 2

LANES = 128
SSM_CHUNK = 64
VMEM_LIMIT = 48 * 1024 * 1024


def _params(*sem):
    return pltpu.CompilerParams(dimension_semantics=sem, vmem_limit_bytes=VMEM_LIMIT)


def _rms(x, gain):
    ms = jnp.mean(x * x, axis=-1, keepdims=True)
    return x * lax.rsqrt(ms + RMS_EPS) * gain


def _dot(a, b):
    return jnp.dot(a, b, preferred_element_type=F32)


def _qkv_kernel(x_ref, g_ref, wqk_ref, wvt_ref, qg_ref, kg_ref, cos_ref, sin_ref,
                q_ref, k_ref, vt_ref):
    tm = x_ref.shape[0]
    xn = _rms(x_ref[...], g_ref[...]).astype(BF16)
    row = lax.broadcasted_iota(jnp.int32, (LANES, LANES), 0)
    col = lax.broadcasted_iota(jnp.int32, (LANES, LANES), 1)
    seg_mean = jnp.where(row // HEAD_DIM == col // HEAD_DIM, 1.0 / HEAD_DIM, 0.0).astype(BF16)
    lane = lax.broadcasted_iota(jnp.int32, (tm, LANES), 1)
    first_half = (lane % HEAD_DIM) < (HEAD_DIM // 2)
    cos = cos_ref[...]
    sin = sin_ref[...]

    def norm_rope(y, gain, scale):
        sq = y * y
        hi = sq.astype(BF16)
        lo = (sq - hi.astype(F32)).astype(BF16)
        ms = _dot(hi, seg_mean) + _dot(lo, seg_mean)
        yn = y * lax.rsqrt(ms + RMS_EPS) * gain
        partner = jnp.where(first_half,
                            pltpu.roll(yn, LANES - HEAD_DIM // 2, 1),
                            pltpu.roll(yn, HEAD_DIM // 2, 1))
        return (yn * cos + partner * sin) * scale

    yq = _dot(xn, wqk_ref[:, 0:D_MODEL])
    for t in range(D_MODEL // LANES):
        sl = slice(t * LANES, (t + 1) * LANES)
        q_ref[:, sl] = norm_rope(yq[:, sl], qg_ref[...], Q_SCALE).astype(BF16)
    yk = _dot(xn, wqk_ref[:, D_MODEL:2 * D_MODEL])
    for t in range(D_MODEL // LANES):
        sl = slice(t * LANES, (t + 1) * LANES)
        k_ref[:, sl] = norm_rope(yk[:, sl], kg_ref[...], 1.0).astype(BF16)
    yvt = lax.dot_general(wvt_ref[...], xn, (((1,), (1,)), ((), ())),
                          preferred_element_type=F32)
    pad_rows = lax.broadcasted_iota(jnp.int32, (V_ROWS - V_HEAD_DIM, tm), 0)
    ones_row = jnp.where(pad_rows == 0, 1.0, 0.0).astype(BF16)
    for t in range(N_HEADS):
        vt_ref[t, 0:V_HEAD_DIM, :] = yvt[t * V_HEAD_DIM:(t + 1) * V_HEAD_DIM, :].astype(BF16)
        vt_ref[t, V_HEAD_DIM:V_ROWS, :] = ones_row


def _qkv(h2, gain, w_qk, w_v_t, q_gain, k_gain, cos_t, sin_t, seq, tm=512):
    tokens = h2.shape[0]
    tm = min(tm, seq)
    assert tokens % tm == 0 and seq % tm == 0
    n_pos_blocks = seq // tm
    tok = lambda i: (i, 0)
    full = lambda i: (0, 0)
    pos = lambda i: (i % n_pos_blocks, 0)
    vt_map = lambda i: (i // n_pos_blocks, 0, i % n_pos_blocks, 0, 0)
    out = jax.ShapeDtypeStruct((tokens, D_MODEL), BF16)
    vt_shape = (tokens // seq, N_HEADS, n_pos_blocks, V_ROWS, tm)
    return pl.pallas_call(
        _qkv_kernel,
        grid=(tokens // tm,),
        in_specs=[
            pl.BlockSpec((tm, D_MODEL), tok),
            pl.BlockSpec((1, D_MODEL), full),
            pl.BlockSpec((D_MODEL, 2 * D_MODEL), full),
            pl.BlockSpec((D_MODEL, D_MODEL), full),
            pl.BlockSpec((1, LANES), full),
            pl.BlockSpec((1, LANES), full),
            pl.BlockSpec((tm, LANES), pos),
            pl.BlockSpec((tm, LANES), pos),
        ],
        out_specs=[pl.BlockSpec((tm, D_MODEL), tok), pl.BlockSpec((tm, D_MODEL), tok),
                   pl.BlockSpec((None, N_HEADS, None, V_ROWS, tm), vt_map)],
        out_shape=[out, out, jax.ShapeDtypeStruct(vt_shape, BF16)],
        compiler_params=_params("parallel"),
        name="qkv_norm_rope",
    )(h2, gain, w_qk, w_v_t, q_gain, k_gain, cos_t, sin_t)


def _stack_q(q_ref, qs_ref):
    tq = q_ref.shape[0]
    q = q_ref[...]
    lane = lax.broadcasted_iota(jnp.int32, q.shape, 1)
    zero = jnp.zeros_like(q)
    qs_ref[0:tq, :] = jnp.where(lane < HEAD_DIM, q, zero)
    qs_ref[tq:2 * tq, :] = jnp.where(lane >= HEAD_DIM, q, zero)


def _attn_kernel(q_ref, k_ref, vt_ref, lam_ref, sg_ref, o_ref, qs_ref, acc_ref, *m_scratch,
                 lam_init, subtract_max):
    tq = q_ref.shape[0]
    n_chunks, _, tk = vt_ref.shape
    _stack_q(q_ref, qs_ref)
    acc_ref[...] = jnp.zeros(acc_ref.shape, F32)
    if subtract_max:
        m_ref, = m_scratch
        m_ref[...] = jnp.full(m_ref.shape, -jnp.inf, F32)

    def chunk(c, carry):
        off = pl.multiple_of(c * tk, tk)
        st = lax.dot_general(k_ref[pl.ds(off, tk), :], qs_ref[...], (((1,), (1,)), ((), ())),
                             preferred_element_type=F32)
        if subtract_max:
            m_prev = m_ref[...]
            m_new = jnp.maximum(m_prev, jnp.max(st, axis=0, keepdims=True))
            acc_ref[...] = acc_ref[...] * jnp.exp2(m_prev - m_new)
            m_ref[...] = m_new
            st = st - m_new
        acc_ref[...] += _dot(vt_ref[c], jnp.exp2(st).astype(BF16))
        return carry

    lax.fori_loop(0, n_chunks, chunk, 0, unroll=1 if subtract_max else 8)

    ot = acc_ref[0:V_HEAD_DIM, :] / acc_ref[V_HEAD_DIM:V_HEAD_DIM + 1, :]
    lv = lam_ref[...]
    lam = (jnp.exp(jnp.sum(lv[0:1] * lv[1:2], axis=1, keepdims=True))
           - jnp.exp(jnp.sum(lv[2:3] * lv[3:4], axis=1, keepdims=True)) + lam_init)
    od = ot[:, 0:tq] - lam * ot[:, tq:2 * tq]
    ms = jnp.mean(od * od, axis=0, keepdims=True)
    on = od * lax.rsqrt(ms + RMS_EPS) * sg_ref[...] * (1.0 - lam_init)
    o_ref[...] = on.T.astype(BF16)


def _attention_call(q, k, vt, lam_vecs, subln_col, lam_init, subtract_max, tq=512):
    b, s, _ = q.shape
    tq = min(tq, s)
    assert s % tq == 0
    n_chunks, tk = vt.shape[2], vt.shape[4]
    qmap = lambda bi, hi, qi: (bi, qi, hi)
    full = lambda bi, hi, qi: (0, 0)
    scratch = [pltpu.VMEM((2 * tq, LANES), BF16), pltpu.VMEM((V_ROWS, 2 * tq), F32)]
    if subtract_max:
        scratch.append(pltpu.VMEM((1, 2 * tq), F32))
    return pl.pallas_call(
        functools.partial(_attn_kernel, lam_init=lam_init, subtract_max=subtract_max),
        grid=(b, N_HEADS, s // tq),
        in_specs=[
            pl.BlockSpec((None, tq, LANES), qmap),
            pl.BlockSpec((None, s, LANES), lambda bi, hi, qi: (bi, 0, hi)),
            pl.BlockSpec((None, None, n_chunks, V_ROWS, tk), lambda bi, hi, qi: (bi, hi, 0, 0, 0)),
            pl.BlockSpec((4, HEAD_DIM), full),
            pl.BlockSpec((V_HEAD_DIM, 1), full),
        ],
        out_specs=pl.BlockSpec((None, tq, LANES), qmap),
        out_shape=jax.ShapeDtypeStruct((b, s, D_MODEL), BF16),
        scratch_shapes=scratch,
        compiler_params=_params("parallel", "parallel", "parallel"),
        name="diff_attention_online" if subtract_max else "diff_attention_bounded",
    )(q, k, vt, lam_vecs, subln_col)


def _attention(q, k, vt, q_gain, k_gain, lam_vecs, subln_col, lam_init):
    score_bound = (HEAD_DIM * Q_SCALE) * jnp.max(jnp.abs(q_gain)) * jnp.max(jnp.abs(k_gain))
    return lax.cond(score_bound < SAFE_LOG2_SCORE,
                    lambda *a: _attention_call(*a, lam_init, False),
                    lambda *a: _attention_call(*a, lam_init, True),
                    q, k, vt, lam_vecs, subln_col)


MLP_FF_TILE = 1024


def _mlp_body(h, g_ref, wu_ref, wd_ref, out_ref):
    xn = _rms(h, g_ref[...]).astype(BF16)
    acc = h
    for f in range(D_FF // MLP_FF_TILE):
        sl = slice(f * MLP_FF_TILE, (f + 1) * MLP_FF_TILE)
        a = jnp.maximum(_dot(xn, wu_ref[:, sl]), 0.0)
        acc = acc + _dot((a * a).astype(BF16), wd_ref[sl, :])
    out_ref[...] = acc


def _mlp_kernel(h_ref, g_ref, wu_ref, wd_ref, out_ref):
    _mlp_body(h_ref[...], g_ref, wu_ref, wd_ref, out_ref)


def _proj_mlp_kernel(h_ref, o_ref, wo_ref, g_ref, wu_ref, wd_ref, out_ref):
    _mlp_body(h_ref[...] + _dot(o_ref[...], wo_ref[...]), g_ref, wu_ref, wd_ref, out_ref)


def _resident(shape):
    return pl.BlockSpec(shape, lambda i: (0,) * len(shape), pipeline_mode=pl.Buffered(1))


def _mlp(h2, gain, w_up, w_down, attn_out=None, w_o=None, tm=512):
    tokens = h2.shape[0]
    tok = pl.BlockSpec((tm, D_MODEL), lambda i: (i, 0))
    mlp_specs = [_resident((1, D_MODEL)), _resident((D_MODEL, D_FF)), _resident((D_FF, D_MODEL))]
    if attn_out is None:
        body, name = _mlp_kernel, "relu2_mlp"
        in_specs, args = [tok] + mlp_specs, (h2, gain, w_up, w_down)
    else:
        body, name = _proj_mlp_kernel, "attn_proj_relu2_mlp"
        in_specs = [tok, tok, _resident((D_MODEL, D_MODEL))] + mlp_specs
        args = (h2, attn_out, w_o, gain, w_up, w_down)
    return pl.pallas_call(
        body,
        grid=(tokens // tm,),
        in_specs=in_specs,
        out_specs=tok,
        out_shape=jax.ShapeDtypeStruct((tokens, D_MODEL), F32),
        compiler_params=_params("parallel"),
        name=name,
    )(*args)


def _norm_kernel(h_ref, g_ref, u_ref):
    u_ref[...] = _rms(h_ref[...], g_ref[...]).astype(BF16)


def _norm(h2, gain, tm=1024):
    tokens = h2.shape[0]
    tok = lambda i: (i, 0)
    return pl.pallas_call(
        _norm_kernel,
        grid=(tokens // tm,),
        in_specs=[pl.BlockSpec((tm, D_MODEL), tok), pl.BlockSpec((1, D_MODEL), lambda i: (0, 0))],
        out_specs=pl.BlockSpec((tm, D_MODEL), tok),
        out_shape=jax.ShapeDtypeStruct((tokens, D_MODEL), BF16),
        compiler_params=_params("parallel"),
        name="ssm_pre_norm",
    )(h2, gain)


def _ssm_kernel(u_ref, e_ref, pin_ref, bb_ref, pout_ref, cc_ref, al_ref, y_ref,
                t_ref, zs_ref, sfr_ref, sbr_ref, sfi_ref, sbi_ref, *, batch):
    rows, width = u_ref.shape
    n_chunks = rows // batch
    chunk = width // SSM_GROUP
    p = SSM_STATE

    e = e_ref[...]
    for j in range(chunk):
        off = SSM_GROUP * (chunk - 1 - j)
        t_ref[SSM_GROUP * j:SSM_GROUP * (j + 1), :] = e[:, off:off + width].astype(BF16)

    def outer(pow_ref, vec_ref):
        ar_, ai_ = pow_ref[0][:, None, :], pow_ref[1][:, None, :]
        vr_, vi_ = vec_ref[0][None], vec_ref[1][None]
        return ((ar_ * vr_ - ai_ * vi_).reshape(width, 2 * p),
                (ar_ * vi_ + ai_ * vr_).reshape(width, 2 * p))

    wi_re, wi_im = outer(pin_ref, bb_ref)
    w_in = jnp.concatenate([wi_re, wi_im], axis=1).astype(BF16)
    wo_re, wo_im = outer(pout_ref, cc_ref)
    w_out_t = jnp.concatenate([wo_re, -wo_im], axis=1).astype(BF16)

    u = u_ref[...]
    zs_ref[...] = _dot(u, w_in)

    ar = al_ref[0:1, :]
    ai = al_ref[1:2, :]
    is_fwd = lax.broadcasted_iota(jnp.int32, (batch, 2 * p), 1) < p

    def step(kk, carry):
        xr, xi = carry
        rf = pl.multiple_of(kk * batch, batch)
        rb = pl.multiple_of((n_chunks - 1 - kk) * batch, batch)
        lr = jnp.where(is_fwd, zs_ref[pl.ds(rf, batch), 0:2 * p], zs_ref[pl.ds(rb, batch), 0:2 * p])
        li = jnp.where(is_fwd, zs_ref[pl.ds(rf, batch), 2 * p:4 * p],
                       zs_ref[pl.ds(rb, batch), 2 * p:4 * p])
        sfr_ref[pl.ds(rf, batch), :] = xr
        sbr_ref[pl.ds(rb, batch), :] = xr
        sfi_ref[pl.ds(rf, batch), :] = xi
        sbi_ref[pl.ds(rb, batch), :] = xi
        return ar * xr - ai * xi + lr, ar * xi + ai * xr + li

    zero = jnp.zeros((batch, 2 * p), F32)
    lax.fori_loop(0, n_chunks, step, (zero, zero))

    fwd_lane = lax.broadcasted_iota(jnp.int32, (rows, 2 * p), 1) < p
    xin = jnp.concatenate(
        [jnp.where(fwd_lane, sfr_ref[...], sbr_ref[...]),
         jnp.where(fwd_lane, sfi_ref[...], sbi_ref[...])], axis=1).astype(BF16)
    carry_out = lax.dot_general(xin, w_out_t, (((1,), (1,)), ((), ())), preferred_element_type=F32)
    y_ref[...] = (_dot(u, t_ref[...]) + carry_out).astype(BF16)


def _ssm(ug, e_rows, pow_in, b_bar, pow_out, c_mat, a_chunk, batch):
    g, rows, width = ug.shape
    chunk = width // SSM_GROUP
    p = SSM_STATE
    grp = lambda i: (i, 0, 0)
    grp4 = lambda i: (i, 0, 0, 0)
    return pl.pallas_call(
        functools.partial(_ssm_kernel, batch=batch),
        grid=(g,),
        in_specs=[
            pl.BlockSpec((None, rows, width), grp),
            pl.BlockSpec((None, SSM_GROUP, 2 * width), grp),
            pl.BlockSpec((None, 2, chunk, 2 * p), grp4),
            pl.BlockSpec((None, 2, SSM_GROUP, 2 * p), grp4),
            pl.BlockSpec((None, 2, chunk, 2 * p), grp4),
            pl.BlockSpec((None, 2, SSM_GROUP, 2 * p), grp4),
            pl.BlockSpec((None, 2, 2 * p), grp),
        ],
        out_specs=pl.BlockSpec((None, rows, width), grp),
        out_shape=jax.ShapeDtypeStruct((g, rows, width), BF16),
        scratch_shapes=[
            pltpu.VMEM((width, width), BF16),
            pltpu.VMEM((rows, 4 * p), F32),
            pltpu.VMEM((rows, 2 * p), F32),
            pltpu.VMEM((rows, 2 * p), F32),
            pltpu.VMEM((rows, 2 * p), F32),
            pltpu.VMEM((rows, 2 * p), F32),
        ],
        compiler_params=_params("parallel"),
        name="s5_chunked_scan",
    )(ug, e_rows, pow_in, b_bar, pow_out, c_mat, a_chunk)


def _glu_kernel(h_ref, y_ref, g_ref, d_ref, w_ref, out_ref):
    h = h_ref[...]
    u = _rms(h, g_ref[...])
    gact = jax.nn.gelu(y_ref[...].astype(F32) + d_ref[...] * u).astype(BF16)
    z = _dot(gact, w_ref[...])
    out_ref[...] = h + z[:, 0:D_MODEL] * jax.nn.sigmoid(z[:, D_MODEL:2 * D_MODEL])


def _glu(h2, y2, gain, d_skip, w_glu, tm=512):
    tokens = h2.shape[0]
    tok = lambda i: (i, 0)
    vec = lambda i: (0, 0)
    return pl.pallas_call(
        _glu_kernel,
        grid=(tokens // tm,),
        in_specs=[
            pl.BlockSpec((tm, D_MODEL), tok),
            pl.BlockSpec((tm, D_MODEL), tok),
            pl.BlockSpec((1, D_MODEL), vec),
            pl.BlockSpec((1, D_MODEL), vec),
            pl.BlockSpec((D_MODEL, 2 * D_MODEL), vec),
        ],
        out_specs=pl.BlockSpec((tm, D_MODEL), tok),
        out_shape=jax.ShapeDtypeStruct((tokens, D_MODEL), F32),
        compiler_params=_params("parallel"),
        name="ssm_gelu_glu",
    )(h2, y2, gain, d_skip, w_glu)


def _ssm_operands(a_re, a_im, log_dt, b_re, b_im, c_re, c_im, chunk):
    hp = lax.Precision.HIGHEST
    dt = jnp.exp(log_dt)[..., None]
    zr, zi = a_re * dt, a_im * dt
    t = jnp.arange(chunk + 1, dtype=F32)[None, None, :, None]
    mag = jnp.exp(zr[:, :, None, :] * t)
    ang = zi[:, :, None, :] * t
    pr, pi = mag * jnp.cos(ang), mag * jnp.sin(ang)
    nr, ni = pr[:, :, 1, :] - 1.0, pi[:, :, 1, :]
    den = a_re * a_re + a_im * a_im
    qr = (nr * a_re + ni * a_im) / den
    qi = (ni * a_re - nr * a_im) / den
    bbr = qr[..., None] * b_re - qi[..., None] * b_im
    bbi = qr[..., None] * b_im + qi[..., None] * b_re

    car = c_re[:, :, None] * pr[:, :, :chunk, None, :] - c_im[:, :, None] * pi[:, :, :chunk, None, :]
    cai = c_re[:, :, None] * pi[:, :, :chunk, None, :] + c_im[:, :, None] * pr[:, :, :chunk, None, :]
    kern = (jnp.einsum('dgtop,dgpi->dgtoi', car, bbr, precision=hp)
            - jnp.einsum('dgtop,dgpi->dgtoi', cai, bbi, precision=hp))
    g = a_re.shape[1]
    krow = kern.transpose(0, 1, 4, 2, 3)
    center = (krow[0, :, :, 0] + krow[1, :, :, 0])[:, :, None]
    e_rows = jnp.concatenate(
        [krow[1, :, :, :0:-1], center, krow[0, :, :, 1:],
         jnp.zeros((g, SSM_GROUP, 1, SSM_GROUP), F32)], axis=2)
    e_rows = e_rows.reshape(g, SSM_GROUP, 2 * chunk * SSM_GROUP)

    def fwd_bwd(x):
        return jnp.concatenate([x[0], x[1]], axis=-1)

    def re_im(xr, xi):
        return jnp.stack([fwd_bwd(xr), fwd_bwd(xi)], axis=1)

    pow_in = re_im(jnp.stack([pr[0, :, chunk - 1::-1], pr[1, :, :chunk]]),
                   jnp.stack([pi[0, :, chunk - 1::-1], pi[1, :, :chunk]]))
    b_bar = re_im(bbr.transpose(0, 1, 3, 2), bbi.transpose(0, 1, 3, 2))
    pow_out = re_im(jnp.stack([pr[0, :, 1:], pr[1, :, :0:-1]]),
                    jnp.stack([pi[0, :, 1:], pi[1, :, :0:-1]]))
    c_mat = re_im(c_re, c_im)
    a_chunk = jnp.stack([jnp.concatenate([pr[0, :, chunk], pr[1, :, chunk]], axis=-1),
                         jnp.concatenate([pi[0, :, chunk], pi[1, :, chunk]], axis=-1)], axis=1)
    return e_rows, pow_in, b_bar, pow_out, c_mat, a_chunk


def _ssm_layer(h2, batch, seq, gain, a_re, a_im, log_dt, b_re, b_im, c_re, c_im, d_skip, w_glu):
    chunk = SSM_CHUNK
    n_chunks = seq // chunk
    operands = _ssm_operands(a_re, a_im, log_dt, b_re, b_im, c_re, c_im, chunk)
    u = _norm(h2, gain)
    ug = u.reshape(batch, n_chunks, chunk, N_GROUPS, SSM_GROUP).transpose(3, 1, 0, 2, 4)
    ug = ug.reshape(N_GROUPS, n_chunks * batch, chunk * SSM_GROUP)
    yg = _ssm(ug, *operands, batch)
    y = yg.reshape(N_GROUPS, n_chunks, batch, chunk, SSM_GROUP).transpose(2, 1, 3, 0, 4)
    y2 = y.reshape(batch * seq, D_MODEL)
    return _glu(h2, y2, gain, d_skip, w_glu)


def _rope_tables(seq):
    pos = jnp.arange(seq, dtype=F32)
    inv_freq = ROPE_THETA ** (-jnp.arange(0, HEAD_DIM, 2, dtype=F32) / HEAD_DIM)
    ang = pos[:, None] * inv_freq[None, :]
    cos, sin = jnp.cos(ang), jnp.sin(ang)
    reps = LANES // HEAD_DIM
    cos_t = jnp.tile(jnp.concatenate([cos, cos], axis=-1), (1, reps))
    sin_t = jnp.tile(jnp.concatenate([-sin, sin], axis=-1), (1, reps))
    return cos_t, sin_t


def _lambda_init(layer_idx):
    return 0.8 - 0.6 * math.exp(-0.3 * layer_idx)


def kernel(x, norm_mix, norm_ffn, attn_w_qkv, attn_q_gain, attn_k_gain, attn_lambda,
           attn_subln, attn_w_o, ssm_a_re, ssm_a_im, ssm_log_dt, ssm_b_re, ssm_b_im,
           ssm_c_re, ssm_c_im, ssm_d, ssm_w_glu, ffn_w_up, ffn_w_down):
    batch, seq, d = x.shape
    assert d == D_MODEL
    depth = norm_mix.shape[0]
    cos_t, sin_t = _rope_tables(seq)
    reps = LANES // HEAD_DIM
    h2 = x.reshape(batch * seq, D_MODEL)
    for i in range(depth):
        j = i // N_MIXERS
        gain = norm_mix[i][None, :]
        if i % N_MIXERS == 0:
            qg, kg = attn_q_gain[j], attn_k_gain[j]
            w_qk = attn_w_qkv[j][:, :2 * D_MODEL].astype(BF16)
            w_v_t = attn_w_qkv[j][:, 2 * D_MODEL:].T.astype(BF16)
            q, k, vt = _qkv(h2, gain, w_qk, w_v_t, jnp.tile(qg, reps)[None, :],
                            jnp.tile(kg, reps)[None, :], cos_t, sin_t, seq)
            o = _attention(q.reshape(batch, seq, D_MODEL), k.reshape(batch, seq, D_MODEL), vt, qg, kg,
                           attn_lambda[j], attn_subln[j][:, None], _lambda_init(i))
            proj = dict(attn_out=o.reshape(batch * seq, D_MODEL), w_o=attn_w_o[j].astype(BF16))
        else:
            h2 = _ssm_layer(h2, batch, seq, gain, ssm_a_re[j], ssm_a_im[j], ssm_log_dt[j],
                            ssm_b_re[j], ssm_b_im[j], ssm_c_re[j], ssm_c_im[j],
                            ssm_d[j][None, :], ssm_w_glu[j].astype(BF16))
            proj = {}
        h2 = _mlp(h2, norm_ffn[i][None, :], ffn_w_up[i].astype(BF16), ffn_w_down[i].astype(BF16),
                  **proj)
    return h2.reshape(batch, seq, D_MODEL)
```

```python
import functools
import math

import jax
import jax.numpy as jnp
from jax import lax
from jax.experimental import pallas as pl
from jax.experimental.pallas import tpu as pltpu

F32 = jnp.float32
BF16 = jnp.bfloat16

D_MODEL = 1024
N_HEADS = 8
HEAD_DIM = 64
V_HEAD_DIM = 2 * HEAD_DIM
V_ROWS = V_HEAD_DIM + 16
ROPE_THETA = 10000.0
SSM_GROUP = 16
N_GROUPS = D_MODEL // SSM_GROUP
SSM_STATE = 64
D_FF = 4 * D_MODEL
RMS_EPS = 1e-6
Q_SCALE = HEAD_DIM ** -0.5 * math.log2(math.e)
SAFE_LOG2_SCORE = 80.0
N_MIXERS = 2

LANES = 128
SSM_CHUNK = 64
VMEM_LIMIT = 48 * 1024 * 1024


def _params(*sem):
    return pltpu.CompilerParams(dimension_semantics=sem, vmem_limit_bytes=VMEM_LIMIT)


def _rms(x, gain):
    ms = jnp.mean(x * x, axis=-1, keepdims=True)
    return x * lax.rsqrt(ms + RMS_EPS) * gain


def _dot(a, b):
    return jnp.dot(a, b, preferred_element_type=F32)


def _qkv_kernel(x_ref, g_ref, wqk_ref, wvt_ref, qg_ref, kg_ref, cos_ref, sin_ref,
                q_ref, k_ref, vt_ref):
    tm = x_ref.shape[0]
    xn = _rms(x_ref[...], g_ref[...]).astype(BF16)
    row = lax.broadcasted_iota(jnp.int32, (LANES, LANES), 0)
    col = lax.broadcasted_iota(jnp.int32, (LANES, LANES), 1)
    seg_mean = jnp.where(row // HEAD_DIM == col // HEAD_DIM, 1.0 / HEAD_DIM, 0.0).astype(BF16)
    lane = lax.broadcasted_iota(jnp.int32, (tm, LANES), 1)
    first_half = (lane % HEAD_DIM) < (HEAD_DIM // 2)
    cos = cos_ref[...]
    sin = sin_ref[...]

    def norm_rope(y, gain, scale):
        sq = y * y
        hi = sq.astype(BF16)
        lo = (sq - hi.astype(F32)).astype(BF16)
        ms = _dot(hi, seg_mean) + _dot(lo, seg_mean)
        yn = y * lax.rsqrt(ms + RMS_EPS) * gain
        partner = jnp.where(first_half,
                            pltpu.roll(yn, LANES - HEAD_DIM // 2, 1),
                            pltpu.roll(yn, HEAD_DIM // 2, 1))
        return (yn * cos + partner * sin) * scale

    yq = _dot(xn, wqk_ref[:, 0:D_MODEL])
    for t in range(D_MODEL // LANES):
        sl = slice(t * LANES, (t + 1) * LANES)
        q_ref[:, sl] = norm_rope(yq[:, sl], qg_ref[...], Q_SCALE).astype(BF16)
    yk = _dot(xn, wqk_ref[:, D_MODEL:2 * D_MODEL])
    for t in range(D_MODEL // LANES):
        sl = slice(t * LANES, (t + 1) * LANES)
        k_ref[:, sl] = norm_rope(yk[:, sl], kg_ref[...], 1.0).astype(BF16)
    yvt = lax.dot_general(wvt_ref[...], xn, (((1,), (1,)), ((), ())),
                          preferred_element_type=F32)
    pad_rows = lax.broadcasted_iota(jnp.int32, (V_ROWS - V_HEAD_DIM, tm), 0)
    ones_row = jnp.where(pad_rows == 0, 1.0, 0.0).astype(BF16)
    for t in range(N_HEADS):
        vt_ref[t, 0:V_HEAD_DIM, :] = yvt[t * V_HEAD_DIM:(t + 1) * V_HEAD_DIM, :].astype(BF16)
        vt_ref[t, V_HEAD_DIM:V_ROWS, :] = ones_row


def _qkv(h2, gain, w_qk, w_v_t, q_gain, k_gain, cos_t, sin_t, seq, tm=512):
    tokens = h2.shape[0]
    tm = min(tm, seq)
    assert tokens % tm == 0 and seq % tm == 0
    n_pos_blocks = seq // tm
    tok = lambda i: (i, 0)
    full = lambda i: (0, 0)
    pos = lambda i: (i % n_pos_blocks, 0)
    vt_map = lambda i: (i // n_pos_blocks, 0, i % n_pos_blocks, 0, 0)
    out = jax.ShapeDtypeStruct((tokens, D_MODEL), BF16)
    vt_shape = (tokens // seq, N_HEADS, n_pos_blocks, V_ROWS, tm)
    return pl.pallas_call(
        _qkv_kernel,
        grid=(tokens // tm,),
        in_specs=[
            pl.BlockSpec((tm, D_MODEL), tok),
            pl.BlockSpec((1, D_MODEL), full),
            pl.BlockSpec((D_MODEL, 2 * D_MODEL), full),
            pl.BlockSpec((D_MODEL, D_MODEL), full),
            pl.BlockSpec((1, LANES), full),
            pl.BlockSpec((1, LANES), full),
            pl.BlockSpec((tm, LANES), pos),
            pl.BlockSpec((tm, LANES), pos),
        ],
        out_specs=[pl.BlockSpec((tm, D_MODEL), tok), pl.BlockSpec((tm, D_MODEL), tok),
                   pl.BlockSpec((None, N_HEADS, None, V_ROWS, tm), vt_map)],
        out_shape=[out, out, jax.ShapeDtypeStruct(vt_shape, BF16)],
        compiler_params=_params("parallel"),
        name="qkv_norm_rope",
    )(h2, gain, w_qk, w_v_t, q_gain, k_gain, cos_t, sin_t)


def _stack_q(q_ref, qs_ref):
    tq = q_ref.shape[0]
    q = q_ref[...]
    lane = lax.broadcasted_iota(jnp.int32, q.shape, 1)
    zero = jnp.zeros_like(q)
    qs_ref[0:tq, :] = jnp.where(lane < HEAD_DIM, q, zero)
    qs_ref[tq:2 * tq, :] = jnp.where(lane >= HEAD_DIM, q, zero)


def _attn_kernel(q_ref, k_ref, vt_ref, lam_ref, sg_ref, o_ref, qs_ref, acc_ref, *m_scratch,
                 lam_init, subtract_max):
    tq = q_ref.shape[0]
    n_chunks, _, tk = vt_ref.shape
    _stack_q(q_ref, qs_ref)
    acc_ref[...] = jnp.zeros(acc_ref.shape, F32)
    if subtract_max:
        m_ref, = m_scratch
        m_ref[...] = jnp.full(m_ref.shape, -jnp.inf, F32)

    def chunk(c, carry):
        off = pl.multiple_of(c * tk, tk)
        st = lax.dot_general(k_ref[pl.ds(off, tk), :], qs_ref[...], (((1,), (1,)), ((), ())),
                             preferred_element_type=F32)
        if subtract_max:
            m_prev = m_ref[...]
            m_new = jnp.maximum(m_prev, jnp.max(st, axis=0, keepdims=True))
            acc_ref[...] = acc_ref[...] * jnp.exp2(m_prev - m_new)
            m_ref[...] = m_new
            st = st - m_new
        acc_ref[...] += _dot(vt_ref[c], jnp.exp2(st).astype(BF16))
        return carry

    lax.fori_loop(0, n_chunks, chunk, 0, unroll=1 if subtract_max else 8)

    ot = acc_ref[0:V_HEAD_DIM, :] / acc_ref[V_HEAD_DIM:V_HEAD_DIM + 1, :]
    lv = lam_ref[...]
    lam = (jnp.exp(jnp.sum(lv[0:1] * lv[1:2], axis=1, keepdims=True))
           - jnp.exp(jnp.sum(lv[2:3] * lv[3:4], axis=1, keepdims=True)) + lam_init)
    od = ot[:, 0:tq] - lam * ot[:, tq:2 * tq]
    ms = jnp.mean(od * od, axis=0, keepdims=True)
    on = od * lax.rsqrt(ms + RMS_EPS) * sg_ref[...] * (1.0 - lam_init)
    o_ref[...] = on.T.astype(BF16)


def _attention_call(q, k, vt, lam_vecs, subln_col, lam_init, subtract_max, tq=512):
    b, s, _ = q.shape
    tq = min(tq, s)
    assert s % tq == 0
    n_chunks, tk = vt.shape[2], vt.shape[4]
    qmap = lambda bi, hi, qi: (bi, qi, hi)
    full = lambda bi, hi, qi: (0, 0)
    scratch = [pltpu.VMEM((2 * tq, LANES), BF16), pltpu.VMEM((V_ROWS, 2 * tq), F32)]
    if subtract_max:
        scratch.append(pltpu.VMEM((1, 2 * tq), F32))
    return pl.pallas_call(
        functools.partial(_attn_kernel, lam_init=lam_init, subtract_max=subtract_max),
        grid=(b, N_HEADS, s // tq),
        in_specs=[
            pl.BlockSpec((None, tq, LANES), qmap),
            pl.BlockSpec((None, s, LANES), lambda bi, hi, qi: (bi, 0, hi)),
            pl.BlockSpec((None, None, n_chunks, V_ROWS, tk), lambda bi, hi, qi: (bi, hi, 0, 0, 0)),
            pl.BlockSpec((4, HEAD_DIM), full),
            pl.BlockSpec((V_HEAD_DIM, 1), full),
        ],
        out_specs=pl.BlockSpec((None, tq, LANES), qmap),
        out_shape=jax.ShapeDtypeStruct((b, s, D_MODEL), BF16),
        scratch_shapes=scratch,
        compiler_params=_params("parallel", "parallel", "parallel"),
        name="diff_attention_online" if subtract_max else "diff_attention_bounded",
    )(q, k, vt, lam_vecs, subln_col)


def _attention(q, k, vt, q_gain, k_gain, lam_vecs, subln_col, lam_init):
    score_bound = (HEAD_DIM * Q_SCALE) * jnp.max(jnp.abs(q_gain)) * jnp.max(jnp.abs(k_gain))
    return lax.cond(score_bound < SAFE_LOG2_SCORE,
                    lambda *a: _attention_call(*a, lam_init, False),
                    lambda *a: _attention_call(*a, lam_init, True),
                    q, k, vt, lam_vecs, subln_col)


MLP_FF_TILE = 1024


def _mlp_body(h, g_ref, wu_ref, wd_ref, out_ref):
    xn = _rms(h, g_ref[...]).astype(BF16)
    acc = h
    for f in range(D_FF // MLP_FF_TILE):
        sl = slice(f * MLP_FF_TILE, (f + 1) * MLP_FF_TILE)
        a = jnp.maximum(_dot(xn, wu_ref[:, sl]), 0.0)
        acc = acc + _dot((a * a).astype(BF16), wd_ref[sl, :])
    out_ref[...] = acc


def _mlp_kernel(h_ref, g_ref, wu_ref, wd_ref, out_ref):
    _mlp_body(h_ref[...], g_ref, wu_ref, wd_ref, out_ref)


def _proj_mlp_kernel(h_ref, o_ref, wo_ref, g_ref, wu_ref, wd_ref, out_ref):
    _mlp_body(h_ref[...] + _dot(o_ref[...], wo_ref[...]), g_ref, wu_ref, wd_ref, out_ref)


def _resident(shape):
    return pl.BlockSpec(shape, lambda i: (0,) * len(shape), pipeline_mode=pl.Buffered(1))


def _mlp(h2, gain, w_up, w_down, attn_out=None, w_o=None, tm=512):
    tokens = h2.shape[0]
    tok = pl.BlockSpec((tm, D_MODEL), lambda i: (i, 0))
    mlp_specs = [_resident((1, D_MODEL)), _resident((D_MODEL, D_FF)), _resident((D_FF, D_MODEL))]
    if attn_out is None:
        body, name = _mlp_kernel, "relu2_mlp"
        in_specs, args = [tok] + mlp_specs, (h2, gain, w_up, w_down)
    else:
        body, name = _proj_mlp_kernel, "attn_proj_relu2_mlp"
        in_specs = [tok, tok, _resident((D_MODEL, D_MODEL))] + mlp_specs
        args = (h2, attn_out, w_o, gain, w_up, w_down)
    return pl.pallas_call(
        body,
        grid=(tokens // tm,),
        in_specs=in_specs,
        out_specs=tok,
        out_shape=jax.ShapeDtypeStruct((tokens, D_MODEL), F32),
        compiler_params=_params("parallel"),
        name=name,
    )(*args)


def _norm_t_kernel(h_ref, g_ref, ut_ref):
    ut_ref[...] = _rms(h_ref[...], g_ref[...]).T.astype(BF16)


def _norm_t(h2, gain, tm=512):
    tokens = h2.shape[0]
    return pl.pallas_call(
        _norm_t_kernel,
        grid=(tokens // tm,),
        in_specs=[pl.BlockSpec((tm, D_MODEL), lambda i: (i, 0)),
                  pl.BlockSpec((1, D_MODEL), lambda i: (0, 0))],
        out_specs=pl.BlockSpec((D_MODEL, tm), lambda i: (0, i)),
        out_shape=jax.ShapeDtypeStruct((D_MODEL, tokens), BF16),
        compiler_params=_params("parallel"),
        name="ssm_pre_norm",
    )(h2, gain)


def _ssm_kernel(u_ref, e_ref, pin_ref, bb_ref, pout_ref, cc_ref, al_ref, y_ref,
                t_ref, zs_ref, sfr_ref, sbr_ref, sfi_ref, sbi_ref, *, batch):
    rows, width = u_ref.shape
    n_chunks = rows // batch
    chunk = width // SSM_GROUP
    p = SSM_STATE
    assert 2 * chunk == LANES

    low_half = lax.broadcasted_iota(jnp.int32, (chunk, LANES), 1) < chunk
    for h in range(SSM_GROUP):
        for hp in range(SSM_GROUP // 2):
            k_even = jnp.broadcast_to(e_ref[h, 2 * hp:2 * hp + 1, :], (chunk, LANES))
            k_odd = jnp.broadcast_to(e_ref[h, 2 * hp + 1:2 * hp + 2, :], (chunk, LANES))
            t_even = pltpu.roll(k_even, LANES - (chunk - 1), 1, stride=1, stride_axis=0)
            t_odd = pltpu.roll(k_odd, 1, 1, stride=1, stride_axis=0)
            t_ref[chunk * h:chunk * (h + 1), LANES * hp:LANES * (hp + 1)] = (
                jnp.where(low_half, t_even, t_odd).astype(BF16))

    def outer(pow_ref, vec_ref):
        ar_, ai_ = pow_ref[0][None], pow_ref[1][None]
        vr_, vi_ = vec_ref[0][:, None, :], vec_ref[1][:, None, :]
        return ((ar_ * vr_ - ai_ * vi_).reshape(width, 2 * p),
                (ar_ * vi_ + ai_ * vr_).reshape(width, 2 * p))

    wi_re, wi_im = outer(pin_ref, bb_ref)
    w_in = jnp.concatenate([wi_re, wi_im], axis=1).astype(BF16)
    wo_re, wo_im = outer(pout_ref, cc_ref)
    w_out_t = jnp.concatenate([wo_re, -wo_im], axis=1).astype(BF16)

    u = u_ref[...]
    zs_ref[...] = _dot(u, w_in)

    ar = al_ref[0:1, :]
    ai = al_ref[1:2, :]
    is_fwd = lax.broadcasted_iota(jnp.int32, (batch, 2 * p), 1) < p

    def step(kk, carry):
        xr, xi = carry
        rf = pl.multiple_of(kk * batch, batch)
        rb = pl.multiple_of((n_chunks - 1 - kk) * batch, batch)
        lr = jnp.where(is_fwd, zs_ref[pl.ds(rf, batch), 0:2 * p], zs_ref[pl.ds(rb, batch), 0:2 * p])
        li = jnp.where(is_fwd, zs_ref[pl.ds(rf, batch), 2 * p:4 * p],
                       zs_ref[pl.ds(rb, batch), 2 * p:4 * p])
        sfr_ref[pl.ds(rf, batch), :] = xr
        sbr_ref[pl.ds(rb, batch), :] = xr
        sfi_ref[pl.ds(rf, batch), :] = xi
        sbi_ref[pl.ds(rb, batch), :] = xi
        return ar * xr - ai * xi + lr, ar * xi + ai * xr + li

    zero = jnp.zeros((batch, 2 * p), F32)
    lax.fori_loop(0, n_chunks, step, (zero, zero))

    fwd_lane = lax.broadcasted_iota(jnp.int32, (rows, 2 * p), 1) < p
    xin = jnp.concatenate(
        [jnp.where(fwd_lane, sfr_ref[...], sbr_ref[...]),
         jnp.where(fwd_lane, sfi_ref[...], sbi_ref[...])], axis=1).astype(BF16)
    carry_out = lax.dot_general(xin, w_out_t, (((1,), (1,)), ((), ())), preferred_element_type=F32)
    y_ref[...] = (_dot(u, t_ref[...]) + carry_out).astype(BF16)


def _ssm(ug, e_rows, pow_in, b_bar, pow_out, c_mat, a_chunk, batch):
    g, rows, width = ug.shape
    chunk = width // SSM_GROUP
    p = SSM_STATE
    grp = lambda i: (i, 0, 0)
    grp4 = lambda i: (i, 0, 0, 0)
    return pl.pallas_call(
        functools.partial(_ssm_kernel, batch=batch),
        grid=(g,),
        in_specs=[
            pl.BlockSpec((None, rows, width), grp),
            pl.BlockSpec((None, SSM_GROUP, SSM_GROUP, LANES), grp4),
            pl.BlockSpec((None, 2, chunk, 2 * p), grp4),
            pl.BlockSpec((None, 2, SSM_GROUP, 2 * p), grp4),
            pl.BlockSpec((None, 2, chunk, 2 * p), grp4),
            pl.BlockSpec((None, 2, SSM_GROUP, 2 * p), grp4),
            pl.BlockSpec((None, 2, 2 * p), grp),
        ],
        out_specs=pl.BlockSpec((None, rows, width), grp),
        out_shape=jax.ShapeDtypeStruct((g, rows, width), BF16),
        scratch_shapes=[
            pltpu.VMEM((width, width), BF16),
            pltpu.VMEM((rows, 4 * p), F32),
            pltpu.VMEM((rows, 2 * p), F32),
            pltpu.VMEM((rows, 2 * p), F32),
            pltpu.VMEM((rows, 2 * p), F32),
            pltpu.VMEM((rows, 2 * p), F32),
        ],
        compiler_params=_params("parallel"),
        name="s5_chunked_scan",
    )(ug, e_rows, pow_in, b_bar, pow_out, c_mat, a_chunk)


def _glu_kernel(h_ref, yt_ref, g_ref, d_ref, w_ref, out_ref):
    h = h_ref[...]
    u = _rms(h, g_ref[...])
    y = yt_ref[...].astype(F32).T
    gact = jax.nn.gelu(y + d_ref[...] * u).astype(BF16)
    z = _dot(gact, w_ref[...])
    out_ref[...] = h + z[:, 0:D_MODEL] * jax.nn.sigmoid(z[:, D_MODEL:2 * D_MODEL])


def _glu(h2, yt, gain, d_skip, w_glu, tm=512):
    tokens = h2.shape[0]
    tok = lambda i: (i, 0)
    vec = lambda i: (0, 0)
    return pl.pallas_call(
        _glu_kernel,
        grid=(tokens // tm,),
        in_specs=[
            pl.BlockSpec((tm, D_MODEL), tok),
            pl.BlockSpec((D_MODEL, tm), lambda i: (0, i)),
            pl.BlockSpec((1, D_MODEL), vec),
            pl.BlockSpec((1, D_MODEL), vec),
            pl.BlockSpec((D_MODEL, 2 * D_MODEL), vec),
        ],
        out_specs=pl.BlockSpec((tm, D_MODEL), tok),
        out_shape=jax.ShapeDtypeStruct((tokens, D_MODEL), F32),
        compiler_params=_params("parallel"),
        name="ssm_gelu_glu",
    )(h2, yt, gain, d_skip, w_glu)


def _ssm_operands(a_re, a_im, log_dt, b_re, b_im, c_re, c_im, chunk):
    hp = lax.Precision.HIGHEST
    dt = jnp.exp(log_dt)[..., None]
    zr, zi = a_re * dt, a_im * dt
    t = jnp.arange(chunk + 1, dtype=F32)[None, None, :, None]
    mag = jnp.exp(zr[:, :, None, :] * t)
    ang = zi[:, :, None, :] * t
    pr, pi = mag * jnp.cos(ang), mag * jnp.sin(ang)
    nr, ni = pr[:, :, 1, :] - 1.0, pi[:, :, 1, :]
    den = a_re * a_re + a_im * a_im
    qr = (nr * a_re + ni * a_im) / den
    qi = (ni * a_re - nr * a_im) / den
    bbr = qr[..., None] * b_re - qi[..., None] * b_im
    bbi = qr[..., None] * b_im + qi[..., None] * b_re

    car = c_re[:, :, None] * pr[:, :, :chunk, None, :] - c_im[:, :, None] * pi[:, :, :chunk, None, :]
    cai = c_re[:, :, None] * pi[:, :, :chunk, None, :] + c_im[:, :, None] * pr[:, :, :chunk, None, :]
    kern = (jnp.einsum('dgtop,dgpi->dgtoi', car, bbr, precision=hp)
            - jnp.einsum('dgtop,dgpi->dgtoi', cai, bbi, precision=hp))
    center = (kern[0, :, 0] + kern[1, :, 0])[:, None]
    lagged = jnp.concatenate([kern[1, :, :0:-1], center, kern[0, :, 1:],
                              jnp.zeros_like(center)], axis=1)
    e_rows = lagged.transpose(0, 3, 2, 1)

    def fwd_bwd(x):
        return jnp.concatenate([x[0], x[1]], axis=-1)

    def re_im(xr, xi):
        return jnp.stack([fwd_bwd(xr), fwd_bwd(xi)], axis=1)

    pow_in = re_im(jnp.stack([pr[0, :, chunk - 1::-1], pr[1, :, :chunk]]),
                   jnp.stack([pi[0, :, chunk - 1::-1], pi[1, :, :chunk]]))
    b_bar = re_im(bbr.transpose(0, 1, 3, 2), bbi.transpose(0, 1, 3, 2))
    pow_out = re_im(jnp.stack([pr[0, :, 1:], pr[1, :, :0:-1]]),
                    jnp.stack([pi[0, :, 1:], pi[1, :, :0:-1]]))
    c_mat = re_im(c_re, c_im)
    a_chunk = jnp.stack([jnp.concatenate([pr[0, :, chunk], pr[1, :, chunk]], axis=-1),
                         jnp.concatenate([pi[0, :, chunk], pi[1, :, chunk]], axis=-1)], axis=1)
    return e_rows, pow_in, b_bar, pow_out, c_mat, a_chunk


def _ssm_layer(h2, batch, seq, gain, a_re, a_im, log_dt, b_re, b_im, c_re, c_im, d_skip, w_glu):
    chunk = SSM_CHUNK
    n_chunks = seq // chunk
    operands = _ssm_operands(a_re, a_im, log_dt, b_re, b_im, c_re, c_im, chunk)
    ut = _norm_t(h2, gain)
    ug = ut.reshape(N_GROUPS, SSM_GROUP, batch, n_chunks, chunk).transpose(0, 3, 2, 1, 4)
    ug = ug.reshape(N_GROUPS, n_chunks * batch, SSM_GROUP * chunk)
    yg = _ssm(ug, *operands, batch)
    yt = yg.reshape(N_GROUPS, n_chunks, batch, SSM_GROUP, chunk).transpose(0, 3, 2, 1, 4)
    return _glu(h2, yt.reshape(D_MODEL, batch * seq), gain, d_skip, w_glu)


def _rope_tables(seq):
    pos = jnp.arange(seq, dtype=F32)
    inv_freq = ROPE_THETA ** (-jnp.arange(0, HEAD_DIM, 2, dtype=F32) / HEAD_DIM)
    ang = pos[:, None] * inv_freq[None, :]
    cos, sin = jnp.cos(ang), jnp.sin(ang)
    reps = LANES // HEAD_DIM
    cos_t = jnp.tile(jnp.concatenate([cos, cos], axis=-1), (1, reps))
    sin_t = jnp.tile(jnp.concatenate([-sin, sin], axis=-1), (1, reps))
    return cos_t, sin_t


def _lambda_init(layer_idx):
    return 0.8 - 0.6 * math.exp(-0.3 * layer_idx)


def kernel(x, norm_mix, norm_ffn, attn_w_qkv, attn_q_gain, attn_k_gain, attn_lambda,
           attn_subln, attn_w_o, ssm_a_re, ssm_a_im, ssm_log_dt, ssm_b_re, ssm_b_im,
           ssm_c_re, ssm_c_im, ssm_d, ssm_w_glu, ffn_w_up, ffn_w_down):
    batch, seq, d = x.shape
    assert d == D_MODEL
    depth = norm_mix.shape[0]
    cos_t, sin_t = _rope_tables(seq)
    reps = LANES // HEAD_DIM
    h2 = x.reshape(batch * seq, D_MODEL)
    for i in range(depth):
        j = i // N_MIXERS
        gain = norm_mix[i][None, :]
        if i % N_MIXERS == 0:
            qg, kg = attn_q_gain[j], attn_k_gain[j]
            w_qk = attn_w_qkv[j][:, :2 * D_MODEL].astype(BF16)
            w_v_t = attn_w_qkv[j][:, 2 * D_MODEL:].T.astype(BF16)
            q, k, vt = _qkv(h2, gain, w_qk, w_v_t, jnp.tile(qg, reps)[None, :],
                            jnp.tile(kg, reps)[None, :], cos_t, sin_t, seq)
            o = _attention(q.reshape(batch, seq, D_MODEL), k.reshape(batch, seq, D_MODEL), vt, qg, kg,
                           attn_lambda[j], attn_subln[j][:, None], _lambda_init(i))
            proj = dict(attn_out=o.reshape(batch * seq, D_MODEL), w_o=attn_w_o[j].astype(BF16))
        else:
            h2 = _ssm_layer(h2, batch, seq, gain, ssm_a_re[j], ssm_a_im[j], ssm_log_dt[j],
                            ssm_b_re[j], ssm_b_im[j], ssm_c_re[j], ssm_c_im[j],
                            ssm_d[j][None, :], ssm_w_glu[j].astype(BF16))
            proj = {}
        h2 = _mlp(h2, norm_ffn[i][None, :], ffn_w_up[i].astype(BF16), ffn_w_down[i].astype(BF16),
                  **proj)
    return h2.reshape(batch, seq, D_MODEL)
```

```python
import functools
import math

import jax
import jax.numpy as jnp
from jax import lax
from jax.experimental import pallas as pl
from jax.experimental.pallas import tpu as pltpu

F32 = jnp.float32
BF16 = jnp.bfloat16

D_MODEL = 1024
N_HEADS = 8
HEAD_DIM = 64
V_HEAD_DIM = 2 * HEAD_DIM
SUBLANES = 8
ROPE_THETA = 10000.0
SSM_GROUP = 16
N_GROUPS = D_MODEL // SSM_GROUP
SSM_STATE = 64
D_FF = 4 * D_MODEL
RMS_EPS = 1e-6
Q_SCALE = HEAD_DIM ** -0.5 * math.log2(math.e)
SAFE_LOG2_SCORE = 80.0
N_MIXERS = 2

LANES = 128
SSM_CHUNK = LANES
VMEM_LIMIT = 48 * 1024 * 1024


def _params(*sem):
    return pltpu.CompilerParams(dimension_semantics=sem, vmem_limit_bytes=VMEM_LIMIT)


def _rms(x, gain):
    ms = jnp.mean(x * x, axis=-1, keepdims=True)
    return x * lax.rsqrt(ms + RMS_EPS) * gain


def _dot(a, b):
    return jnp.dot(a, b, preferred_element_type=F32)


def _qkv_kernel(x_ref, g_ref, wqk_ref, wvt_ref, qg_ref, kg_ref, cos_ref, sin_ref,
                q_ref, k_ref, vt_ref):
    tm = x_ref.shape[0]
    xn = _rms(x_ref[...], g_ref[...]).astype(BF16)
    row = lax.broadcasted_iota(jnp.int32, (LANES, LANES), 0)
    col = lax.broadcasted_iota(jnp.int32, (LANES, LANES), 1)
    seg_mean = jnp.where(row // HEAD_DIM == col // HEAD_DIM, 1.0 / HEAD_DIM, 0.0).astype(BF16)
    lane = lax.broadcasted_iota(jnp.int32, (tm, LANES), 1)
    first_half = (lane % HEAD_DIM) < (HEAD_DIM // 2)
    cos = cos_ref[...]
    sin = sin_ref[...]

    def norm_rope(y, gain, scale):
        sq = y * y
        hi = sq.astype(BF16)
        lo = (sq - hi.astype(F32)).astype(BF16)
        ms = _dot(hi, seg_mean) + _dot(lo, seg_mean)
        yn = y * lax.rsqrt(ms + RMS_EPS) * gain
        partner = jnp.where(first_half,
                            pltpu.roll(yn, LANES - HEAD_DIM // 2, 1),
                            pltpu.roll(yn, HEAD_DIM // 2, 1))
        return (yn * cos + partner * sin) * scale

    yq = _dot(xn, wqk_ref[:, 0:D_MODEL])
    for t in range(D_MODEL // LANES):
        sl = slice(t * LANES, (t + 1) * LANES)
        q_ref[:, sl] = norm_rope(yq[:, sl], qg_ref[...], Q_SCALE).astype(BF16)
    yk = _dot(xn, wqk_ref[:, D_MODEL:2 * D_MODEL])
    for t in range(D_MODEL // LANES):
        sl = slice(t * LANES, (t + 1) * LANES)
        k_ref[:, sl] = norm_rope(yk[:, sl], kg_ref[...], 1.0).astype(BF16)
    yvt = lax.dot_general(wvt_ref[...], xn, (((1,), (1,)), ((), ())),
                          preferred_element_type=F32)
    for t in range(N_HEADS):
        vt_ref[t] = yvt[t * V_HEAD_DIM:(t + 1) * V_HEAD_DIM, :].astype(BF16)


def _qkv(h2, gain, w_qk, w_v_t, q_gain, k_gain, cos_t, sin_t, seq, tm=512):
    tokens = h2.shape[0]
    tm = min(tm, seq)
    assert tokens % tm == 0 and seq % tm == 0
    n_pos_blocks = seq // tm
    tok = lambda i: (i, 0)
    full = lambda i: (0, 0)
    pos = lambda i: (i % n_pos_blocks, 0)
    vt_map = lambda i: (i // n_pos_blocks, 0, i % n_pos_blocks, 0, 0)
    out = jax.ShapeDtypeStruct((tokens, D_MODEL), BF16)
    vt_shape = (tokens // seq, N_HEADS, n_pos_blocks, V_HEAD_DIM, tm)
    return pl.pallas_call(
        _qkv_kernel,
        grid=(tokens // tm,),
        in_specs=[
            pl.BlockSpec((tm, D_MODEL), tok),
            pl.BlockSpec((1, D_MODEL), full),
            pl.BlockSpec((D_MODEL, 2 * D_MODEL), full),
            pl.BlockSpec((D_MODEL, D_MODEL), full),
            pl.BlockSpec((1, LANES), full),
            pl.BlockSpec((1, LANES), full),
            pl.BlockSpec((tm, LANES), pos),
            pl.BlockSpec((tm, LANES), pos),
        ],
        out_specs=[pl.BlockSpec((tm, D_MODEL), tok), pl.BlockSpec((tm, D_MODEL), tok),
                   pl.BlockSpec((None, N_HEADS, None, V_HEAD_DIM, tm), vt_map)],
        out_shape=[out, out, jax.ShapeDtypeStruct(vt_shape, BF16)],
        compiler_params=_params("parallel"),
        name="qkv_norm_rope",
    )(h2, gain, w_qk, w_v_t, q_gain, k_gain, cos_t, sin_t)


def _stack_q(q_ref, qs_ref):
    tq = q_ref.shape[0]
    q = q_ref[...]
    lane = lax.broadcasted_iota(jnp.int32, q.shape, 1)
    zero = jnp.zeros_like(q)
    qs_ref[0:tq, :] = jnp.where(lane < HEAD_DIM, q, zero)
    qs_ref[tq:2 * tq, :] = jnp.where(lane >= HEAD_DIM, q, zero)


def _attn_kernel(q_ref, k_ref, vt_ref, lam_ref, sg_ref, o_ref, qs_ref, acc_ref, l_ref, *m_scratch,
                 lam_init, subtract_max):
    tq = q_ref.shape[0]
    n_chunks, _, tk = vt_ref.shape
    _stack_q(q_ref, qs_ref)
    acc_ref[...] = jnp.zeros(acc_ref.shape, F32)
    l_ref[...] = jnp.zeros(l_ref.shape, F32)
    if subtract_max:
        m_ref, = m_scratch
        m_ref[...] = jnp.full(m_ref.shape, -jnp.inf, F32)

    def chunk(c, carry):
        off = pl.multiple_of(c * tk, tk)
        st = lax.dot_general(k_ref[pl.ds(off, tk), :], qs_ref[...], (((1,), (1,)), ((), ())),
                             preferred_element_type=F32)
        if subtract_max:
            m_prev = m_ref[...]
            m_new = jnp.maximum(m_prev, jnp.max(st, axis=0, keepdims=True))
            alpha = jnp.exp2(m_prev - m_new)
            acc_ref[...] = acc_ref[...] * alpha
            l_ref[...] = l_ref[...] * alpha
            m_ref[...] = m_new
            st = st - m_new
        pt = jnp.exp2(st)
        l_ref[...] += jnp.sum(pt.reshape(tk // SUBLANES, SUBLANES, 2 * tq), axis=0)
        acc_ref[...] += _dot(vt_ref[c], pt.astype(BF16))
        return carry

    lax.fori_loop(0, n_chunks, chunk, 0, unroll=1 if subtract_max else 8)

    ot = acc_ref[...] / jnp.sum(l_ref[...], axis=0, keepdims=True)
    lv = lam_ref[...]
    lam = (jnp.exp(jnp.sum(lv[0:1] * lv[1:2], axis=1, keepdims=True))
           - jnp.exp(jnp.sum(lv[2:3] * lv[3:4], axis=1, keepdims=True)) + lam_init)
    od = ot[:, 0:tq] - lam * ot[:, tq:2 * tq]
    ms = jnp.mean(od * od, axis=0, keepdims=True)
    on = od * lax.rsqrt(ms + RMS_EPS) * sg_ref[...] * (1.0 - lam_init)
    o_ref[...] = on.T.astype(BF16)


def _attention_call(q, k, vt, lam_vecs, subln_col, lam_init, subtract_max, tq=512):
    b, s, _ = q.shape
    tq = min(tq, s)
    assert s % tq == 0
    n_chunks, tk = vt.shape[2], vt.shape[4]
    qmap = lambda bi, hi, qi: (bi, qi, hi)
    full = lambda bi, hi, qi: (0, 0)
    scratch = [pltpu.VMEM((2 * tq, LANES), BF16), pltpu.VMEM((V_HEAD_DIM, 2 * tq), F32),
               pltpu.VMEM((SUBLANES, 2 * tq), F32)]
    if subtract_max:
        scratch.append(pltpu.VMEM((1, 2 * tq), F32))
    return pl.pallas_call(
        functools.partial(_attn_kernel, lam_init=lam_init, subtract_max=subtract_max),
        grid=(b, N_HEADS, s // tq),
        in_specs=[
            pl.BlockSpec((None, tq, LANES), qmap),
            pl.BlockSpec((None, s, LANES), lambda bi, hi, qi: (bi, 0, hi)),
            pl.BlockSpec((None, None, n_chunks, V_HEAD_DIM, tk), lambda bi, hi, qi: (bi, hi, 0, 0, 0)),
            pl.BlockSpec((4, HEAD_DIM), full),
            pl.BlockSpec((V_HEAD_DIM, 1), full),
        ],
        out_specs=pl.BlockSpec((None, tq, LANES), qmap),
        out_shape=jax.ShapeDtypeStruct((b, s, D_MODEL), BF16),
        scratch_shapes=scratch,
        compiler_params=_params("parallel", "parallel", "parallel"),
        name="diff_attention_online" if subtract_max else "diff_attention_bounded",
    )(q, k, vt, lam_vecs, subln_col)


def _attention(q, k, vt, q_gain, k_gain, lam_vecs, subln_col, lam_init):
    score_bound = (HEAD_DIM * Q_SCALE) * jnp.max(jnp.abs(q_gain)) * jnp.max(jnp.abs(k_gain))
    return lax.cond(score_bound < SAFE_LOG2_SCORE,
                    lambda *a: _attention_call(*a, lam_init, False),
                    lambda *a: _attention_call(*a, lam_init, True),
                    q, k, vt, lam_vecs, subln_col)


MLP_FF_TILE = 1024


def _mlp_body(h, g_ref, wu_ref, wd_ref, out_ref):
    xn = _rms(h, g_ref[...]).astype(BF16)
    acc = h
    for f in range(D_FF // MLP_FF_TILE):
        sl = slice(f * MLP_FF_TILE, (f + 1) * MLP_FF_TILE)
        a = jnp.maximum(_dot(xn, wu_ref[:, sl]), 0.0)
        acc = acc + _dot((a * a).astype(BF16), wd_ref[sl, :])
    out_ref[...] = acc


def _mlp_kernel(h_ref, g_ref, wu_ref, wd_ref, out_ref):
    _mlp_body(h_ref[...], g_ref, wu_ref, wd_ref, out_ref)


def _proj_mlp_kernel(h_ref, o_ref, wo_ref, g_ref, wu_ref, wd_ref, out_ref):
    _mlp_body(h_ref[...] + _dot(o_ref[...], wo_ref[...]), g_ref, wu_ref, wd_ref, out_ref)


def _resident(shape):
    return pl.BlockSpec(shape, lambda i: (0,) * len(shape), pipeline_mode=pl.Buffered(1))


def _mlp(h2, gain, w_up, w_down, attn_out=None, w_o=None, tm=512):
    tokens = h2.shape[0]
    tok = pl.BlockSpec((tm, D_MODEL), lambda i: (i, 0))
    mlp_specs = [_resident((1, D_MODEL)), _resident((D_MODEL, D_FF)), _resident((D_FF, D_MODEL))]
    if attn_out is None:
        body, name = _mlp_kernel, "relu2_mlp"
        in_specs, args = [tok] + mlp_specs, (h2, gain, w_up, w_down)
    else:
        body, name = _proj_mlp_kernel, "attn_proj_relu2_mlp"
        in_specs = [tok, tok, _resident((D_MODEL, D_MODEL))] + mlp_specs
        args = (h2, attn_out, w_o, gain, w_up, w_down)
    return pl.pallas_call(
        body,
        grid=(tokens // tm,),
        in_specs=in_specs,
        out_specs=tok,
        out_shape=jax.ShapeDtypeStruct((tokens, D_MODEL), F32),
        compiler_params=_params("parallel"),
        name=name,
    )(*args)


def _norm_group_kernel(h_ref, g_ref, u_ref):
    chunk = h_ref.shape[1]
    for b in range(h_ref.shape[0]):
        ut = _rms(h_ref[b], g_ref[...]).T
        u_ref[:, b] = ut.reshape(N_GROUPS, SSM_GROUP, chunk)


def _norm_group(h4, gain):
    batch, n_chunks, chunk, _ = h4.shape
    return pl.pallas_call(
        _norm_group_kernel,
        grid=(n_chunks,),
        in_specs=[pl.BlockSpec((batch, None, chunk, D_MODEL), lambda c: (0, c, 0, 0)),
                  pl.BlockSpec((1, D_MODEL), lambda c: (0, 0))],
        out_specs=pl.BlockSpec((N_GROUPS, None, batch, SSM_GROUP, chunk), lambda c: (0, c, 0, 0, 0)),
        out_shape=jax.ShapeDtypeStruct((N_GROUPS, n_chunks, batch, SSM_GROUP, chunk), F32),
        compiler_params=_params("parallel"),
        name="ssm_pre_norm",
    )(h4, gain)


def _ssm_kernel(u_ref, e_ref, pin_ref, bb_ref, pout_ref, cc_ref, al_ref, y_ref,
                t_ref, zs_ref, sfr_ref, sbr_ref, sfi_ref, sbi_ref):
    n_chunks, batch, _, chunk = u_ref.shape
    rows, width = n_chunks * batch, SSM_GROUP * chunk
    p = SSM_STATE
    assert chunk == LANES

    upper = (lax.broadcasted_iota(jnp.int32, (chunk, chunk), 1)
             >= lax.broadcasted_iota(jnp.int32, (chunk, chunk), 0))
    for h in range(SSM_GROUP):
        for ho in range(SSM_GROUP):
            k_bwd = jnp.broadcast_to(e_ref[h, ho:ho + 1, 0:chunk], (chunk, chunk))
            k_fwd = jnp.broadcast_to(e_ref[h, ho:ho + 1, chunk:2 * chunk], (chunk, chunk))
            t_fwd = pltpu.roll(k_fwd, 0, 1, stride=1, stride_axis=0)
            t_bwd = pltpu.roll(k_bwd, 0, 1, stride=1, stride_axis=0)
            t_ref[chunk * h:chunk * (h + 1), chunk * ho:chunk * (ho + 1)] = (
                jnp.where(upper, t_fwd, t_bwd).astype(BF16))

    def outer(pow_ref, vec_ref):
        ar_, ai_ = pow_ref[0][None], pow_ref[1][None]
        vr_, vi_ = vec_ref[0][:, None, :], vec_ref[1][:, None, :]
        return ((ar_ * vr_ - ai_ * vi_).reshape(width, 2 * p),
                (ar_ * vi_ + ai_ * vr_).reshape(width, 2 * p))

    wi_re, wi_im = outer(pin_ref, bb_ref)
    w_in = jnp.concatenate([wi_re, wi_im], axis=1).astype(BF16)
    wo_re, wo_im = outer(pout_ref, cc_ref)
    w_out_t = jnp.concatenate([wo_re, -wo_im], axis=1).astype(BF16)

    u = jnp.concatenate([u_ref[:, :, h, :].reshape(rows, chunk) for h in range(SSM_GROUP)],
                        axis=1).astype(BF16)
    zs_ref[...] = _dot(u, w_in)

    ar = al_ref[0:1, :]
    ai = al_ref[1:2, :]
    is_fwd = lax.broadcasted_iota(jnp.int32, (batch, 2 * p), 1) < p

    def step(kk, carry):
        xr, xi = carry
        rf = pl.multiple_of(kk * batch, batch)
        rb = pl.multiple_of((n_chunks - 1 - kk) * batch, batch)
        lr = jnp.where(is_fwd, zs_ref[pl.ds(rf, batch), 0:2 * p], zs_ref[pl.ds(rb, batch), 0:2 * p])
        li = jnp.where(is_fwd, zs_ref[pl.ds(rf, batch), 2 * p:4 * p],
                       zs_ref[pl.ds(rb, batch), 2 * p:4 * p])
        sfr_ref[pl.ds(rf, batch), :] = xr
        sbr_ref[pl.ds(rb, batch), :] = xr
        sfi_ref[pl.ds(rf, batch), :] = xi
        sbi_ref[pl.ds(rb, batch), :] = xi
        return ar * xr - ai * xi + lr, ar * xi + ai * xr + li

    zero = jnp.zeros((batch, 2 * p), F32)
    lax.fori_loop(0, n_chunks, step, (zero, zero))

    fwd_lane = lax.broadcasted_iota(jnp.int32, (rows, 2 * p), 1) < p
    xin = jnp.concatenate(
        [jnp.where(fwd_lane, sfr_ref[...], sbr_ref[...]),
         jnp.where(fwd_lane, sfi_ref[...], sbi_ref[...])], axis=1).astype(BF16)
    carry_out = lax.dot_general(xin, w_out_t, (((1,), (1,)), ((), ())), preferred_element_type=F32)
    y = _dot(u, t_ref[...]) + carry_out
    for h in range(SSM_GROUP):
        y_ref[h] = y[:, chunk * h:chunk * (h + 1)]


def _ssm(ug, e_rows, pow_in, b_bar, pow_out, c_mat, a_chunk):
    g, n_chunks, batch, _, chunk = ug.shape
    rows, width = n_chunks * batch, SSM_GROUP * chunk
    p = SSM_STATE
    grp = lambda i: (i, 0, 0)
    grp4 = lambda i: (i, 0, 0, 0)
    return pl.pallas_call(
        _ssm_kernel,
        grid=(g,),
        in_specs=[
            pl.BlockSpec((None, n_chunks, batch, SSM_GROUP, chunk), lambda i: (i, 0, 0, 0, 0)),
            pl.BlockSpec((None, SSM_GROUP, SSM_GROUP, 2 * chunk), grp4),
            pl.BlockSpec((None, 2, chunk, 2 * p), grp4),
            pl.BlockSpec((None, 2, SSM_GROUP, 2 * p), grp4),
            pl.BlockSpec((None, 2, chunk, 2 * p), grp4),
            pl.BlockSpec((None, 2, SSM_GROUP, 2 * p), grp4),
            pl.BlockSpec((None, 2, 2 * p), grp),
        ],
        out_specs=pl.BlockSpec((None, SSM_GROUP, rows, chunk), grp4),
        out_shape=jax.ShapeDtypeStruct((g, SSM_GROUP, rows, chunk), F32),
        scratch_shapes=[
            pltpu.VMEM((width, width), BF16),
            pltpu.VMEM((rows, 4 * p), F32),
            pltpu.VMEM((rows, 2 * p), F32),
            pltpu.VMEM((rows, 2 * p), F32),
            pltpu.VMEM((rows, 2 * p), F32),
            pltpu.VMEM((rows, 2 * p), F32),
        ],
        compiler_params=_params("parallel"),
        name="s5_chunked_scan",
    )(ug, e_rows, pow_in, b_bar, pow_out, c_mat, a_chunk)


def _glu_kernel(h_ref, y_ref, g_ref, d_ref, w_ref, out_ref):
    batch, chunk, _ = h_ref.shape
    acts = []
    for b in range(batch):
        u = _rms(h_ref[b], g_ref[...])
        y = y_ref[:, :, b, :].reshape(D_MODEL, chunk).T
        acts.append(jax.nn.gelu(y + d_ref[...] * u).astype(BF16))
    z = _dot(jnp.concatenate(acts, axis=0), w_ref[...])
    for b in range(batch):
        zb = z[b * chunk:(b + 1) * chunk]
        out_ref[b] = h_ref[b] + zb[:, 0:D_MODEL] * jax.nn.sigmoid(zb[:, D_MODEL:2 * D_MODEL])


def _glu(h4, y4, gain, d_skip, w_glu):
    batch, n_chunks, chunk, _ = h4.shape
    tok = pl.BlockSpec((batch, None, chunk, D_MODEL), lambda c: (0, c, 0, 0))
    return pl.pallas_call(
        _glu_kernel,
        grid=(n_chunks,),
        in_specs=[
            tok,
            pl.BlockSpec((N_GROUPS, SSM_GROUP, batch, chunk), lambda c: (0, 0, c, 0)),
            _resident((1, D_MODEL)),
            _resident((1, D_MODEL)),
            _resident((D_MODEL, 2 * D_MODEL)),
        ],
        out_specs=tok,
        out_shape=jax.ShapeDtypeStruct(h4.shape, F32),
        compiler_params=_params("parallel"),
        name="ssm_gelu_glu",
    )(h4, y4, gain, d_skip, w_glu)


def _ssm_operands(a_re, a_im, log_dt, b_re, b_im, c_re, c_im, chunk):
    hp = lax.Precision.HIGHEST
    dt = jnp.exp(log_dt)[..., None]
    zr, zi = a_re * dt, a_im * dt
    t = jnp.arange(chunk + 1, dtype=F32)[None, None, :, None]
    mag = jnp.exp(zr[:, :, None, :] * t)
    ang = zi[:, :, None, :] * t
    pr, pi = mag * jnp.cos(ang), mag * jnp.sin(ang)
    nr, ni = pr[:, :, 1, :] - 1.0, pi[:, :, 1, :]
    den = a_re * a_re + a_im * a_im
    qr = (nr * a_re + ni * a_im) / den
    qi = (ni * a_re - nr * a_im) / den
    bbr = qr[..., None] * b_re - qi[..., None] * b_im
    bbi = qr[..., None] * b_im + qi[..., None] * b_re

    car = c_re[:, :, None] * pr[:, :, :chunk, None, :] - c_im[:, :, None] * pi[:, :, :chunk, None, :]
    cai = c_re[:, :, None] * pi[:, :, :chunk, None, :] + c_im[:, :, None] * pr[:, :, :chunk, None, :]
    kern = jnp.einsum('dgtop,dgpi->dgtoi', jnp.concatenate([car, -cai], axis=-1),
                      jnp.concatenate([bbr, bbi], axis=2), precision=hp)
    center = (kern[0, :, 0] + kern[1, :, 0])[:, None]
    lagged = jnp.concatenate([jnp.zeros_like(center), kern[1, :, :0:-1], center, kern[0, :, 1:]],
                             axis=1)
    e_rows = lagged.transpose(0, 3, 2, 1)

    def fwd_bwd(x):
        return jnp.concatenate([x[0], x[1]], axis=-1)

    def re_im(xr, xi):
        return jnp.stack([fwd_bwd(xr), fwd_bwd(xi)], axis=1)

    pow_in = re_im(jnp.stack([pr[0, :, chunk - 1::-1], pr[1, :, :chunk]]),
                   jnp.stack([pi[0, :, chunk - 1::-1], pi[1, :, :chunk]]))
    b_bar = re_im(bbr.transpose(0, 1, 3, 2), bbi.transpose(0, 1, 3, 2))
    pow_out = re_im(jnp.stack([pr[0, :, 1:], pr[1, :, :0:-1]]),
                    jnp.stack([pi[0, :, 1:], pi[1, :, :0:-1]]))
    c_mat = re_im(c_re, c_im)
    a_chunk = jnp.stack([jnp.concatenate([pr[0, :, chunk], pr[1, :, chunk]], axis=-1),
                         jnp.concatenate([pi[0, :, chunk], pi[1, :, chunk]], axis=-1)], axis=1)
    return e_rows, pow_in, b_bar, pow_out, c_mat, a_chunk


def _ssm_layer(h2, batch, seq, gain, a_re, a_im, log_dt, b_re, b_im, c_re, c_im, d_skip, w_glu):
    chunk = SSM_CHUNK
    n_chunks = seq // chunk
    operands = _ssm_operands(a_re, a_im, log_dt, b_re, b_im, c_re, c_im, chunk)
    h4 = h2.reshape(batch, n_chunks, chunk, D_MODEL)
    ug = _norm_group(h4, gain)
    y4 = _ssm(ug, *operands)
    return _glu(h4, y4, gain, d_skip, w_glu).reshape(batch * seq, D_MODEL)


def _rope_tables(seq):
    pos = jnp.arange(seq, dtype=F32)
    inv_freq = ROPE_THETA ** (-jnp.arange(0, HEAD_DIM, 2, dtype=F32) / HEAD_DIM)
    ang = pos[:, None] * inv_freq[None, :]
    cos, sin = jnp.cos(ang), jnp.sin(ang)
    reps = LANES // HEAD_DIM
    cos_t = jnp.tile(jnp.concatenate([cos, cos], axis=-1), (1, reps))
    sin_t = jnp.tile(jnp.concatenate([-sin, sin], axis=-1), (1, reps))
    return cos_t, sin_t


def _lambda_init(layer_idx):
    return 0.8 - 0.6 * math.exp(-0.3 * layer_idx)


def kernel(x, norm_mix, norm_ffn, attn_w_qkv, attn_q_gain, attn_k_gain, attn_lambda,
           attn_subln, attn_w_o, ssm_a_re, ssm_a_im, ssm_log_dt, ssm_b_re, ssm_b_im,
           ssm_c_re, ssm_c_im, ssm_d, ssm_w_glu, ffn_w_up, ffn_w_down):
    batch, seq, d = x.shape
    assert d == D_MODEL
    depth = norm_mix.shape[0]
    cos_t, sin_t = _rope_tables(seq)
    reps = LANES // HEAD_DIM
    h2 = x.reshape(batch * seq, D_MODEL)
    for i in range(depth):
        j = i // N_MIXERS
        gain = norm_mix[i][None, :]
        if i % N_MIXERS == 0:
            qg, kg = attn_q_gain[j], attn_k_gain[j]
            w_qk = attn_w_qkv[j][:, :2 * D_MODEL].astype(BF16)
            w_v_t = attn_w_qkv[j][:, 2 * D_MODEL:].T.astype(BF16)
            q, k, vt = _qkv(h2, gain, w_qk, w_v_t, jnp.tile(qg, reps)[None, :],
                            jnp.tile(kg, reps)[None, :], cos_t, sin_t, seq)
            o = _attention(q.reshape(batch, seq, D_MODEL), k.reshape(batch, seq, D_MODEL), vt, qg, kg,
                           attn_lambda[j], attn_subln[j][:, None], _lambda_init(i))
            proj = dict(attn_out=o.reshape(batch * seq, D_MODEL), w_o=attn_w_o[j].astype(BF16))
        else:
            h2 = _ssm_layer(h2, batch, seq, gain, ssm_a_re[j], ssm_a_im[j], ssm_log_dt[j],
                            ssm_b_re[j], ssm_b_im[j], ssm_c_re[j], ssm_c_im[j],
                            ssm_d[j][None, :], ssm_w_glu[j].astype(BF16))
            proj = {}
        h2 = _mlp(h2, norm_ffn[i][None, :], ffn_w_up[i].astype(BF16), ffn_w_down[i].astype(BF16),
                  **proj)
    return h2.reshape(batch, seq, D_MODEL)
```

```python
import functools
import math

import jax
import jax.numpy as jnp
from jax import lax
from jax.experimental import pallas as pl
from jax.experimental.pallas import tpu as pltpu

F32 = jnp.float32
BF16 = jnp.bfloat16

D_MODEL = 1024
N_HEADS = 8
HEAD_DIM = 64
V_HEAD_DIM = 2 * HEAD_DIM
V_ROWS = V_HEAD_DIM + 16
ROPE_THETA = 10000.0
SSM_GROUP = 16
N_GROUPS = D_MODEL // SSM_GROUP
SSM_STATE = 64
D_FF = 4 * D_MODEL
RMS_EPS = 1e-6
Q_SCALE = HEAD_DIM ** -0.5 * math.log2(math.e)
SAFE_LOG2_SCORE = 80.0
N_MIXERS = 2

LANES = 128
SSM_CHUNK = LANES
VMEM_LIMIT = 48 * 1024 * 1024


def _params(*sem):
    return pltpu.CompilerParams(dimension_semantics=sem, vmem_limit_bytes=VMEM_LIMIT)


def _rms(x, gain):
    ms = jnp.mean(x * x, axis=-1, keepdims=True)
    return x * lax.rsqrt(ms + RMS_EPS) * gain


def _dot(a, b):
    return jnp.dot(a, b, preferred_element_type=F32)


def _qkv_kernel(x_ref, g_ref, wqk_ref, wvt_ref, qg_ref, kg_ref, cos_ref, sin_ref,
                q_ref, k_ref, vt_ref):
    tm = x_ref.shape[0]
    xn = _rms(x_ref[...], g_ref[...]).astype(BF16)
    row = lax.broadcasted_iota(jnp.int32, (LANES, LANES), 0)
    col = lax.broadcasted_iota(jnp.int32, (LANES, LANES), 1)
    seg_mean = jnp.where(row // HEAD_DIM == col // HEAD_DIM, 1.0 / HEAD_DIM, 0.0).astype(BF16)
    lane = lax.broadcasted_iota(jnp.int32, (tm, LANES), 1)
    first_half = (lane % HEAD_DIM) < (HEAD_DIM // 2)
    cos = cos_ref[...]
    sin = sin_ref[...]

    def norm_rope(y, gain, scale):
        sq = y * y
        hi = sq.astype(BF16)
        lo = (sq - hi.astype(F32)).astype(BF16)
        ms = _dot(hi, seg_mean) + _dot(lo, seg_mean)
        yn = y * lax.rsqrt(ms + RMS_EPS) * gain
        partner = jnp.where(first_half,
                            pltpu.roll(yn, LANES - HEAD_DIM // 2, 1),
                            pltpu.roll(yn, HEAD_DIM // 2, 1))
        return (yn * cos + partner * sin) * scale

    yq = _dot(xn, wqk_ref[:, 0:D_MODEL])
    for t in range(D_MODEL // LANES):
        sl = slice(t * LANES, (t + 1) * LANES)
        q_ref[:, sl] = norm_rope(yq[:, sl], qg_ref[...], Q_SCALE).astype(BF16)
    yk = _dot(xn, wqk_ref[:, D_MODEL:2 * D_MODEL])
    for t in range(D_MODEL // LANES):
        sl = slice(t * LANES, (t + 1) * LANES)
        k_ref[:, sl] = norm_rope(yk[:, sl], kg_ref[...], 1.0).astype(BF16)
    yvt = lax.dot_general(wvt_ref[...], xn, (((1,), (1,)), ((), ())),
                          preferred_element_type=F32)
    pad_rows = lax.broadcasted_iota(jnp.int32, (V_ROWS - V_HEAD_DIM, tm), 0)
    ones_row = jnp.where(pad_rows == 0, 1.0, 0.0).astype(BF16)
    for t in range(N_HEADS):
        vt_ref[t, 0:V_HEAD_DIM, :] = yvt[t * V_HEAD_DIM:(t + 1) * V_HEAD_DIM, :].astype(BF16)
        vt_ref[t, V_HEAD_DIM:V_ROWS, :] = ones_row


def _qkv(h2, gain, w_qk, w_v_t, q_gain, k_gain, cos_t, sin_t, seq, tm=512):
    tokens = h2.shape[0]
    tm = min(tm, seq)
    assert tokens % tm == 0 and seq % tm == 0
    n_pos_blocks = seq // tm
    tok = lambda i: (i, 0)
    full = lambda i: (0, 0)
    pos = lambda i: (i % n_pos_blocks, 0)
    vt_map = lambda i: (i // n_pos_blocks, 0, i % n_pos_blocks, 0, 0)
    out = jax.ShapeDtypeStruct((tokens, D_MODEL), BF16)
    vt_shape = (tokens // seq, N_HEADS, n_pos_blocks, V_ROWS, tm)
    return pl.pallas_call(
        _qkv_kernel,
        grid=(tokens // tm,),
        in_specs=[
            pl.BlockSpec((tm, D_MODEL), tok),
            pl.BlockSpec((1, D_MODEL), full),
            pl.BlockSpec((D_MODEL, 2 * D_MODEL), full),
            pl.BlockSpec((D_MODEL, D_MODEL), full),
            pl.BlockSpec((1, LANES), full),
            pl.BlockSpec((1, LANES), full),
            pl.BlockSpec((tm, LANES), pos),
            pl.BlockSpec((tm, LANES), pos),
        ],
        out_specs=[pl.BlockSpec((tm, D_MODEL), tok), pl.BlockSpec((tm, D_MODEL), tok),
                   pl.BlockSpec((None, N_HEADS, None, V_ROWS, tm), vt_map)],
        out_shape=[out, out, jax.ShapeDtypeStruct(vt_shape, BF16)],
        compiler_params=_params("parallel"),
        name="qkv_norm_rope",
    )(h2, gain, w_qk, w_v_t, q_gain, k_gain, cos_t, sin_t)


def _stack_q(q_ref, qs_ref):
    tq = q_ref.shape[0]
    q = q_ref[...]
    lane = lax.broadcasted_iota(jnp.int32, q.shape, 1)
    zero = jnp.zeros_like(q)
    qs_ref[0:tq, :] = jnp.where(lane < HEAD_DIM, q, zero)
    qs_ref[tq:2 * tq, :] = jnp.where(lane >= HEAD_DIM, q, zero)


def _attn_kernel(q_ref, k_ref, vt_ref, lam_ref, sg_ref, o_ref, qs_ref, acc_ref, *m_scratch,
                 lam_init, subtract_max):
    tq = q_ref.shape[0]
    n_chunks, _, tk = vt_ref.shape
    _stack_q(q_ref, qs_ref)
    acc_ref[...] = jnp.zeros(acc_ref.shape, F32)
    if subtract_max:
        m_ref, = m_scratch
        m_ref[...] = jnp.full(m_ref.shape, -jnp.inf, F32)

    def chunk(c, carry):
        off = pl.multiple_of(c * tk, tk)
        st = lax.dot_general(k_ref[pl.ds(off, tk), :], qs_ref[...], (((1,), (1,)), ((), ())),
                             preferred_element_type=F32)
        if subtract_max:
            m_prev = m_ref[...]
            m_new = jnp.maximum(m_prev, jnp.max(st, axis=0, keepdims=True))
            acc_ref[...] = acc_ref[...] * jnp.exp2(m_prev - m_new)
            m_ref[...] = m_new
            st = st - m_new
        acc_ref[...] += _dot(vt_ref[c], jnp.exp2(st).astype(BF16))
        return carry

    lax.fori_loop(0, n_chunks, chunk, 0, unroll=1 if subtract_max else 4)

    ot = acc_ref[0:V_HEAD_DIM, :] / acc_ref[V_HEAD_DIM:V_HEAD_DIM + 1, :]
    lv = lam_ref[...]
    lam = (jnp.exp(jnp.sum(lv[0:1] * lv[1:2], axis=1, keepdims=True))
           - jnp.exp(jnp.sum(lv[2:3] * lv[3:4], axis=1, keepdims=True)) + lam_init)
    od = ot[:, 0:tq] - lam * ot[:, tq:2 * tq]
    ms = jnp.mean(od * od, axis=0, keepdims=True)
    on = od * lax.rsqrt(ms + RMS_EPS) * sg_ref[...] * (1.0 - lam_init)
    o_ref[...] = on.T.astype(BF16)


def _attention_call(q, k, vt, lam_vecs, subln_col, lam_init, subtract_max, tq=1024):
    b, s, _ = q.shape
    tq = min(tq, s)
    assert s % tq == 0
    n_chunks, tk = vt.shape[2], vt.shape[4]
    qmap = lambda bi, hi, qi: (bi, qi, hi)
    full = lambda bi, hi, qi: (0, 0)
    scratch = [pltpu.VMEM((2 * tq, LANES), BF16), pltpu.VMEM((V_ROWS, 2 * tq), F32)]
    if subtract_max:
        scratch.append(pltpu.VMEM((1, 2 * tq), F32))
    return pl.pallas_call(
        functools.partial(_attn_kernel, lam_init=lam_init, subtract_max=subtract_max),
        grid=(b, N_HEADS, s // tq),
        in_specs=[
            pl.BlockSpec((None, tq, LANES), qmap),
            pl.BlockSpec((None, s, LANES), lambda bi, hi, qi: (bi, 0, hi)),
            pl.BlockSpec((None, None, n_chunks, V_ROWS, tk), lambda bi, hi, qi: (bi, hi, 0, 0, 0)),
            pl.BlockSpec((4, HEAD_DIM), full),
            pl.BlockSpec((V_HEAD_DIM, 1), full),
        ],
        out_specs=pl.BlockSpec((None, tq, LANES), qmap),
        out_shape=jax.ShapeDtypeStruct((b, s, D_MODEL), BF16),
        scratch_shapes=scratch,
        compiler_params=_params("parallel", "parallel", "parallel"),
        name="diff_attention_online" if subtract_max else "diff_attention_bounded",
    )(q, k, vt, lam_vecs, subln_col)


def _attention(q, k, vt, q_gain, k_gain, lam_vecs, subln_col, lam_init):
    score_bound = (HEAD_DIM * Q_SCALE) * jnp.max(jnp.abs(q_gain)) * jnp.max(jnp.abs(k_gain))
    return lax.cond(score_bound < SAFE_LOG2_SCORE,
                    lambda *a: _attention_call(*a, lam_init, False),
                    lambda *a: _attention_call(*a, lam_init, True),
                    q, k, vt, lam_vecs, subln_col)


MLP_FF_TILE = 1024


def _mlp_body(h, g_ref, wu_ref, wd_ref, out_ref):
    xn = _rms(h, g_ref[...]).astype(BF16)
    acc = h
    for f in range(D_FF // MLP_FF_TILE):
        sl = slice(f * MLP_FF_TILE, (f + 1) * MLP_FF_TILE)
        a = jnp.maximum(_dot(xn, wu_ref[:, sl]), 0.0)
        acc = acc + _dot((a * a).astype(BF16), wd_ref[sl, :])
    out_ref[...] = acc


def _mlp_kernel(h_ref, g_ref, wu_ref, wd_ref, out_ref):
    _mlp_body(h_ref[...], g_ref, wu_ref, wd_ref, out_ref)


def _proj_mlp_kernel(h_ref, o_ref, wo_ref, g_ref, wu_ref, wd_ref, out_ref):
    _mlp_body(h_ref[...] + _dot(o_ref[...], wo_ref[...]), g_ref, wu_ref, wd_ref, out_ref)


def _resident(shape):
    return pl.BlockSpec(shape, lambda i: (0,) * len(shape), pipeline_mode=pl.Buffered(1))


def _mlp(h2, gain, w_up, w_down, attn_out=None, w_o=None, tm=512):
    tokens = h2.shape[0]
    tok = pl.BlockSpec((tm, D_MODEL), lambda i: (i, 0))
    mlp_specs = [_resident((1, D_MODEL)), _resident((D_MODEL, D_FF)), _resident((D_FF, D_MODEL))]
    if attn_out is None:
        body, name = _mlp_kernel, "relu2_mlp"
        in_specs, args = [tok] + mlp_specs, (h2, gain, w_up, w_down)
    else:
        body, name = _proj_mlp_kernel, "attn_proj_relu2_mlp"
        in_specs = [tok, tok, _resident((D_MODEL, D_MODEL))] + mlp_specs
        args = (h2, attn_out, w_o, gain, w_up, w_down)
    return pl.pallas_call(
        body,
        grid=(tokens // tm,),
        in_specs=in_specs,
        out_specs=tok,
        out_shape=jax.ShapeDtypeStruct((tokens, D_MODEL), F32),
        compiler_params=_params("parallel"),
        name=name,
    )(*args)


def _norm_group_kernel(h_ref, g_ref, u_ref):
    chunk = h_ref.shape[1]
    for b in range(h_ref.shape[0]):
        ut = _rms(h_ref[b], g_ref[...]).T
        u_ref[:, b] = ut.reshape(N_GROUPS, SSM_GROUP, chunk)


def _norm_group(h4, gain):
    batch, n_chunks, chunk, _ = h4.shape
    return pl.pallas_call(
        _norm_group_kernel,
        grid=(n_chunks,),
        in_specs=[pl.BlockSpec((batch, None, chunk, D_MODEL), lambda c: (0, c, 0, 0)),
                  pl.BlockSpec((1, D_MODEL), lambda c: (0, 0))],
        out_specs=pl.BlockSpec((N_GROUPS, None, batch, SSM_GROUP, chunk), lambda c: (0, c, 0, 0, 0)),
        out_shape=jax.ShapeDtypeStruct((N_GROUPS, n_chunks, batch, SSM_GROUP, chunk), F32),
        compiler_params=_params("parallel"),
        name="ssm_pre_norm",
    )(h4, gain)


def _ssm_kernel(u_ref, e_ref, pin_ref, bb_ref, pout_ref, cc_ref, al_ref, y_ref,
                t_ref, zs_ref, sfr_ref, sbr_ref, sfi_ref, sbi_ref):
    n_chunks, batch, _, chunk = u_ref.shape
    rows, width = n_chunks * batch, SSM_GROUP * chunk
    p = SSM_STATE
    assert chunk == LANES

    upper = (lax.broadcasted_iota(jnp.int32, (chunk, chunk), 1)
             >= lax.broadcasted_iota(jnp.int32, (chunk, chunk), 0))

    def build_t_rows(h):
        for ho in range(SSM_GROUP):
            k_bwd = jnp.broadcast_to(e_ref[h, ho:ho + 1, 0:chunk], (chunk, chunk))
            k_fwd = jnp.broadcast_to(e_ref[h, ho:ho + 1, chunk:2 * chunk], (chunk, chunk))
            t_fwd = pltpu.roll(k_fwd, 0, 1, stride=1, stride_axis=0)
            t_bwd = pltpu.roll(k_bwd, 0, 1, stride=1, stride_axis=0)
            t_ref[chunk * h:chunk * (h + 1), chunk * ho:chunk * (ho + 1)] = (
                jnp.where(upper, t_fwd, t_bwd).astype(BF16))

    def outer(pow_ref, vec_ref):
        ar_, ai_ = pow_ref[0][None], pow_ref[1][None]
        vr_, vi_ = vec_ref[0][:, None, :], vec_ref[1][:, None, :]
        return ((ar_ * vr_ - ai_ * vi_).reshape(width, 2 * p),
                (ar_ * vi_ + ai_ * vr_).reshape(width, 2 * p))

    wi_re, wi_im = outer(pin_ref, bb_ref)
    w_in = jnp.concatenate([wi_re, wi_im], axis=1).astype(BF16)
    wo_re, wo_im = outer(pout_ref, cc_ref)
    w_out_t = jnp.concatenate([wo_re, -wo_im], axis=1).astype(BF16)

    u = jnp.concatenate([u_ref[:, :, h, :].reshape(rows, chunk) for h in range(SSM_GROUP)],
                        axis=1).astype(BF16)
    zs_ref[...] = _dot(u, w_in)

    ar = al_ref[0:1, :]
    ai = al_ref[1:2, :]
    is_fwd = lax.broadcasted_iota(jnp.int32, (batch, 2 * p), 1) < p

    def step(kk, carry):
        xr, xi = carry
        rf = pl.multiple_of(kk * batch, batch)
        rb = pl.multiple_of((n_chunks - 1 - kk) * batch, batch)
        lr = jnp.where(is_fwd, zs_ref[pl.ds(rf, batch), 0:2 * p], zs_ref[pl.ds(rb, batch), 0:2 * p])
        li = jnp.where(is_fwd, zs_ref[pl.ds(rf, batch), 2 * p:4 * p],
                       zs_ref[pl.ds(rb, batch), 2 * p:4 * p])
        sfr_ref[pl.ds(rf, batch), :] = xr
        sbr_ref[pl.ds(rb, batch), :] = xr
        sfi_ref[pl.ds(rf, batch), :] = xi
        sbi_ref[pl.ds(rb, batch), :] = xi
        return ar * xr - ai * xi + lr, ar * xi + ai * xr + li

    zero = jnp.zeros((batch, 2 * p), F32)
    lax.fori_loop(0, n_chunks, step, (zero, zero))

    fwd_lane = lax.broadcasted_iota(jnp.int32, (rows, 2 * p), 1) < p
    xin = jnp.concatenate(
        [jnp.where(fwd_lane, sfr_ref[...], sbr_ref[...]),
         jnp.where(fwd_lane, sfi_ref[...], sbi_ref[...])], axis=1).astype(BF16)
    carry_out = lax.dot_general(xin, w_out_t, (((1,), (1,)), ((), ())), preferred_element_type=F32)
    y = carry_out
    slab = 2 * chunk
    for hb in range(SSM_GROUP // 2):
        build_t_rows(2 * hb)
        build_t_rows(2 * hb + 1)
        y = y + _dot(u[:, slab * hb:slab * (hb + 1)], t_ref[slab * hb:slab * (hb + 1), :])
    for h in range(SSM_GROUP):
        y_ref[h] = y[:, chunk * h:chunk * (h + 1)]


def _ssm(ug, e_rows, pow_in, b_bar, pow_out, c_mat, a_chunk):
    g, n_chunks, batch, _, chunk = ug.shape
    rows, width = n_chunks * batch, SSM_GROUP * chunk
    p = SSM_STATE
    grp = lambda i: (i, 0, 0)
    grp4 = lambda i: (i, 0, 0, 0)
    return pl.pallas_call(
        _ssm_kernel,
        grid=(g,),
        in_specs=[
            pl.BlockSpec((None, n_chunks, batch, SSM_GROUP, chunk), lambda i: (i, 0, 0, 0, 0)),
            pl.BlockSpec((None, SSM_GROUP, SSM_GROUP, 2 * chunk), grp4),
            pl.BlockSpec((None, 2, chunk, 2 * p), grp4),
            pl.BlockSpec((None, 2, SSM_GROUP, 2 * p), grp4),
            pl.BlockSpec((None, 2, chunk, 2 * p), grp4),
            pl.BlockSpec((None, 2, SSM_GROUP, 2 * p), grp4),
            pl.BlockSpec((None, 2, 2 * p), grp),
        ],
        out_specs=pl.BlockSpec((None, SSM_GROUP, rows, chunk), grp4),
        out_shape=jax.ShapeDtypeStruct((g, SSM_GROUP, rows, chunk), F32),
        scratch_shapes=[
            pltpu.VMEM((width, width), BF16),
            pltpu.VMEM((rows, 4 * p), F32),
            pltpu.VMEM((rows, 2 * p), F32),
            pltpu.VMEM((rows, 2 * p), F32),
            pltpu.VMEM((rows, 2 * p), F32),
            pltpu.VMEM((rows, 2 * p), F32),
        ],
        compiler_params=_params("parallel"),
        name="s5_chunked_scan",
    )(ug, e_rows, pow_in, b_bar, pow_out, c_mat, a_chunk)


def _glu_kernel(h_ref, y_ref, g_ref, d_ref, w_ref, out_ref):
    batch, chunk, _ = h_ref.shape
    acts = []
    for b in range(batch):
        u = _rms(h_ref[b], g_ref[...])
        y = y_ref[:, :, b, :].reshape(D_MODEL, chunk).T
        acts.append(jax.nn.gelu(y + d_ref[...] * u).astype(BF16))
    z = _dot(jnp.concatenate(acts, axis=0), w_ref[...])
    for b in range(batch):
        zb = z[b * chunk:(b + 1) * chunk]
        out_ref[b] = h_ref[b] + zb[:, 0:D_MODEL] * jax.nn.sigmoid(zb[:, D_MODEL:2 * D_MODEL])


def _glu(h4, y4, gain, d_skip, w_glu):
    batch, n_chunks, chunk, _ = h4.shape
    tok = pl.BlockSpec((batch, None, chunk, D_MODEL), lambda c: (0, c, 0, 0))
    return pl.pallas_call(
        _glu_kernel,
        grid=(n_chunks,),
        in_specs=[
            tok,
            pl.BlockSpec((N_GROUPS, SSM_GROUP, batch, chunk), lambda c: (0, 0, c, 0)),
            _resident((1, D_MODEL)),
            _resident((1, D_MODEL)),
            _resident((D_MODEL, 2 * D_MODEL)),
        ],
        out_specs=tok,
        out_shape=jax.ShapeDtypeStruct(h4.shape, F32),
        compiler_params=_params("parallel"),
        name="ssm_gelu_glu",
    )(h4, y4, gain, d_skip, w_glu)


def _ssm_operands(a_re, a_im, log_dt, b_re, b_im, c_re, c_im, chunk):
    hp = lax.Precision.HIGHEST
    dt = jnp.exp(log_dt)[..., None]
    zr, zi = a_re * dt, a_im * dt
    t = jnp.arange(chunk + 1, dtype=F32)[None, None, :, None]
    mag = jnp.exp(zr[:, :, None, :] * t)
    ang = zi[:, :, None, :] * t
    pr, pi = mag * jnp.cos(ang), mag * jnp.sin(ang)
    nr, ni = pr[:, :, 1, :] - 1.0, pi[:, :, 1, :]
    den = a_re * a_re + a_im * a_im
    qr = (nr * a_re + ni * a_im) / den
    qi = (ni * a_re - nr * a_im) / den
    bbr = qr[..., None] * b_re - qi[..., None] * b_im
    bbi = qr[..., None] * b_im + qi[..., None] * b_re

    car = c_re[:, :, None] * pr[:, :, :chunk, None, :] - c_im[:, :, None] * pi[:, :, :chunk, None, :]
    cai = c_re[:, :, None] * pi[:, :, :chunk, None, :] + c_im[:, :, None] * pr[:, :, :chunk, None, :]
    kern = jnp.einsum('dgtop,dgpi->dgtoi', jnp.concatenate([car, -cai], axis=-1),
                      jnp.concatenate([bbr, bbi], axis=2), precision=hp)
    center = (kern[0, :, 0] + kern[1, :, 0])[:, None]
    lagged = jnp.concatenate([jnp.zeros_like(center), kern[1, :, :0:-1], center, kern[0, :, 1:]],
                             axis=1)
    e_rows = lagged.transpose(0, 3, 2, 1)

    def fwd_bwd(x):
        return jnp.concatenate([x[0], x[1]], axis=-1)

    def re_im(xr, xi):
        return jnp.stack([fwd_bwd(xr), fwd_bwd(xi)], axis=1)

    pow_in = re_im(jnp.stack([pr[0, :, chunk - 1::-1], pr[1, :, :chunk]]),
                   jnp.stack([pi[0, :, chunk - 1::-1], pi[1, :, :chunk]]))
    b_bar = re_im(bbr.transpose(0, 1, 3, 2), bbi.transpose(0, 1, 3, 2))
    pow_out = re_im(jnp.stack([pr[0, :, 1:], pr[1, :, :0:-1]]),
                    jnp.stack([pi[0, :, 1:], pi[1, :, :0:-1]]))
    c_mat = re_im(c_re, c_im)
    a_chunk = jnp.stack([jnp.concatenate([pr[0, :, chunk], pr[1, :, chunk]], axis=-1),
                         jnp.concatenate([pi[0, :, chunk], pi[1, :, chunk]], axis=-1)], axis=1)
    return e_rows, pow_in, b_bar, pow_out, c_mat, a_chunk


def _ssm_layer(h2, batch, seq, gain, a_re, a_im, log_dt, b_re, b_im, c_re, c_im, d_skip, w_glu):
    chunk = SSM_CHUNK
    n_chunks = seq // chunk
    operands = _ssm_operands(a_re, a_im, log_dt, b_re, b_im, c_re, c_im, chunk)
    h4 = h2.reshape(batch, n_chunks, chunk, D_MODEL)
    ug = _norm_group(h4, gain)
    y4 = _ssm(ug, *operands)
    return _glu(h4, y4, gain, d_skip, w_glu).reshape(batch * seq, D_MODEL)


def _rope_tables(seq):
    pos = jnp.arange(seq, dtype=F32)
    inv_freq = ROPE_THETA ** (-jnp.arange(0, HEAD_DIM, 2, dtype=F32) / HEAD_DIM)
    ang = pos[:, None] * inv_freq[None, :]
    cos, sin = jnp.cos(ang), jnp.sin(ang)
    reps = LANES // HEAD_DIM
    cos_t = jnp.tile(jnp.concatenate([cos, cos], axis=-1), (1, reps))
    sin_t = jnp.tile(jnp.concatenate([-sin, sin], axis=-1), (1, reps))
    return cos_t, sin_t


def _lambda_init(layer_idx):
    return 0.8 - 0.6 * math.exp(-0.3 * layer_idx)


def kernel(x, norm_mix, norm_ffn, attn_w_qkv, attn_q_gain, attn_k_gain, attn_lambda,
           attn_subln, attn_w_o, ssm_a_re, ssm_a_im, ssm_log_dt, ssm_b_re, ssm_b_im,
           ssm_c_re, ssm_c_im, ssm_d, ssm_w_glu, ffn_w_up, ffn_w_down):
    batch, seq, d = x.shape
    assert d == D_MODEL
    depth = norm_mix.shape[0]
    cos_t, sin_t = _rope_tables(seq)
    reps = LANES // HEAD_DIM
    h2 = x.reshape(batch * seq, D_MODEL)
    for i in range(depth):
        j = i // N_MIXERS
        gain = norm_mix[i][None, :]
        if i % N_MIXERS == 0:
            qg, kg = attn_q_gain[j], attn_k_gain[j]
            w_qk = attn_w_qkv[j][:, :2 * D_MODEL].astype(BF16)
            w_v_t = attn_w_qkv[j][:, 2 * D_MODEL:].T.astype(BF16)
            q, k, vt = _qkv(h2, gain, w_qk, w_v_t, jnp.tile(qg, reps)[None, :],
                            jnp.tile(kg, reps)[None, :], cos_t, sin_t, seq)
            o = _attention(q.reshape(batch, seq, D_MODEL), k.reshape(batch, seq, D_MODEL), vt, qg, kg,
                           attn_lambda[j], attn_subln[j][:, None], _lambda_init(i))
            proj = dict(attn_out=o.reshape(batch * seq, D_MODEL), w_o=attn_w_o[j].astype(BF16))
        else:
            h2 = _ssm_layer(h2, batch, seq, gain, ssm_a_re[j], ssm_a_im[j], ssm_log_dt[j],
                            ssm_b_re[j], ssm_b_im[j], ssm_c_re[j], ssm_c_im[j],
                            ssm_d[j][None, :], ssm_w_glu[j].astype(BF16))
            proj = {}
        h2 = _mlp(h2, norm_ffn[i][None, :], ffn_w_up[i].astype(BF16), ffn_w_down[i].astype(BF16),
                  **proj)
    return h2.reshape(batch, seq, D_MODEL)
```

```python
import functools
import math

import jax
import jax.numpy as jnp
from jax import lax
from jax.experimental import pallas as pl
from jax.experimental.pallas import tpu as pltpu

F32 = jnp.float32
BF16 = jnp.bfloat16

D_MODEL = 1024
N_HEADS = 8
HEAD_DIM = 64
V_HEAD_DIM = 2 * HEAD_DIM
V_ROWS = V_HEAD_DIM + 16
ROPE_THETA = 10000.0
SSM_GROUP = 16
N_GROUPS = D_MODEL // SSM_GROUP
SSM_STATE = 64
D_FF = 4 * D_MODEL
RMS_EPS = 1e-6
Q_SCALE = HEAD_DIM ** -0.5 * math.log2(math.e)
SAFE_LOG2_SCORE = 80.0
N_MIXERS = 2

LANES = 128
MXU_DIM = 256
SSM_CHUNK = LANES
VMEM_LIMIT = 48 * 1024 * 1024


def _params(*sem):
    return pltpu.CompilerParams(dimension_semantics=sem, vmem_limit_bytes=VMEM_LIMIT)


def _rms(x, gain):
    ms = jnp.mean(x * x, axis=-1, keepdims=True)
    return x * lax.rsqrt(ms + RMS_EPS) * gain


def _dot(a, b):
    return jnp.dot(a, b, preferred_element_type=F32)


def _qkv_kernel(x_ref, g_ref, wqk_ref, wvt_ref, qg_ref, kg_ref, cos_ref, sin_ref,
                q_ref, k_ref, vt_ref):
    tm = x_ref.shape[0]
    xn = _rms(x_ref[...], g_ref[...]).astype(BF16)
    row = lax.broadcasted_iota(jnp.int32, (MXU_DIM, MXU_DIM), 0)
    col = lax.broadcasted_iota(jnp.int32, (MXU_DIM, MXU_DIM), 1)
    seg_mean = jnp.where(row // HEAD_DIM == col // HEAD_DIM, 1.0 / HEAD_DIM, 0.0).astype(BF16)
    lane = lax.broadcasted_iota(jnp.int32, (tm, LANES), 1)
    first_half = (lane % HEAD_DIM) < (HEAD_DIM // 2)
    cos = cos_ref[...]
    sin = sin_ref[...]

    def norm_rope(y, gain, scale, out_ref):
        sq = (y * y).astype(BF16)
        for t in range(D_MODEL // MXU_DIM):
            ms = _dot(sq[:, t * MXU_DIM:(t + 1) * MXU_DIM], seg_mean)
            for s in range(MXU_DIM // LANES):
                sl = slice(t * MXU_DIM + s * LANES, t * MXU_DIM + (s + 1) * LANES)
                yn = y[:, sl] * lax.rsqrt(ms[:, s * LANES:(s + 1) * LANES] + RMS_EPS) * gain
                partner = jnp.where(first_half,
                                    pltpu.roll(yn, LANES - HEAD_DIM // 2, 1),
                                    pltpu.roll(yn, HEAD_DIM // 2, 1))
                out_ref[:, sl] = ((yn * cos + partner * sin) * scale).astype(BF16)

    norm_rope(_dot(xn, wqk_ref[:, 0:D_MODEL]), qg_ref[...], Q_SCALE, q_ref)
    norm_rope(_dot(xn, wqk_ref[:, D_MODEL:2 * D_MODEL]), kg_ref[...], 1.0, k_ref)
    yvt = lax.dot_general(wvt_ref[...], xn, (((1,), (1,)), ((), ())),
                          preferred_element_type=F32)
    pad_rows = lax.broadcasted_iota(jnp.int32, (V_ROWS - V_HEAD_DIM, tm), 0)
    ones_row = jnp.where(pad_rows == 0, 1.0, 0.0).astype(BF16)
    for t in range(N_HEADS):
        vt_ref[t, 0:V_HEAD_DIM, :] = yvt[t * V_HEAD_DIM:(t + 1) * V_HEAD_DIM, :].astype(BF16)
        vt_ref[t, V_HEAD_DIM:V_ROWS, :] = ones_row


def _qkv(h2, gain, w_qk, w_v_t, q_gain, k_gain, cos_t, sin_t, seq, tm=512):
    tokens = h2.shape[0]
    tm = min(tm, seq)
    assert tokens % tm == 0 and seq % tm == 0
    n_pos_blocks = seq // tm
    tok = lambda i: (i, 0)
    full = lambda i: (0, 0)
    pos = lambda i: (i % n_pos_blocks, 0)
    vt_map = lambda i: (i // n_pos_blocks, 0, i % n_pos_blocks, 0, 0)
    out = jax.ShapeDtypeStruct((tokens, D_MODEL), BF16)
    vt_shape = (tokens // seq, N_HEADS, n_pos_blocks, V_ROWS, tm)
    return pl.pallas_call(
        _qkv_kernel,
        grid=(tokens // tm,),
        in_specs=[
            pl.BlockSpec((tm, D_MODEL), tok),
            pl.BlockSpec((1, D_MODEL), full),
            pl.BlockSpec((D_MODEL, 2 * D_MODEL), full),
            pl.BlockSpec((D_MODEL, D_MODEL), full),
            pl.BlockSpec((1, LANES), full),
            pl.BlockSpec((1, LANES), full),
            pl.BlockSpec((tm, LANES), pos),
            pl.BlockSpec((tm, LANES), pos),
        ],
        out_specs=[pl.BlockSpec((tm, D_MODEL), tok), pl.BlockSpec((tm, D_MODEL), tok),
                   pl.BlockSpec((None, N_HEADS, None, V_ROWS, tm), vt_map)],
        out_shape=[out, out, jax.ShapeDtypeStruct(vt_shape, BF16)],
        compiler_params=_params("parallel"),
        name="qkv_norm_rope",
    )(h2, gain, w_qk, w_v_t, q_gain, k_gain, cos_t, sin_t)


def _stack_q(q_ref, qs_ref):
    tq = q_ref.shape[0]
    q = q_ref[...]
    lane = lax.broadcasted_iota(jnp.int32, q.shape, 1)
    zero = jnp.zeros_like(q)
    qs_ref[0:tq, :] = jnp.where(lane < HEAD_DIM, q, zero)
    qs_ref[tq:2 * tq, :] = jnp.where(lane >= HEAD_DIM, q, zero)


def _attn_kernel(q_ref, k_ref, vt_ref, lam_ref, sg_ref, o_ref, qs_ref, acc_ref, *m_scratch,
                 lam_init, subtract_max):
    tq = q_ref.shape[0]
    n_chunks, _, tk = vt_ref.shape
    _stack_q(q_ref, qs_ref)
    acc_ref[...] = jnp.zeros(acc_ref.shape, F32)
    if subtract_max:
        m_ref, = m_scratch
        m_ref[...] = jnp.full(m_ref.shape, -jnp.inf, F32)

    def chunk(c, carry):
        off = pl.multiple_of(c * tk, tk)
        st = lax.dot_general(k_ref[pl.ds(off, tk), :], qs_ref[...], (((1,), (1,)), ((), ())),
                             preferred_element_type=F32)
        if subtract_max:
            m_prev = m_ref[...]
            m_new = jnp.maximum(m_prev, jnp.max(st, axis=0, keepdims=True))
            acc_ref[...] = acc_ref[...] * jnp.exp2(m_prev - m_new)
            m_ref[...] = m_new
            st = st - m_new
        acc_ref[...] += _dot(vt_ref[c], jnp.exp2(st).astype(BF16))
        return carry

    lax.fori_loop(0, n_chunks, chunk, 0, unroll=1 if subtract_max else 4)

    ot = acc_ref[0:V_HEAD_DIM, :] / acc_ref[V_HEAD_DIM:V_HEAD_DIM + 1, :]
    lv = lam_ref[...]
    lam = (jnp.exp(jnp.sum(lv[0:1] * lv[1:2], axis=1, keepdims=True))
           - jnp.exp(jnp.sum(lv[2:3] * lv[3:4], axis=1, keepdims=True)) + lam_init)
    od = ot[:, 0:tq] - lam * ot[:, tq:2 * tq]
    ms = jnp.mean(od * od, axis=0, keepdims=True)
    on = od * lax.rsqrt(ms + RMS_EPS) * sg_ref[...] * (1.0 - lam_init)
    o_ref[...] = on.T.astype(BF16)


def _attention_call(q, k, vt, lam_vecs, subln_col, lam_init, subtract_max, tq=1024):
    b, s, _ = q.shape
    tq = min(tq, s)
    assert s % tq == 0
    n_chunks, tk = vt.shape[2], vt.shape[4]
    qmap = lambda bi, hi, qi: (bi, qi, hi)
    full = lambda bi, hi, qi: (0, 0)
    scratch = [pltpu.VMEM((2 * tq, LANES), BF16), pltpu.VMEM((V_ROWS, 2 * tq), F32)]
    if subtract_max:
        scratch.append(pltpu.VMEM((1, 2 * tq), F32))
    return pl.pallas_call(
        functools.partial(_attn_kernel, lam_init=lam_init, subtract_max=subtract_max),
        grid=(b, N_HEADS, s // tq),
        in_specs=[
            pl.BlockSpec((None, tq, LANES), qmap),
            pl.BlockSpec((None, s, LANES), lambda bi, hi, qi: (bi, 0, hi)),
            pl.BlockSpec((None, None, n_chunks, V_ROWS, tk), lambda bi, hi, qi: (bi, hi, 0, 0, 0)),
            pl.BlockSpec((4, HEAD_DIM), full),
            pl.BlockSpec((V_HEAD_DIM, 1), full),
        ],
        out_specs=pl.BlockSpec((None, tq, LANES), qmap),
        out_shape=jax.ShapeDtypeStruct((b, s, D_MODEL), BF16),
        scratch_shapes=scratch,
        compiler_params=_params("parallel", "parallel", "parallel"),
        name="diff_attention_online" if subtract_max else "diff_attention_bounded",
    )(q, k, vt, lam_vecs, subln_col)


def _attention(q, k, vt, q_gain, k_gain, lam_vecs, subln_col, lam_init):
    score_bound = (HEAD_DIM * Q_SCALE) * jnp.max(jnp.abs(q_gain)) * jnp.max(jnp.abs(k_gain))
    return lax.cond(score_bound < SAFE_LOG2_SCORE,
                    lambda *a: _attention_call(*a, lam_init, False),
                    lambda *a: _attention_call(*a, lam_init, True),
                    q, k, vt, lam_vecs, subln_col)


MLP_FF_TILE = 1024


def _mlp_body(h, g_ref, wu_ref, wd_ref, out_ref):
    xn = _rms(h, g_ref[...]).astype(BF16)
    acc = h
    for f in range(D_FF // MLP_FF_TILE):
        sl = slice(f * MLP_FF_TILE, (f + 1) * MLP_FF_TILE)
        a = jnp.maximum(_dot(xn, wu_ref[:, sl]), 0.0)
        acc = acc + _dot((a * a).astype(BF16), wd_ref[sl, :])
    out_ref[...] = acc


def _mlp_kernel(h_ref, g_ref, wu_ref, wd_ref, out_ref):
    _mlp_body(h_ref[...], g_ref, wu_ref, wd_ref, out_ref)


def _proj_mlp_kernel(h_ref, o_ref, wo_ref, g_ref, wu_ref, wd_ref, out_ref):
    _mlp_body(h_ref[...] + _dot(o_ref[...], wo_ref[...]), g_ref, wu_ref, wd_ref, out_ref)


def _resident(shape):
    return pl.BlockSpec(shape, lambda i: (0,) * len(shape), pipeline_mode=pl.Buffered(1))


def _mlp(h2, gain, w_up, w_down, attn_out=None, w_o=None, tm=512):
    tokens = h2.shape[0]
    tok = pl.BlockSpec((tm, D_MODEL), lambda i: (i, 0))
    mlp_specs = [_resident((1, D_MODEL)), _resident((D_MODEL, D_FF)), _resident((D_FF, D_MODEL))]
    if attn_out is None:
        body, name = _mlp_kernel, "relu2_mlp"
        in_specs, args = [tok] + mlp_specs, (h2, gain, w_up, w_down)
    else:
        body, name = _proj_mlp_kernel, "attn_proj_relu2_mlp"
        in_specs = [tok, tok, _resident((D_MODEL, D_MODEL))] + mlp_specs
        args = (h2, attn_out, w_o, gain, w_up, w_down)
    return pl.pallas_call(
        body,
        grid=(tokens // tm,),
        in_specs=in_specs,
        out_specs=tok,
        out_shape=jax.ShapeDtypeStruct((tokens, D_MODEL), F32),
        compiler_params=_params("parallel"),
        name=name,
    )(*args)


def _norm_group_kernel(h_ref, g_ref, u_ref):
    chunk = h_ref.shape[1]
    for b in range(h_ref.shape[0]):
        ut = _rms(h_ref[b], g_ref[...]).T
        u_ref[:, b] = ut.reshape(N_GROUPS, SSM_GROUP, chunk)


def _norm_group(h4, gain):
    batch, n_chunks, chunk, _ = h4.shape
    return pl.pallas_call(
        _norm_group_kernel,
        grid=(n_chunks,),
        in_specs=[pl.BlockSpec((batch, None, chunk, D_MODEL), lambda c: (0, c, 0, 0)),
                  pl.BlockSpec((1, D_MODEL), lambda c: (0, 0))],
        out_specs=pl.BlockSpec((N_GROUPS, None, batch, SSM_GROUP, chunk), lambda c: (0, c, 0, 0, 0)),
        out_shape=jax.ShapeDtypeStruct((N_GROUPS, n_chunks, batch, SSM_GROUP, chunk), F32),
        compiler_params=_params("parallel"),
        name="ssm_pre_norm",
    )(h4, gain)


def _ssm_kernel(u_ref, e_ref, e_next_ref, pin_ref, bb_ref, pout_ref, cc_ref, al_ref, y_ref,
                t_even_ref, t_odd_ref, zs_ref, sfr_ref, sbr_ref, sfi_ref, sbi_ref):
    n_chunks, batch, _, chunk = u_ref.shape
    rows, width = n_chunks * batch, SSM_GROUP * chunk
    p = SSM_STATE
    assert chunk == LANES

    upper = (lax.broadcasted_iota(jnp.int32, (chunk, chunk), 1)
             >= lax.broadcasted_iota(jnp.int32, (chunk, chunk), 0))

    def build_t(lag_ref, t_ref):
        for h in range(SSM_GROUP):
            for ho in range(SSM_GROUP):
                k_bwd = jnp.broadcast_to(lag_ref[h, ho:ho + 1, 0:chunk], (chunk, chunk))
                k_fwd = jnp.broadcast_to(lag_ref[h, ho:ho + 1, chunk:2 * chunk], (chunk, chunk))
                t_fwd = pltpu.roll(k_fwd, 0, 1, stride=1, stride_axis=0)
                t_bwd = pltpu.roll(k_bwd, 0, 1, stride=1, stride_axis=0)
                t_ref[chunk * h:chunk * (h + 1), chunk * ho:chunk * (ho + 1)] = (
                    jnp.where(upper, t_fwd, t_bwd).astype(BF16))

    group = pl.program_id(0)

    @pl.when(group == 0)
    def _():
        build_t(e_ref, t_even_ref)

    def outer(pow_ref, vec_ref):
        ar_, ai_ = pow_ref[0][None], pow_ref[1][None]
        vr_, vi_ = vec_ref[0][:, None, :], vec_ref[1][:, None, :]
        return ((ar_ * vr_ - ai_ * vi_).reshape(width, 2 * p),
                (ar_ * vi_ + ai_ * vr_).reshape(width, 2 * p))

    wi_re, wi_im = outer(pin_ref, bb_ref)
    w_in = jnp.concatenate([wi_re, wi_im], axis=1).astype(BF16)
    wo_re, wo_im = outer(pout_ref, cc_ref)
    w_out_t = jnp.concatenate([wo_re, -wo_im], axis=1).astype(BF16)

    u = jnp.concatenate([u_ref[:, :, h, :].reshape(rows, chunk) for h in range(SSM_GROUP)],
                        axis=1).astype(BF16)
    zs_ref[...] = _dot(u, w_in)

    ar = al_ref[0:1, :]
    ai = al_ref[1:2, :]
    is_fwd = lax.broadcasted_iota(jnp.int32, (batch, 2 * p), 1) < p

    def step(kk, carry):
        xr, xi = carry
        rf = pl.multiple_of(kk * batch, batch)
        rb = pl.multiple_of((n_chunks - 1 - kk) * batch, batch)
        lr = jnp.where(is_fwd, zs_ref[pl.ds(rf, batch), 0:2 * p], zs_ref[pl.ds(rb, batch), 0:2 * p])
        li = jnp.where(is_fwd, zs_ref[pl.ds(rf, batch), 2 * p:4 * p],
                       zs_ref[pl.ds(rb, batch), 2 * p:4 * p])
        sfr_ref[pl.ds(rf, batch), :] = xr
        sbr_ref[pl.ds(rb, batch), :] = xr
        sfi_ref[pl.ds(rf, batch), :] = xi
        sbi_ref[pl.ds(rb, batch), :] = xi
        return ar * xr - ai * xi + lr, ar * xi + ai * xr + li

    zero = jnp.zeros((batch, 2 * p), F32)
    lax.fori_loop(0, n_chunks, step, (zero, zero))

    fwd_lane = lax.broadcasted_iota(jnp.int32, (rows, 2 * p), 1) < p
    xin = jnp.concatenate(
        [jnp.where(fwd_lane, sfr_ref[...], sbr_ref[...]),
         jnp.where(fwd_lane, sfi_ref[...], sbi_ref[...])], axis=1).astype(BF16)
    carry_out = lax.dot_general(xin, w_out_t, (((1,), (1,)), ((), ())), preferred_element_type=F32)

    def finish(t_ref, t_next_ref):
        build_t(e_next_ref, t_next_ref)
        y = _dot(u, t_ref[...]) + carry_out
        for h in range(SSM_GROUP):
            y_ref[h] = y[:, chunk * h:chunk * (h + 1)]

    @pl.when(group % 2 == 0)
    def _():
        finish(t_even_ref, t_odd_ref)

    @pl.when(group % 2 == 1)
    def _():
        finish(t_odd_ref, t_even_ref)


def _ssm(ug, e_rows, pow_in, b_bar, pow_out, c_mat, a_chunk):
    g, n_chunks, batch, _, chunk = ug.shape
    rows, width = n_chunks * batch, SSM_GROUP * chunk
    p = SSM_STATE
    grp = lambda i: (i, 0, 0)
    grp4 = lambda i: (i, 0, 0, 0)
    return pl.pallas_call(
        _ssm_kernel,
        grid=(g,),
        in_specs=[
            pl.BlockSpec((None, n_chunks, batch, SSM_GROUP, chunk), lambda i: (i, 0, 0, 0, 0)),
            pl.BlockSpec((None, SSM_GROUP, SSM_GROUP, 2 * chunk), grp4),
            pl.BlockSpec((None, SSM_GROUP, SSM_GROUP, 2 * chunk),
                         lambda i: (jnp.minimum(i + 1, g - 1), 0, 0, 0)),
            pl.BlockSpec((None, 2, chunk, 2 * p), grp4),
            pl.BlockSpec((None, 2, SSM_GROUP, 2 * p), grp4),
            pl.BlockSpec((None, 2, chunk, 2 * p), grp4),
            pl.BlockSpec((None, 2, SSM_GROUP, 2 * p), grp4),
            pl.BlockSpec((None, 2, 2 * p), grp),
        ],
        out_specs=pl.BlockSpec((None, SSM_GROUP, rows, chunk), grp4),
        out_shape=jax.ShapeDtypeStruct((g, SSM_GROUP, rows, chunk), F32),
        scratch_shapes=[
            pltpu.VMEM((width, width), BF16),
            pltpu.VMEM((width, width), BF16),
            pltpu.VMEM((rows, 4 * p), F32),
            pltpu.VMEM((rows, 2 * p), F32),
            pltpu.VMEM((rows, 2 * p), F32),
            pltpu.VMEM((rows, 2 * p), F32),
            pltpu.VMEM((rows, 2 * p), F32),
        ],
        compiler_params=_params("arbitrary"),
        name="s5_chunked_scan",
    )(ug, e_rows, e_rows, pow_in, b_bar, pow_out, c_mat, a_chunk)


def _glu_kernel(h_ref, y_ref, g_ref, d_ref, w_ref, out_ref):
    batch, chunk, _ = h_ref.shape
    acts = []
    for b in range(batch):
        u = _rms(h_ref[b], g_ref[...])
        y = y_ref[:, :, b, :].reshape(D_MODEL, chunk).T
        acts.append(jax.nn.gelu(y + d_ref[...] * u).astype(BF16))
    z = _dot(jnp.concatenate(acts, axis=0), w_ref[...])
    for b in range(batch):
        zb = z[b * chunk:(b + 1) * chunk]
        out_ref[b] = h_ref[b] + zb[:, 0:D_MODEL] * jax.nn.sigmoid(zb[:, D_MODEL:2 * D_MODEL])


def _glu(h4, y4, gain, d_skip, w_glu):
    batch, n_chunks, chunk, _ = h4.shape
    tok = pl.BlockSpec((batch, None, chunk, D_MODEL), lambda c: (0, c, 0, 0))
    return pl.pallas_call(
        _glu_kernel,
        grid=(n_chunks,),
        in_specs=[
            tok,
            pl.BlockSpec((N_GROUPS, SSM_GROUP, batch, chunk), lambda c: (0, 0, c, 0)),
            _resident((1, D_MODEL)),
            _resident((1, D_MODEL)),
            _resident((D_MODEL, 2 * D_MODEL)),
        ],
        out_specs=tok,
        out_shape=jax.ShapeDtypeStruct(h4.shape, F32),
        compiler_params=_params("parallel"),
        name="ssm_gelu_glu",
    )(h4, y4, gain, d_skip, w_glu)


def _ssm_operands(a_re, a_im, log_dt, b_re, b_im, c_re, c_im, chunk):
    hp = lax.Precision.HIGH
    dt = jnp.exp(log_dt)[..., None]
    zr, zi = a_re * dt, a_im * dt
    t = jnp.arange(chunk + 1, dtype=F32)[None, None, :, None]
    mag = jnp.exp(zr[:, :, None, :] * t)
    ang = zi[:, :, None, :] * t
    pr, pi = mag * jnp.cos(ang), mag * jnp.sin(ang)
    nr, ni = pr[:, :, 1, :] - 1.0, pi[:, :, 1, :]
    den = a_re * a_re + a_im * a_im
    qr = (nr * a_re + ni * a_im) / den
    qi = (ni * a_re - nr * a_im) / den
    bbr = qr[..., None] * b_re - qi[..., None] * b_im
    bbi = qr[..., None] * b_im + qi[..., None] * b_re

    car = c_re[:, :, None] * pr[:, :, :chunk, None, :] - c_im[:, :, None] * pi[:, :, :chunk, None, :]
    cai = c_re[:, :, None] * pi[:, :, :chunk, None, :] + c_im[:, :, None] * pr[:, :, :chunk, None, :]
    kern = jnp.einsum('dgtop,dgpi->dgtoi', jnp.concatenate([car, -cai], axis=-1),
                      jnp.concatenate([bbr, bbi], axis=2), precision=hp)
    center = (kern[0, :, 0] + kern[1, :, 0])[:, None]
    lagged = jnp.concatenate([jnp.zeros_like(center), kern[1, :, :0:-1], center, kern[0, :, 1:]],
                             axis=1)
    e_rows = lagged.transpose(0, 3, 2, 1)

    def fwd_bwd(x):
        return jnp.concatenate([x[0], x[1]], axis=-1)

    def re_im(xr, xi):
        return jnp.stack([fwd_bwd(xr), fwd_bwd(xi)], axis=1)

    pow_in = re_im(jnp.stack([pr[0, :, chunk - 1::-1], pr[1, :, :chunk]]),
                   jnp.stack([pi[0, :, chunk - 1::-1], pi[1, :, :chunk]]))
    b_bar = re_im(bbr.transpose(0, 1, 3, 2), bbi.transpose(0, 1, 3, 2))
    pow_out = re_im(jnp.stack([pr[0, :, 1:], pr[1, :, :0:-1]]),
                    jnp.stack([pi[0, :, 1:], pi[1, :, :0:-1]]))
    c_mat = re_im(c_re, c_im)
    a_chunk = jnp.stack([jnp.concatenate([pr[0, :, chunk], pr[1, :, chunk]], axis=-1),
                         jnp.concatenate([pi[0, :, chunk], pi[1, :, chunk]], axis=-1)], axis=1)
    return e_rows, pow_in, b_bar, pow_out, c_mat, a_chunk


def _ssm_layer(h2, batch, seq, gain, a_re, a_im, log_dt, b_re, b_im, c_re, c_im, d_skip, w_glu):
    chunk = SSM_CHUNK
    n_chunks = seq // chunk
    operands = _ssm_operands(a_re, a_im, log_dt, b_re, b_im, c_re, c_im, chunk)
    h4 = h2.reshape(batch, n_chunks, chunk, D_MODEL)
    ug = _norm_group(h4, gain)
    y4 = _ssm(ug, *operands)
    return _glu(h4, y4, gain, d_skip, w_glu).reshape(batch * seq, D_MODEL)


def _rope_tables(seq):
    pos = jnp.arange(seq, dtype=F32)
    inv_freq = ROPE_THETA ** (-jnp.arange(0, HEAD_DIM, 2, dtype=F32) / HEAD_DIM)
    ang = pos[:, None] * inv_freq[None, :]
    cos, sin = jnp.cos(ang), jnp.sin(ang)
    reps = LANES // HEAD_DIM
    cos_t = jnp.tile(jnp.concatenate([cos, cos], axis=-1), (1, reps))
    sin_t = jnp.tile(jnp.concatenate([-sin, sin], axis=-1), (1, reps))
    return cos_t, sin_t


def _lambda_init(layer_idx):
    return 0.8 - 0.6 * math.exp(-0.3 * layer_idx)


def kernel(x, norm_mix, norm_ffn, attn_w_qkv, attn_q_gain, attn_k_gain, attn_lambda,
           attn_subln, attn_w_o, ssm_a_re, ssm_a_im, ssm_log_dt, ssm_b_re, ssm_b_im,
           ssm_c_re, ssm_c_im, ssm_d, ssm_w_glu, ffn_w_up, ffn_w_down):
    batch, seq, d = x.shape
    assert d == D_MODEL
    depth = norm_mix.shape[0]
    cos_t, sin_t = _rope_tables(seq)
    reps = LANES // HEAD_DIM
    h2 = x.reshape(batch * seq, D_MODEL)
    for i in range(depth):
        j = i // N_MIXERS
        gain = norm_mix[i][None, :]
        if i % N_MIXERS == 0:
            qg, kg = attn_q_gain[j], attn_k_gain[j]
            w_qk = attn_w_qkv[j][:, :2 * D_MODEL].astype(BF16)
            w_v_t = attn_w_qkv[j][:, 2 * D_MODEL:].T.astype(BF16)
            q, k, vt = _qkv(h2, gain, w_qk, w_v_t, jnp.tile(qg, reps)[None, :],
                            jnp.tile(kg, reps)[None, :], cos_t, sin_t, seq)
            o = _attention(q.reshape(batch, seq, D_MODEL), k.reshape(batch, seq, D_MODEL), vt, qg, kg,
                           attn_lambda[j], attn_subln[j][:, None], _lambda_init(i))
            proj = dict(attn_out=o.reshape(batch * seq, D_MODEL), w_o=attn_w_o[j].astype(BF16))
        else:
            h2 = _ssm_layer(h2, batch, seq, gain, ssm_a_re[j], ssm_a_im[j], ssm_log_dt[j],
                            ssm_b_re[j], ssm_b_im[j], ssm_c_re[j], ssm_c_im[j],
                            ssm_d[j][None, :], ssm_w_glu[j].astype(BF16))
            proj = {}
        h2 = _mlp(h2, norm_ffn[i][None, :], ffn_w_up[i].astype(BF16), ffn_w_down[i].astype(BF16),
                  **proj)
    return h2.reshape(batch, seq, D_MODEL)
```

```python
import functools
import math

import jax
import jax.numpy as jnp
from jax import lax
from jax.experimental import pallas as pl
from jax.experimental.pallas import tpu as pltpu

F32 = jnp.float32
BF16 = jnp.bfloat16

D_MODEL = 1024
N_HEADS = 8
HEAD_DIM = 64
V_HEAD_DIM = 2 * HEAD_DIM
V_ROWS = V_HEAD_DIM + 16
ROPE_THETA = 10000.0
SSM_GROUP = 16
N_GROUPS = D_MODEL // SSM_GROUP
SSM_STATE = 64
D_FF = 4 * D_MODEL
RMS_EPS = 1e-6
Q_SCALE = HEAD_DIM ** -0.5 * math.log2(math.e)
SAFE_LOG2_SCORE = 80.0
N_MIXERS = 2

LANES = 128
MXU_DIM = 256
SSM_CHUNK = LANES
VMEM_LIMIT = 48 * 1024 * 1024


def _params(*sem):
    return pltpu.CompilerParams(dimension_semantics=sem, vmem_limit_bytes=VMEM_LIMIT)


def _rms(x, gain):
    ms = jnp.mean(x * x, axis=-1, keepdims=True)
    return x * lax.rsqrt(ms + RMS_EPS) * gain


def _dot(a, b):
    return jnp.dot(a, b, preferred_element_type=F32)


def _qkv_kernel(x_ref, g_ref, wqk_ref, wvt_ref, qg_ref, kg_ref, cos_ref, sin_ref,
                q_ref, k_ref, vt_ref):
    tm = x_ref.shape[0]
    xn = _rms(x_ref[...], g_ref[...]).astype(BF16)
    row = lax.broadcasted_iota(jnp.int32, (MXU_DIM, MXU_DIM), 0)
    col = lax.broadcasted_iota(jnp.int32, (MXU_DIM, MXU_DIM), 1)
    seg_mean = jnp.where(row // HEAD_DIM == col // HEAD_DIM, 1.0 / HEAD_DIM, 0.0).astype(BF16)
    lane = lax.broadcasted_iota(jnp.int32, (tm, LANES), 1)
    first_half = (lane % HEAD_DIM) < (HEAD_DIM // 2)
    cos = cos_ref[...]
    sin = sin_ref[...]

    def norm_rope(y, gain, scale, out_ref):
        sq = (y * y).astype(BF16)
        for t in range(D_MODEL // MXU_DIM):
            ms = _dot(sq[:, t * MXU_DIM:(t + 1) * MXU_DIM], seg_mean)
            for s in range(MXU_DIM // LANES):
                sl = slice(t * MXU_DIM + s * LANES, t * MXU_DIM + (s + 1) * LANES)
                yn = y[:, sl] * lax.rsqrt(ms[:, s * LANES:(s + 1) * LANES] + RMS_EPS) * gain
                partner = jnp.where(first_half,
                                    pltpu.roll(yn, LANES - HEAD_DIM // 2, 1),
                                    pltpu.roll(yn, HEAD_DIM // 2, 1))
                out_ref[:, sl] = ((yn * cos + partner * sin) * scale).astype(BF16)

    norm_rope(_dot(xn, wqk_ref[:, 0:D_MODEL]), qg_ref[...], Q_SCALE, q_ref)
    norm_rope(_dot(xn, wqk_ref[:, D_MODEL:2 * D_MODEL]), kg_ref[...], 1.0, k_ref)
    yvt = lax.dot_general(wvt_ref[...], xn, (((1,), (1,)), ((), ())),
                          preferred_element_type=F32)
    pad_rows = lax.broadcasted_iota(jnp.int32, (V_ROWS - V_HEAD_DIM, tm), 0)
    ones_row = jnp.where(pad_rows == 0, 1.0, 0.0).astype(BF16)
    for t in range(N_HEADS):
        vt_ref[t, 0:V_HEAD_DIM, :] = yvt[t * V_HEAD_DIM:(t + 1) * V_HEAD_DIM, :].astype(BF16)
        vt_ref[t, V_HEAD_DIM:V_ROWS, :] = ones_row


def _qkv(h2, gain, w_qk, w_v_t, q_gain, k_gain, cos_t, sin_t, seq, tm=512):
    tokens = h2.shape[0]
    tm = min(tm, seq)
    assert tokens % tm == 0 and seq % tm == 0
    n_pos_blocks = seq // tm
    tok = lambda i: (i, 0)
    full = lambda i: (0, 0)
    pos = lambda i: (i % n_pos_blocks, 0)
    vt_map = lambda i: (i // n_pos_blocks, 0, i % n_pos_blocks, 0, 0)
    out = jax.ShapeDtypeStruct((tokens, D_MODEL), BF16)
    vt_shape = (tokens // seq, N_HEADS, n_pos_blocks, V_ROWS, tm)
    return pl.pallas_call(
        _qkv_kernel,
        grid=(tokens // tm,),
        in_specs=[
            pl.BlockSpec((tm, D_MODEL), tok),
            pl.BlockSpec((1, D_MODEL), full),
            pl.BlockSpec((D_MODEL, 2 * D_MODEL), full),
            pl.BlockSpec((D_MODEL, D_MODEL), full),
            pl.BlockSpec((1, LANES), full),
            pl.BlockSpec((1, LANES), full),
            pl.BlockSpec((tm, LANES), pos),
            pl.BlockSpec((tm, LANES), pos),
        ],
        out_specs=[pl.BlockSpec((tm, D_MODEL), tok), pl.BlockSpec((tm, D_MODEL), tok),
                   pl.BlockSpec((None, N_HEADS, None, V_ROWS, tm), vt_map)],
        out_shape=[out, out, jax.ShapeDtypeStruct(vt_shape, BF16)],
        compiler_params=_params("parallel"),
        name="qkv_norm_rope",
    )(h2, gain, w_qk, w_v_t, q_gain, k_gain, cos_t, sin_t)


def _stack_q(q_ref, qs_ref):
    tq = q_ref.shape[0]
    q = q_ref[...]
    lane = lax.broadcasted_iota(jnp.int32, q.shape, 1)
    zero = jnp.zeros_like(q)
    qs_ref[0:tq, :] = jnp.where(lane < HEAD_DIM, q, zero)
    qs_ref[tq:2 * tq, :] = jnp.where(lane >= HEAD_DIM, q, zero)


def _attn_kernel(q_ref, k_ref, vt_ref, lam_ref, sg_ref, o_ref, qs_ref, acc_ref, *m_scratch,
                 lam_init, subtract_max):
    tq = q_ref.shape[0]
    n_chunks, _, tk = vt_ref.shape
    _stack_q(q_ref, qs_ref)
    acc_ref[...] = jnp.zeros(acc_ref.shape, F32)
    if subtract_max:
        m_ref, = m_scratch
        m_ref[...] = jnp.full(m_ref.shape, -jnp.inf, F32)

    def chunk(c, carry):
        off = pl.multiple_of(c * tk, tk)
        st = lax.dot_general(k_ref[pl.ds(off, tk), :], qs_ref[...], (((1,), (1,)), ((), ())),
                             preferred_element_type=F32)
        if subtract_max:
            m_prev = m_ref[...]
            m_new = jnp.maximum(m_prev, jnp.max(st, axis=0, keepdims=True))
            acc_ref[...] = acc_ref[...] * jnp.exp2(m_prev - m_new)
            m_ref[...] = m_new
            st = st - m_new
        acc_ref[...] += _dot(vt_ref[c], jnp.exp2(st).astype(BF16))
        return carry

    lax.fori_loop(0, n_chunks, chunk, 0, unroll=1 if subtract_max else 4)

    ot = acc_ref[0:V_HEAD_DIM, :] * (1.0 / acc_ref[V_HEAD_DIM:V_HEAD_DIM + 1, :])
    lv = lam_ref[...]
    lam = (jnp.exp(jnp.sum(lv[0:1] * lv[1:2], axis=1, keepdims=True))
           - jnp.exp(jnp.sum(lv[2:3] * lv[3:4], axis=1, keepdims=True)) + lam_init)
    od = ot[:, 0:tq] - lam * ot[:, tq:2 * tq]
    ms = jnp.mean(od * od, axis=0, keepdims=True)
    on = od * lax.rsqrt(ms + RMS_EPS) * sg_ref[...] * (1.0 - lam_init)
    o_ref[...] = on.T.astype(BF16)


def _attention_call(q, k, vt, lam_vecs, subln_col, lam_init, subtract_max, tq=1024):
    b, s, _ = q.shape
    tq = min(tq, s)
    assert s % tq == 0
    n_chunks, tk = vt.shape[2], vt.shape[4]
    qmap = lambda bi, hi, qi: (bi, qi, hi)
    full = lambda bi, hi, qi: (0, 0)
    scratch = [pltpu.VMEM((2 * tq, LANES), BF16), pltpu.VMEM((V_ROWS, 2 * tq), F32)]
    if subtract_max:
        scratch.append(pltpu.VMEM((1, 2 * tq), F32))
    return pl.pallas_call(
        functools.partial(_attn_kernel, lam_init=lam_init, subtract_max=subtract_max),
        grid=(b, N_HEADS, s // tq),
        in_specs=[
            pl.BlockSpec((None, tq, LANES), qmap),
            pl.BlockSpec((None, s, LANES), lambda bi, hi, qi: (bi, 0, hi)),
            pl.BlockSpec((None, None, n_chunks, V_ROWS, tk), lambda bi, hi, qi: (bi, hi, 0, 0, 0)),
            pl.BlockSpec((4, HEAD_DIM), full),
            pl.BlockSpec((V_HEAD_DIM, 1), full),
        ],
        out_specs=pl.BlockSpec((None, tq, LANES), qmap),
        out_shape=jax.ShapeDtypeStruct((b, s, D_MODEL), BF16),
        scratch_shapes=scratch,
        compiler_params=_params("parallel", "parallel", "parallel"),
        name="diff_attention_online" if subtract_max else "diff_attention_bounded",
    )(q, k, vt, lam_vecs, subln_col)


def _attention(q, k, vt, q_gain, k_gain, lam_vecs, subln_col, lam_init):
    score_bound = (HEAD_DIM * Q_SCALE) * jnp.max(jnp.abs(q_gain)) * jnp.max(jnp.abs(k_gain))
    return lax.cond(score_bound < SAFE_LOG2_SCORE,
                    lambda *a: _attention_call(*a, lam_init, False),
                    lambda *a: _attention_call(*a, lam_init, True),
                    q, k, vt, lam_vecs, subln_col)


MLP_FF_TILE = 1024


def _mlp_body(h, g_ref, wu_ref, wd_ref, out_ref):
    xn = _rms(h, g_ref[...]).astype(BF16)
    acc = h
    for f in range(D_FF // MLP_FF_TILE):
        sl = slice(f * MLP_FF_TILE, (f + 1) * MLP_FF_TILE)
        a = jnp.maximum(_dot(xn, wu_ref[:, sl]), 0.0)
        acc = acc + _dot((a * a).astype(BF16), wd_ref[sl, :])
    out_ref[...] = acc


def _mlp_kernel(h_ref, g_ref, wu_ref, wd_ref, out_ref):
    _mlp_body(h_ref[...], g_ref, wu_ref, wd_ref, out_ref)


def _proj_mlp_kernel(h_ref, o_ref, wo_ref, g_ref, wu_ref, wd_ref, out_ref):
    _mlp_body(h_ref[...] + _dot(o_ref[...], wo_ref[...]), g_ref, wu_ref, wd_ref, out_ref)


def _resident(shape):
    return pl.BlockSpec(shape, lambda i: (0,) * len(shape), pipeline_mode=pl.Buffered(1))


def _mlp(h2, gain, w_up, w_down, attn_out=None, w_o=None, tm=512):
    tokens = h2.shape[0]
    tok = pl.BlockSpec((tm, D_MODEL), lambda i: (i, 0))
    mlp_specs = [_resident((1, D_MODEL)), _resident((D_MODEL, D_FF)), _resident((D_FF, D_MODEL))]
    if attn_out is None:
        body, name = _mlp_kernel, "relu2_mlp"
        in_specs, args = [tok] + mlp_specs, (h2, gain, w_up, w_down)
    else:
        body, name = _proj_mlp_kernel, "attn_proj_relu2_mlp"
        in_specs = [tok, tok, _resident((D_MODEL, D_MODEL))] + mlp_specs
        args = (h2, attn_out, w_o, gain, w_up, w_down)
    return pl.pallas_call(
        body,
        grid=(tokens // tm,),
        in_specs=in_specs,
        out_specs=tok,
        out_shape=jax.ShapeDtypeStruct((tokens, D_MODEL), F32),
        compiler_params=_params("parallel"),
        name=name,
    )(*args)


def _norm_group_kernel(h_ref, g_ref, u_ref):
    chunk = h_ref.shape[1]
    for b in range(h_ref.shape[0]):
        ut = _rms(h_ref[b], g_ref[...]).T
        u_ref[:, b] = ut.reshape(N_GROUPS, SSM_GROUP, chunk)


def _norm_group(h4, gain):
    batch, n_chunks, chunk, _ = h4.shape
    return pl.pallas_call(
        _norm_group_kernel,
        grid=(n_chunks,),
        in_specs=[pl.BlockSpec((batch, None, chunk, D_MODEL), lambda c: (0, c, 0, 0)),
                  pl.BlockSpec((1, D_MODEL), lambda c: (0, 0))],
        out_specs=pl.BlockSpec((N_GROUPS, None, batch, SSM_GROUP, chunk), lambda c: (0, c, 0, 0, 0)),
        out_shape=jax.ShapeDtypeStruct((N_GROUPS, n_chunks, batch, SSM_GROUP, chunk), F32),
        compiler_params=_params("parallel"),
        name="ssm_pre_norm",
    )(h4, gain)


def _ssm_kernel(u_ref, e_ref, e_next_ref, pin_ref, bb_ref, pout_ref, cc_ref, al_ref, y_ref,
                t_even_ref, t_odd_ref, zs_ref, sfr_ref, sbr_ref, sfi_ref, sbi_ref):
    n_chunks, batch, _, chunk = u_ref.shape
    rows, width = n_chunks * batch, SSM_GROUP * chunk
    p = SSM_STATE
    assert chunk == LANES

    upper = (lax.broadcasted_iota(jnp.int32, (chunk, chunk), 1)
             >= lax.broadcasted_iota(jnp.int32, (chunk, chunk), 0))

    def build_t(lag_ref, t_ref):
        for h in range(SSM_GROUP):
            for ho in range(SSM_GROUP):
                k_bwd = jnp.broadcast_to(lag_ref[h, ho:ho + 1, 0:chunk], (chunk, chunk))
                k_fwd = jnp.broadcast_to(lag_ref[h, ho:ho + 1, chunk:2 * chunk], (chunk, chunk))
                t_fwd = pltpu.roll(k_fwd, 0, 1, stride=1, stride_axis=0)
                t_bwd = pltpu.roll(k_bwd, 0, 1, stride=1, stride_axis=0)
                t_ref[chunk * h:chunk * (h + 1), chunk * ho:chunk * (ho + 1)] = (
                    jnp.where(upper, t_fwd, t_bwd).astype(BF16))

    group = pl.program_id(0)

    @pl.when(group == 0)
    def _():
        build_t(e_ref, t_even_ref)

    def outer(pow_ref, vec_ref):
        ar_, ai_ = pow_ref[0][None], pow_ref[1][None]
        vr_, vi_ = vec_ref[0][:, None, :], vec_ref[1][:, None, :]
        return ((ar_ * vr_ - ai_ * vi_).reshape(width, 2 * p),
                (ar_ * vi_ + ai_ * vr_).reshape(width, 2 * p))

    wi_re, wi_im = outer(pin_ref, bb_ref)
    w_in = jnp.concatenate([wi_re, wi_im], axis=1).astype(BF16)
    wo_re, wo_im = outer(pout_ref, cc_ref)
    w_out_t = jnp.concatenate([wo_re, -wo_im], axis=1).astype(BF16)

    u = jnp.concatenate([u_ref[:, :, h, :].reshape(rows, chunk) for h in range(SSM_GROUP)],
                        axis=1).astype(BF16)
    zs_ref[...] = _dot(u, w_in)

    ar = al_ref[0:1, :]
    ai = al_ref[1:2, :]
    is_fwd = lax.broadcasted_iota(jnp.int32, (batch, 2 * p), 1) < p

    def step(kk, carry):
        xr, xi = carry
        rf = pl.multiple_of(kk * batch, batch)
        rb = pl.multiple_of((n_chunks - 1 - kk) * batch, batch)
        lr = jnp.where(is_fwd, zs_ref[pl.ds(rf, batch), 0:2 * p], zs_ref[pl.ds(rb, batch), 0:2 * p])
        li = jnp.where(is_fwd, zs_ref[pl.ds(rf, batch), 2 * p:4 * p],
                       zs_ref[pl.ds(rb, batch), 2 * p:4 * p])
        sfr_ref[pl.ds(rf, batch), :] = xr
        sbr_ref[pl.ds(rb, batch), :] = xr
        sfi_ref[pl.ds(rf, batch), :] = xi
        sbi_ref[pl.ds(rb, batch), :] = xi
        return ar * xr - ai * xi + lr, ar * xi + ai * xr + li

    zero = jnp.zeros((batch, 2 * p), F32)
    lax.fori_loop(0, n_chunks, step, (zero, zero))

    fwd_lane = lax.broadcasted_iota(jnp.int32, (rows, 2 * p), 1) < p
    xin = jnp.concatenate(
        [jnp.where(fwd_lane, sfr_ref[...], sbr_ref[...]),
         jnp.where(fwd_lane, sfi_ref[...], sbi_ref[...])], axis=1).astype(BF16)
    carry_out = lax.dot_general(xin, w_out_t, (((1,), (1,)), ((), ())), preferred_element_type=F32)

    def finish(t_ref, t_next_ref):
        build_t(e_next_ref, t_next_ref)
        y = _dot(u, t_ref[...]) + carry_out
        for h in range(SSM_GROUP):
            y_ref[h] = y[:, chunk * h:chunk * (h + 1)]

    @pl.when(group % 2 == 0)
    def _():
        finish(t_even_ref, t_odd_ref)

    @pl.when(group % 2 == 1)
    def _():
        finish(t_odd_ref, t_even_ref)


def _ssm(ug, e_rows, pow_in, b_bar, pow_out, c_mat, a_chunk):
    g, n_chunks, batch, _, chunk = ug.shape
    rows, width = n_chunks * batch, SSM_GROUP * chunk
    p = SSM_STATE
    grp = lambda i: (i, 0, 0)
    grp4 = lambda i: (i, 0, 0, 0)
    return pl.pallas_call(
        _ssm_kernel,
        grid=(g,),
        in_specs=[
            pl.BlockSpec((None, n_chunks, batch, SSM_GROUP, chunk), lambda i: (i, 0, 0, 0, 0)),
            pl.BlockSpec((None, SSM_GROUP, SSM_GROUP, 2 * chunk), grp4),
            pl.BlockSpec((None, SSM_GROUP, SSM_GROUP, 2 * chunk),
                         lambda i: (jnp.minimum(i + 1, g - 1), 0, 0, 0)),
            pl.BlockSpec((None, 2, chunk, 2 * p), grp4),
            pl.BlockSpec((None, 2, SSM_GROUP, 2 * p), grp4),
            pl.BlockSpec((None, 2, chunk, 2 * p), grp4),
            pl.BlockSpec((None, 2, SSM_GROUP, 2 * p), grp4),
            pl.BlockSpec((None, 2, 2 * p), grp),
        ],
        out_specs=pl.BlockSpec((None, SSM_GROUP, rows, chunk), grp4),
        out_shape=jax.ShapeDtypeStruct((g, SSM_GROUP, rows, chunk), F32),
        scratch_shapes=[
            pltpu.VMEM((width, width), BF16),
            pltpu.VMEM((width, width), BF16),
            pltpu.VMEM((rows, 4 * p), F32),
            pltpu.VMEM((rows, 2 * p), F32),
            pltpu.VMEM((rows, 2 * p), F32),
            pltpu.VMEM((rows, 2 * p), F32),
            pltpu.VMEM((rows, 2 * p), F32),
        ],
        compiler_params=_params("arbitrary"),
        name="s5_chunked_scan",
    )(ug, e_rows, e_rows, pow_in, b_bar, pow_out, c_mat, a_chunk)


def _glu_kernel(h_ref, y_ref, g_ref, d_ref, w_ref, out_ref):
    batch, chunk, _ = h_ref.shape
    for b0 in range(0, batch, 2):
        acts = []
        for b in (b0, b0 + 1):
            u = _rms(h_ref[b], g_ref[...])
            y = y_ref[:, :, b, :].reshape(D_MODEL, chunk).T
            acts.append(jax.nn.gelu(y + d_ref[...] * u).astype(BF16))
        z = _dot(jnp.concatenate(acts, axis=0), w_ref[...])
        for i, b in enumerate((b0, b0 + 1)):
            zb = z[i * chunk:(i + 1) * chunk]
            out_ref[b] = h_ref[b] + zb[:, 0:D_MODEL] * jax.nn.sigmoid(zb[:, D_MODEL:2 * D_MODEL])


def _glu(h4, y4, gain, d_skip, w_glu):
    batch, n_chunks, chunk, _ = h4.shape
    assert batch % 2 == 0
    tok = pl.BlockSpec((batch, None, chunk, D_MODEL), lambda c: (0, c, 0, 0))
    return pl.pallas_call(
        _glu_kernel,
        grid=(n_chunks,),
        in_specs=[
            tok,
            pl.BlockSpec((N_GROUPS, SSM_GROUP, batch, chunk), lambda c: (0, 0, c, 0)),
            _resident((1, D_MODEL)),
            _resident((1, D_MODEL)),
            _resident((D_MODEL, 2 * D_MODEL)),
        ],
        out_specs=tok,
        out_shape=jax.ShapeDtypeStruct(h4.shape, F32),
        compiler_params=_params("parallel"),
        name="ssm_gelu_glu",
    )(h4, y4, gain, d_skip, w_glu)


def _ssm_operands(a_re, a_im, log_dt, b_re, b_im, c_re, c_im, chunk):
    hp = lax.Precision.HIGH
    dt = jnp.exp(log_dt)[..., None]
    zr, zi = a_re * dt, a_im * dt
    t = jnp.arange(chunk + 1, dtype=F32)[None, None, :, None]
    mag = jnp.exp(zr[:, :, None, :] * t)
    ang = zi[:, :, None, :] * t
    pr, pi = mag * jnp.cos(ang), mag * jnp.sin(ang)
    nr, ni = pr[:, :, 1, :] - 1.0, pi[:, :, 1, :]
    den = a_re * a_re + a_im * a_im
    qr = (nr * a_re + ni * a_im) / den
    qi = (ni * a_re - nr * a_im) / den
    bbr = qr[..., None] * b_re - qi[..., None] * b_im
    bbi = qr[..., None] * b_im + qi[..., None] * b_re

    cp_re, cp_im = c_re.transpose(0, 1, 3, 2)[..., None], c_im.transpose(0, 1, 3, 2)[..., None]
    w_re = cp_re * bbr[:, :, :, None, :] - cp_im * bbi[:, :, :, None, :]
    w_im = cp_re * bbi[:, :, :, None, :] + cp_im * bbr[:, :, :, None, :]
    n_pairs = SSM_GROUP * SSM_GROUP
    g = a_re.shape[1]
    kern = jnp.einsum('dgtp,dgpn->dgtn',
                      jnp.concatenate([pr[:, :, :chunk], -pi[:, :, :chunk]], axis=-1),
                      jnp.concatenate([w_re, w_im], axis=2).reshape(2, g, 2 * SSM_STATE, n_pairs),
                      precision=hp).reshape(2, g, chunk, SSM_GROUP, SSM_GROUP)
    center = (kern[0, :, 0] + kern[1, :, 0])[:, None]
    lagged = jnp.concatenate([jnp.zeros_like(center), kern[1, :, :0:-1], center, kern[0, :, 1:]],
                             axis=1)
    e_rows = lagged.transpose(0, 3, 2, 1)

    def fwd_bwd(x):
        return jnp.concatenate([x[0], x[1]], axis=-1)

    def re_im(xr, xi):
        return jnp.stack([fwd_bwd(xr), fwd_bwd(xi)], axis=1)

    pow_in = re_im(jnp.stack([pr[0, :, chunk - 1::-1], pr[1, :, :chunk]]),
                   jnp.stack([pi[0, :, chunk - 1::-1], pi[1, :, :chunk]]))
    b_bar = re_im(bbr.transpose(0, 1, 3, 2), bbi.transpose(0, 1, 3, 2))
    pow_out = re_im(jnp.stack([pr[0, :, 1:], pr[1, :, :0:-1]]),
                    jnp.stack([pi[0, :, 1:], pi[1, :, :0:-1]]))
    c_mat = re_im(c_re, c_im)
    a_chunk = jnp.stack([jnp.concatenate([pr[0, :, chunk], pr[1, :, chunk]], axis=-1),
                         jnp.concatenate([pi[0, :, chunk], pi[1, :, chunk]], axis=-1)], axis=1)
    return e_rows, pow_in, b_bar, pow_out, c_mat, a_chunk


def _ssm_layer(h2, batch, seq, gain, a_re, a_im, log_dt, b_re, b_im, c_re, c_im, d_skip, w_glu):
    chunk = SSM_CHUNK
    n_chunks = seq // chunk
    operands = _ssm_operands(a_re, a_im, log_dt, b_re, b_im, c_re, c_im, chunk)
    h4 = h2.reshape(batch, n_chunks, chunk, D_MODEL)
    ug = _norm_group(h4, gain)
    y4 = _ssm(ug, *operands)
    return _glu(h4, y4, gain, d_skip, w_glu).reshape(batch * seq, D_MODEL)


def _rope_tables(seq):
    pos = jnp.arange(seq, dtype=F32)
    inv_freq = ROPE_THETA ** (-jnp.arange(0, HEAD_DIM, 2, dtype=F32) / HEAD_DIM)
    ang = pos[:, None] * inv_freq[None, :]
    cos, sin = jnp.cos(ang), jnp.sin(ang)
    reps = LANES // HEAD_DIM
    cos_t = jnp.tile(jnp.concatenate([cos, cos], axis=-1), (1, reps))
    sin_t = jnp.tile(jnp.concatenate([-sin, sin], axis=-1), (1, reps))
    return cos_t, sin_t


def _lambda_init(layer_idx):
    return 0.8 - 0.6 * math.exp(-0.3 * layer_idx)


def kernel(x, norm_mix, norm_ffn, attn_w_qkv, attn_q_gain, attn_k_gain, attn_lambda,
           attn_subln, attn_w_o, ssm_a_re, ssm_a_im, ssm_log_dt, ssm_b_re, ssm_b_im,
           ssm_c_re, ssm_c_im, ssm_d, ssm_w_glu, ffn_w_up, ffn_w_down):
    batch, seq, d = x.shape
    assert d == D_MODEL
    depth = norm_mix.shape[0]
    cos_t, sin_t = _rope_tables(seq)
    reps = LANES // HEAD_DIM
    h2 = x.reshape(batch * seq, D_MODEL)
    for i in range(depth):
        j = i // N_MIXERS
        gain = norm_mix[i][None, :]
        if i % N_MIXERS == 0:
            qg, kg = attn_q_gain[j], attn_k_gain[j]
            w_qk = attn_w_qkv[j][:, :2 * D_MODEL].astype(BF16)
            w_v_t = attn_w_qkv[j][:, 2 * D_MODEL:].T.astype(BF16)
            q, k, vt = _qkv(h2, gain, w_qk, w_v_t, jnp.tile(qg, reps)[None, :],
                            jnp.tile(kg, reps)[None, :], cos_t, sin_t, seq)
            o = _attention(q.reshape(batch, seq, D_MODEL), k.reshape(batch, seq, D_MODEL), vt, qg, kg,
                           attn_lambda[j], attn_subln[j][:, None], _lambda_init(i))
            proj = dict(attn_out=o.reshape(batch * seq, D_MODEL), w_o=attn_w_o[j].astype(BF16))
        else:
            h2 = _ssm_layer(h2, batch, seq, gain, ssm_a_re[j], ssm_a_im[j], ssm_log_dt[j],
                            ssm_b_re[j], ssm_b_im[j], ssm_c_re[j], ssm_c_im[j],
                            ssm_d[j][None, :], ssm_w_glu[j].astype(BF16))
            proj = {}
        h2 = _mlp(h2, norm_ffn[i][None, :], ffn_w_up[i].astype(BF16), ffn_w_down[i].astype(BF16),
                  **proj)
    return h2.reshape(batch, seq, D_MODEL)
```

```python
import functools
import math

import jax
import jax.numpy as jnp
from jax import lax
from jax.experimental import pallas as pl
from jax.experimental.pallas import tpu as pltpu

F32 = jnp.float32
BF16 = jnp.bfloat16

D_MODEL = 1024
N_HEADS = 8
HEAD_DIM = 64
V_HEAD_DIM = 2 * HEAD_DIM
V_ROWS = V_HEAD_DIM + 16
ROPE_THETA = 10000.0
SSM_GROUP = 16
N_GROUPS = D_MODEL // SSM_GROUP
SSM_STATE = 64
D_FF = 4 * D_MODEL
RMS_EPS = 1e-6
Q_SCALE = HEAD_DIM ** -0.5 * math.log2(math.e)
SAFE_LOG2_SCORE = 80.0
N_MIXERS = 2

LANES = 128
MXU_DIM = 256
SSM_CHUNK = LANES
VMEM_LIMIT = 48 * 1024 * 1024


def _params(*sem):
    return pltpu.CompilerParams(dimension_semantics=sem, vmem_limit_bytes=VMEM_LIMIT)


def _rms(x, gain):
    ms = jnp.mean(x * x, axis=-1, keepdims=True)
    return x * lax.rsqrt(ms + RMS_EPS) * gain


def _dot(a, b):
    return jnp.dot(a, b, preferred_element_type=F32)


def _qkv_kernel(x_ref, g_ref, wqk_ref, wvt_ref, qg_ref, kg_ref, cos_ref, sin_ref,
                q_ref, k_ref, vt_ref):
    tm = x_ref.shape[0]
    xn = _rms(x_ref[...], g_ref[...]).astype(BF16)
    row = lax.broadcasted_iota(jnp.int32, (MXU_DIM, MXU_DIM), 0)
    col = lax.broadcasted_iota(jnp.int32, (MXU_DIM, MXU_DIM), 1)
    seg_mean = jnp.where(row // HEAD_DIM == col // HEAD_DIM, 1.0 / HEAD_DIM, 0.0).astype(BF16)
    lane = lax.broadcasted_iota(jnp.int32, (tm, LANES), 1)
    first_half = (lane % HEAD_DIM) < (HEAD_DIM // 2)
    cos = cos_ref[...]
    sin = sin_ref[...]

    def norm_rope(y, gain, scale, out_ref):
        sq = (y * y).astype(BF16)
        for t in range(D_MODEL // MXU_DIM):
            ms = _dot(sq[:, t * MXU_DIM:(t + 1) * MXU_DIM], seg_mean)
            for s in range(MXU_DIM // LANES):
                sl = slice(t * MXU_DIM + s * LANES, t * MXU_DIM + (s + 1) * LANES)
                yn = y[:, sl] * lax.rsqrt(ms[:, s * LANES:(s + 1) * LANES] + RMS_EPS) * gain
                partner = jnp.where(first_half,
                                    pltpu.roll(yn, LANES - HEAD_DIM // 2, 1),
                                    pltpu.roll(yn, HEAD_DIM // 2, 1))
                out_ref[:, sl] = ((yn * cos + partner * sin) * scale).astype(BF16)

    norm_rope(_dot(xn, wqk_ref[:, 0:D_MODEL]), qg_ref[...], Q_SCALE, q_ref)
    norm_rope(_dot(xn, wqk_ref[:, D_MODEL:2 * D_MODEL]), kg_ref[...], 1.0, k_ref)
    yvt = lax.dot_general(wvt_ref[...], xn, (((1,), (1,)), ((), ())),
                          preferred_element_type=F32)
    pad_rows = lax.broadcasted_iota(jnp.int32, (V_ROWS - V_HEAD_DIM, tm), 0)
    ones_row = jnp.where(pad_rows == 0, 1.0, 0.0).astype(BF16)
    for t in range(N_HEADS):
        vt_ref[t, 0:V_HEAD_DIM, :] = yvt[t * V_HEAD_DIM:(t + 1) * V_HEAD_DIM, :].astype(BF16)
        vt_ref[t, V_HEAD_DIM:V_ROWS, :] = ones_row


def _qkv(h2, gain, w_qk, w_v_t, q_gain, k_gain, cos_t, sin_t, seq, tm=512):
    tokens = h2.shape[0]
    tm = min(tm, seq)
    assert tokens % tm == 0 and seq % tm == 0
    n_pos_blocks = seq // tm
    tok = lambda i: (i, 0)
    full = lambda i: (0, 0)
    pos = lambda i: (i % n_pos_blocks, 0)
    vt_map = lambda i: (i // n_pos_blocks, 0, i % n_pos_blocks, 0, 0)
    out = jax.ShapeDtypeStruct((tokens, D_MODEL), BF16)
    vt_shape = (tokens // seq, N_HEADS, n_pos_blocks, V_ROWS, tm)
    return pl.pallas_call(
        _qkv_kernel,
        grid=(tokens // tm,),
        in_specs=[
            pl.BlockSpec((tm, D_MODEL), tok),
            pl.BlockSpec((1, D_MODEL), full),
            pl.BlockSpec((D_MODEL, 2 * D_MODEL), full),
            pl.BlockSpec((D_MODEL, D_MODEL), full),
            pl.BlockSpec((1, LANES), full),
            pl.BlockSpec((1, LANES), full),
            pl.BlockSpec((tm, LANES), pos),
            pl.BlockSpec((tm, LANES), pos),
        ],
        out_specs=[pl.BlockSpec((tm, D_MODEL), tok), pl.BlockSpec((tm, D_MODEL), tok),
                   pl.BlockSpec((None, N_HEADS, None, V_ROWS, tm), vt_map)],
        out_shape=[out, out, jax.ShapeDtypeStruct(vt_shape, BF16)],
        compiler_params=_params("parallel"),
        name="qkv_norm_rope",
    )(h2, gain, w_qk, w_v_t, q_gain, k_gain, cos_t, sin_t)


def _stack_q(q_ref, qs_ref):
    tq = q_ref.shape[0]
    q = q_ref[...]
    lane = lax.broadcasted_iota(jnp.int32, q.shape, 1)
    zero = jnp.zeros_like(q)
    qs_ref[0:tq, :] = jnp.where(lane < HEAD_DIM, q, zero)
    qs_ref[tq:2 * tq, :] = jnp.where(lane >= HEAD_DIM, q, zero)


def _attn_kernel(q_ref, k_ref, vt_ref, lam_ref, sg_ref, o_ref, qs_ref, acc_ref, *m_scratch,
                 lam_init, subtract_max):
    tq = q_ref.shape[0]
    n_chunks, _, tk = vt_ref.shape
    _stack_q(q_ref, qs_ref)
    acc_ref[...] = jnp.zeros(acc_ref.shape, F32)
    if subtract_max:
        m_ref, = m_scratch
        m_ref[...] = jnp.full(m_ref.shape, -jnp.inf, F32)

    def chunk(c, carry):
        off = pl.multiple_of(c * tk, tk)
        st = lax.dot_general(k_ref[pl.ds(off, tk), :], qs_ref[...], (((1,), (1,)), ((), ())),
                             preferred_element_type=F32)
        if subtract_max:
            m_prev = m_ref[...]
            m_new = jnp.maximum(m_prev, jnp.max(st, axis=0, keepdims=True))
            acc_ref[...] = acc_ref[...] * jnp.exp2(m_prev - m_new)
            m_ref[...] = m_new
            st = st - m_new
        acc_ref[...] += _dot(vt_ref[c], jnp.exp2(st).astype(BF16))
        return carry

    lax.fori_loop(0, n_chunks, chunk, 0, unroll=1 if subtract_max else 4)

    ot = acc_ref[0:V_HEAD_DIM, :] * (1.0 / acc_ref[V_HEAD_DIM:V_HEAD_DIM + 1, :])
    lv = lam_ref[...]
    lam = (jnp.exp(jnp.sum(lv[0:1] * lv[1:2], axis=1, keepdims=True))
           - jnp.exp(jnp.sum(lv[2:3] * lv[3:4], axis=1, keepdims=True)) + lam_init)
    od = ot[:, 0:tq] - lam * ot[:, tq:2 * tq]
    ms = jnp.mean(od * od, axis=0, keepdims=True)
    on = od * lax.rsqrt(ms + RMS_EPS) * sg_ref[...] * (1.0 - lam_init)
    o_ref[...] = on.T.astype(BF16)


def _attention_call(q, k, vt, lam_vecs, subln_col, lam_init, subtract_max, tq=2048):
    b, s, _ = q.shape
    tq = min(tq, s)
    assert s % tq == 0
    n_chunks, tk = vt.shape[2], vt.shape[4]
    qmap = lambda bi, hi, qi: (bi, qi, hi)
    full = lambda bi, hi, qi: (0, 0)
    scratch = [pltpu.VMEM((2 * tq, LANES), BF16), pltpu.VMEM((V_ROWS, 2 * tq), F32)]
    if subtract_max:
        scratch.append(pltpu.VMEM((1, 2 * tq), F32))
    return pl.pallas_call(
        functools.partial(_attn_kernel, lam_init=lam_init, subtract_max=subtract_max),
        grid=(b, N_HEADS, s // tq),
        in_specs=[
            pl.BlockSpec((None, tq, LANES), qmap),
            pl.BlockSpec((None, s, LANES), lambda bi, hi, qi: (bi, 0, hi)),
            pl.BlockSpec((None, None, n_chunks, V_ROWS, tk), lambda bi, hi, qi: (bi, hi, 0, 0, 0)),
            pl.BlockSpec((4, HEAD_DIM), full),
            pl.BlockSpec((V_HEAD_DIM, 1), full),
        ],
        out_specs=pl.BlockSpec((None, tq, LANES), qmap),
        out_shape=jax.ShapeDtypeStruct((b, s, D_MODEL), BF16),
        scratch_shapes=scratch,
        compiler_params=_params("parallel", "parallel", "parallel"),
        name="diff_attention_online" if subtract_max else "diff_attention_bounded",
    )(q, k, vt, lam_vecs, subln_col)


def _attention(q, k, vt, q_gain, k_gain, lam_vecs, subln_col, lam_init):
    score_bound = (HEAD_DIM * Q_SCALE) * jnp.max(jnp.abs(q_gain)) * jnp.max(jnp.abs(k_gain))
    return lax.cond(score_bound < SAFE_LOG2_SCORE,
                    lambda *a: _attention_call(*a, lam_init, False),
                    lambda *a: _attention_call(*a, lam_init, True),
                    q, k, vt, lam_vecs, subln_col)


MLP_FF_TILE = 1024


def _mlp_body(h, g_ref, wu_ref, wd_ref, out_ref):
    xn = _rms(h, g_ref[...]).astype(BF16)
    acc = h
    for f in range(D_FF // MLP_FF_TILE):
        sl = slice(f * MLP_FF_TILE, (f + 1) * MLP_FF_TILE)
        a = jnp.maximum(_dot(xn, wu_ref[:, sl]), 0.0)
        acc = acc + _dot((a * a).astype(BF16), wd_ref[sl, :])
    out_ref[...] = acc


def _mlp_kernel(h_ref, g_ref, wu_ref, wd_ref, out_ref):
    _mlp_body(h_ref[...], g_ref, wu_ref, wd_ref, out_ref)


def _proj_mlp_kernel(h_ref, o_ref, wo_ref, g_ref, wu_ref, wd_ref, out_ref):
    _mlp_body(h_ref[...] + _dot(o_ref[...], wo_ref[...]), g_ref, wu_ref, wd_ref, out_ref)


def _resident(shape):
    return pl.BlockSpec(shape, lambda i: (0,) * len(shape), pipeline_mode=pl.Buffered(1))


def _mlp(h2, gain, w_up, w_down, attn_out=None, w_o=None, tm=512):
    tokens = h2.shape[0]
    tok = pl.BlockSpec((tm, D_MODEL), lambda i: (i, 0))
    mlp_specs = [_resident((1, D_MODEL)), _resident((D_MODEL, D_FF)), _resident((D_FF, D_MODEL))]
    if attn_out is None:
        body, name = _mlp_kernel, "relu2_mlp"
        in_specs, args = [tok] + mlp_specs, (h2, gain, w_up, w_down)
    else:
        body, name = _proj_mlp_kernel, "attn_proj_relu2_mlp"
        in_specs = [tok, tok, _resident((D_MODEL, D_MODEL))] + mlp_specs
        args = (h2, attn_out, w_o, gain, w_up, w_down)
    return pl.pallas_call(
        body,
        grid=(tokens // tm,),
        in_specs=in_specs,
        out_specs=tok,
        out_shape=jax.ShapeDtypeStruct((tokens, D_MODEL), F32),
        compiler_params=_params("parallel"),
        name=name,
    )(*args)


def _norm_group_kernel(h_ref, g_ref, u_ref):
    chunk = h_ref.shape[1]
    for b in range(h_ref.shape[0]):
        ut = _rms(h_ref[b], g_ref[...]).T
        u_ref[:, b] = ut.reshape(N_GROUPS, SSM_GROUP, chunk)


def _norm_group(h4, gain):
    batch, n_chunks, chunk, _ = h4.shape
    return pl.pallas_call(
        _norm_group_kernel,
        grid=(n_chunks,),
        in_specs=[pl.BlockSpec((batch, None, chunk, D_MODEL), lambda c: (0, c, 0, 0)),
                  pl.BlockSpec((1, D_MODEL), lambda c: (0, 0))],
        out_specs=pl.BlockSpec((N_GROUPS, None, batch, SSM_GROUP, chunk), lambda c: (0, c, 0, 0, 0)),
        out_shape=jax.ShapeDtypeStruct((N_GROUPS, n_chunks, batch, SSM_GROUP, chunk), F32),
        compiler_params=_params("parallel"),
        name="ssm_pre_norm",
    )(h4, gain)


def _ssm_kernel(u_ref, e_ref, e_next_ref, pin_ref, bb_ref, pout_ref, cc_ref, al_ref, y_ref,
                t_even_ref, t_odd_ref, zs_ref, sfr_ref, sbr_ref, sfi_ref, sbi_ref):
    n_chunks, batch, _, chunk = u_ref.shape
    rows, width = n_chunks * batch, SSM_GROUP * chunk
    p = SSM_STATE
    assert chunk == LANES

    upper = (lax.broadcasted_iota(jnp.int32, (chunk, chunk), 1)
             >= lax.broadcasted_iota(jnp.int32, (chunk, chunk), 0))

    def build_t(lag_ref, t_ref):
        for h in range(SSM_GROUP):
            for ho in range(SSM_GROUP):
                k_bwd = jnp.broadcast_to(lag_ref[h, ho:ho + 1, 0:chunk], (chunk, chunk))
                k_fwd = jnp.broadcast_to(lag_ref[h, ho:ho + 1, chunk:2 * chunk], (chunk, chunk))
                t_fwd = pltpu.roll(k_fwd, 0, 1, stride=1, stride_axis=0)
                t_bwd = pltpu.roll(k_bwd, 0, 1, stride=1, stride_axis=0)
                t_ref[chunk * h:chunk * (h + 1), chunk * ho:chunk * (ho + 1)] = (
                    jnp.where(upper, t_fwd, t_bwd).astype(BF16))

    group = pl.program_id(0)

    @pl.when(group == 0)
    def _():
        build_t(e_ref, t_even_ref)

    def outer(pow_ref, vec_ref):
        ar_, ai_ = pow_ref[0][None], pow_ref[1][None]
        vr_, vi_ = vec_ref[0][:, None, :], vec_ref[1][:, None, :]
        return ((ar_ * vr_ - ai_ * vi_).reshape(width, 2 * p),
                (ar_ * vi_ + ai_ * vr_).reshape(width, 2 * p))

    wi_re, wi_im = outer(pin_ref, bb_ref)
    w_in = jnp.concatenate([wi_re, wi_im], axis=1).astype(BF16)
    wo_re, wo_im = outer(pout_ref, cc_ref)
    w_out_t = jnp.concatenate([wo_re, -wo_im], axis=1).astype(BF16)

    u = jnp.concatenate([u_ref[:, :, h, :].reshape(rows, chunk) for h in range(SSM_GROUP)],
                        axis=1).astype(BF16)
    zs_ref[...] = _dot(u, w_in)

    ar = al_ref[0:1, :]
    ai = al_ref[1:2, :]
    is_fwd = lax.broadcasted_iota(jnp.int32, (batch, 2 * p), 1) < p

    def step(kk, carry):
        xr, xi = carry
        rf = pl.multiple_of(kk * batch, batch)
        rb = pl.multiple_of((n_chunks - 1 - kk) * batch, batch)
        lr = jnp.where(is_fwd, zs_ref[pl.ds(rf, batch), 0:2 * p], zs_ref[pl.ds(rb, batch), 0:2 * p])
        li = jnp.where(is_fwd, zs_ref[pl.ds(rf, batch), 2 * p:4 * p],
                       zs_ref[pl.ds(rb, batch), 2 * p:4 * p])
        sfr_ref[pl.ds(rf, batch), :] = xr
        sbr_ref[pl.ds(rb, batch), :] = xr
        sfi_ref[pl.ds(rf, batch), :] = xi
        sbi_ref[pl.ds(rb, batch), :] = xi
        return ar * xr - ai * xi + lr, ar * xi + ai * xr + li

    zero = jnp.zeros((batch, 2 * p), F32)
    lax.fori_loop(0, n_chunks, step, (zero, zero))

    fwd_lane = lax.broadcasted_iota(jnp.int32, (rows, 2 * p), 1) < p
    xin = jnp.concatenate(
        [jnp.where(fwd_lane, sfr_ref[...], sbr_ref[...]),
         jnp.where(fwd_lane, sfi_ref[...], sbi_ref[...])], axis=1).astype(BF16)
    carry_out = lax.dot_general(xin, w_out_t, (((1,), (1,)), ((), ())), preferred_element_type=F32)

    def finish(t_ref, t_next_ref):
        build_t(e_next_ref, t_next_ref)
        y = _dot(u, t_ref[...]) + carry_out
        for h in range(SSM_GROUP):
            y_ref[h] = y[:, chunk * h:chunk * (h + 1)]

    @pl.when(group % 2 == 0)
    def _():
        finish(t_even_ref, t_odd_ref)

    @pl.when(group % 2 == 1)
    def _():
        finish(t_odd_ref, t_even_ref)


def _ssm(ug, e_rows, pow_in, b_bar, pow_out, c_mat, a_chunk):
    g, n_chunks, batch, _, chunk = ug.shape
    rows, width = n_chunks * batch, SSM_GROUP * chunk
    p = SSM_STATE
    grp = lambda i: (i, 0, 0)
    grp4 = lambda i: (i, 0, 0, 0)
    return pl.pallas_call(
        _ssm_kernel,
        grid=(g,),
        in_specs=[
            pl.BlockSpec((None, n_chunks, batch, SSM_GROUP, chunk), lambda i: (i, 0, 0, 0, 0)),
            pl.BlockSpec((None, SSM_GROUP, SSM_GROUP, 2 * chunk), grp4),
            pl.BlockSpec((None, SSM_GROUP, SSM_GROUP, 2 * chunk),
                         lambda i: (jnp.minimum(i + 1, g - 1), 0, 0, 0)),
            pl.BlockSpec((None, 2, chunk, 2 * p), grp4),
            pl.BlockSpec((None, 2, SSM_GROUP, 2 * p), grp4),
            pl.BlockSpec((None, 2, chunk, 2 * p), grp4),
            pl.BlockSpec((None, 2, SSM_GROUP, 2 * p), grp4),
            pl.BlockSpec((None, 2, 2 * p), grp),
        ],
        out_specs=pl.BlockSpec((None, SSM_GROUP, rows, chunk), grp4),
        out_shape=jax.ShapeDtypeStruct((g, SSM_GROUP, rows, chunk), F32),
        scratch_shapes=[
            pltpu.VMEM((width, width), BF16),
            pltpu.VMEM((width, width), BF16),
            pltpu.VMEM((rows, 4 * p), F32),
            pltpu.VMEM((rows, 2 * p), F32),
            pltpu.VMEM((rows, 2 * p), F32),
            pltpu.VMEM((rows, 2 * p), F32),
            pltpu.VMEM((rows, 2 * p), F32),
        ],
        compiler_params=_params("arbitrary"),
        name="s5_chunked_scan",
    )(ug, e_rows, e_rows, pow_in, b_bar, pow_out, c_mat, a_chunk)


def _glu_kernel(h_ref, y_ref, g_ref, d_ref, w_ref, out_ref):
    batch, chunk, _ = h_ref.shape
    acts = []
    for b in range(batch):
        u = _rms(h_ref[b], g_ref[...])
        y = y_ref[:, :, b, :].reshape(D_MODEL, chunk).T
        acts.append(jax.nn.gelu(y + d_ref[...] * u).astype(BF16))
    z = _dot(jnp.concatenate(acts, axis=0), w_ref[...])
    for b in range(batch):
        zb = z[b * chunk:(b + 1) * chunk]
        out_ref[b] = h_ref[b] + zb[:, 0:D_MODEL] * jax.nn.sigmoid(zb[:, D_MODEL:2 * D_MODEL])


def _glu(h4, y4, gain, d_skip, w_glu):
    batch, n_chunks, chunk, _ = h4.shape
    tok = pl.BlockSpec((batch, None, chunk, D_MODEL), lambda c: (0, c, 0, 0))
    return pl.pallas_call(
        _glu_kernel,
        grid=(n_chunks,),
        in_specs=[
            tok,
            pl.BlockSpec((N_GROUPS, SSM_GROUP, batch, chunk), lambda c: (0, 0, c, 0)),
            _resident((1, D_MODEL)),
            _resident((1, D_MODEL)),
            _resident((D_MODEL, 2 * D_MODEL)),
        ],
        out_specs=tok,
        out_shape=jax.ShapeDtypeStruct(h4.shape, F32),
        compiler_params=_params("parallel"),
        name="ssm_gelu_glu",
    )(h4, y4, gain, d_skip, w_glu)


def _ssm_operands(a_re, a_im, log_dt, b_re, b_im, c_re, c_im, chunk):
    hp = lax.Precision.HIGH
    dt = jnp.exp(log_dt)[..., None]
    zr, zi = a_re * dt, a_im * dt
    t = jnp.arange(chunk + 1, dtype=F32)[None, None, :, None]
    mag = jnp.exp(zr[:, :, None, :] * t)
    ang = zi[:, :, None, :] * t
    pr, pi = mag * jnp.cos(ang), mag * jnp.sin(ang)
    nr, ni = pr[:, :, 1, :] - 1.0, pi[:, :, 1, :]
    den = a_re * a_re + a_im * a_im
    qr = (nr * a_re + ni * a_im) / den
    qi = (ni * a_re - nr * a_im) / den
    bbr = qr[..., None] * b_re - qi[..., None] * b_im
    bbi = qr[..., None] * b_im + qi[..., None] * b_re

    cp_re, cp_im = c_re.transpose(0, 1, 3, 2)[..., None], c_im.transpose(0, 1, 3, 2)[..., None]
    w_re = cp_re * bbr[:, :, :, None, :] - cp_im * bbi[:, :, :, None, :]
    w_im = cp_re * bbi[:, :, :, None, :] + cp_im * bbr[:, :, :, None, :]
    n_pairs = SSM_GROUP * SSM_GROUP
    g = a_re.shape[1]
    kern = jnp.einsum('dgtp,dgpn->dgtn',
                      jnp.concatenate([pr[:, :, :chunk], -pi[:, :, :chunk]], axis=-1),
                      jnp.concatenate([w_re, w_im], axis=2).reshape(2, g, 2 * SSM_STATE, n_pairs),
                      precision=hp).reshape(2, g, chunk, SSM_GROUP, SSM_GROUP)
    center = (kern[0, :, 0] + kern[1, :, 0])[:, None]
    lagged = jnp.concatenate([jnp.zeros_like(center), kern[1, :, :0:-1], center, kern[0, :, 1:]],
                             axis=1)
    e_rows = lagged.transpose(0, 3, 2, 1)

    def fwd_bwd(x):
        return jnp.concatenate([x[0], x[1]], axis=-1)

    def re_im(xr, xi):
        return jnp.stack([fwd_bwd(xr), fwd_bwd(xi)], axis=1)

    pow_in = re_im(jnp.stack([pr[0, :, chunk - 1::-1], pr[1, :, :chunk]]),
                   jnp.stack([pi[0, :, chunk - 1::-1], pi[1, :, :chunk]]))
    b_bar = re_im(bbr.transpose(0, 1, 3, 2), bbi.transpose(0, 1, 3, 2))
    pow_out = re_im(jnp.stack([pr[0, :, 1:], pr[1, :, :0:-1]]),
                    jnp.stack([pi[0, :, 1:], pi[1, :, :0:-1]]))
    c_mat = re_im(c_re, c_im)
    a_chunk = jnp.stack([jnp.concatenate([pr[0, :, chunk], pr[1, :, chunk]], axis=-1),
                         jnp.concatenate([pi[0, :, chunk], pi[1, :, chunk]], axis=-1)], axis=1)
    return e_rows, pow_in, b_bar, pow_out, c_mat, a_chunk


def _ssm_layer(h2, batch, seq, gain, a_re, a_im, log_dt, b_re, b_im, c_re, c_im, d_skip, w_glu):
    chunk = SSM_CHUNK
    n_chunks = seq // chunk
    operands = _ssm_operands(a_re, a_im, log_dt, b_re, b_im, c_re, c_im, chunk)
    h4 = h2.reshape(batch, n_chunks, chunk, D_MODEL)
    ug = _norm_group(h4, gain)
    y4 = _ssm(ug, *operands)
    return _glu(h4, y4, gain, d_skip, w_glu).reshape(batch * seq, D_MODEL)


def _rope_tables(seq):
    pos = jnp.arange(seq, dtype=F32)
    inv_freq = ROPE_THETA ** (-jnp.arange(0, HEAD_DIM, 2, dtype=F32) / HEAD_DIM)
    ang = pos[:, None] * inv_freq[None, :]
    cos, sin = jnp.cos(ang), jnp.sin(ang)
    reps = LANES // HEAD_DIM
    cos_t = jnp.tile(jnp.concatenate([cos, cos], axis=-1), (1, reps))
    sin_t = jnp.tile(jnp.concatenate([-sin, sin], axis=-1), (1, reps))
    return cos_t, sin_t


def _lambda_init(layer_idx):
    return 0.8 - 0.6 * math.exp(-0.3 * layer_idx)


def kernel(x, norm_mix, norm_ffn, attn_w_qkv, attn_q_gain, attn_k_gain, attn_lambda,
           attn_subln, attn_w_o, ssm_a_re, ssm_a_im, ssm_log_dt, ssm_b_re, ssm_b_im,
           ssm_c_re, ssm_c_im, ssm_d, ssm_w_glu, ffn_w_up, ffn_w_down):
    batch, seq, d = x.shape
    assert d == D_MODEL
    depth = norm_mix.shape[0]
    cos_t, sin_t = _rope_tables(seq)
    reps = LANES // HEAD_DIM
    h2 = x.reshape(batch * seq, D_MODEL)
    for i in range(depth):
        j = i // N_MIXERS
        gain = norm_mix[i][None, :]
        if i % N_MIXERS == 0:
            qg, kg = attn_q_gain[j], attn_k_gain[j]
            w_qk = attn_w_qkv[j][:, :2 * D_MODEL].astype(BF16)
            w_v_t = attn_w_qkv[j][:, 2 * D_MODEL:].T.astype(BF16)
            q, k, vt = _qkv(h2, gain, w_qk, w_v_t, jnp.tile(qg, reps)[None, :],
                            jnp.tile(kg, reps)[None, :], cos_t, sin_t, seq)
            o = _attention(q.reshape(batch, seq, D_MODEL), k.reshape(batch, seq, D_MODEL), vt, qg, kg,
                           attn_lambda[j], attn_subln[j][:, None], _lambda_init(i))
            proj = dict(attn_out=o.reshape(batch * seq, D_MODEL), w_o=attn_w_o[j].astype(BF16))
        else:
            h2 = _ssm_layer(h2, batch, seq, gain, ssm_a_re[j], ssm_a_im[j], ssm_log_dt[j],
                            ssm_b_re[j], ssm_b_im[j], ssm_c_re[j], ssm_c_im[j],
                            ssm_d[j][None, :], ssm_w_glu[j].astype(BF16))
            proj = {}
        h2 = _mlp(h2, norm_ffn[i][None, :], ffn_w_up[i].astype(BF16), ffn_w_down[i].astype(BF16),
                  **proj)
    return h2.reshape(batch, seq, D_MODEL)
```

```python
import functools
import math

import jax
import jax.numpy as jnp
from jax import lax
from jax.experimental import pallas as pl
from jax.experimental.pallas import tpu as pltpu

F32 = jnp.float32
BF16 = jnp.bfloat16

D_MODEL = 1024
N_HEADS = 8
HEAD_DIM = 64
V_HEAD_DIM = 2 * HEAD_DIM
V_ROWS = V_HEAD_DIM + 16
ROPE_THETA = 10000.0
SSM_GROUP = 16
N_GROUPS = D_MODEL // SSM_GROUP
SSM_STATE = 64
D_FF = 4 * D_MODEL
RMS_EPS = 1e-6
Q_SCALE = HEAD_DIM ** -0.5 * math.log2(math.e)
SAFE_LOG2_SCORE = 80.0
N_MIXERS = 2

LANES = 128
MXU_DIM = 256
SSM_CHUNK = LANES
VMEM_LIMIT = 52 * 1024 * 1024


def _params(*sem):
    return pltpu.CompilerParams(dimension_semantics=sem, vmem_limit_bytes=VMEM_LIMIT)


def _rms(x, gain):
    ms = jnp.mean(x * x, axis=-1, keepdims=True)
    return x * lax.rsqrt(ms + RMS_EPS) * gain


def _dot(a, b):
    return jnp.dot(a, b, preferred_element_type=F32)


def _qkv_kernel(x_ref, g_ref, wqk_ref, wvt_ref, qg_ref, kg_ref, cos_ref, sin_ref,
                q_ref, k_ref, vt_ref):
    tm = x_ref.shape[0]
    xn = _rms(x_ref[...], g_ref[...]).astype(BF16)
    row = lax.broadcasted_iota(jnp.int32, (MXU_DIM, MXU_DIM), 0)
    col = lax.broadcasted_iota(jnp.int32, (MXU_DIM, MXU_DIM), 1)
    seg_mean = jnp.where(row // HEAD_DIM == col // HEAD_DIM, 1.0 / HEAD_DIM, 0.0).astype(BF16)
    lane = lax.broadcasted_iota(jnp.int32, (tm, LANES), 1)
    first_half = (lane % HEAD_DIM) < (HEAD_DIM // 2)
    cos = cos_ref[...]
    sin = sin_ref[...]

    def norm_rope(y, gain, scale, out_ref):
        sq = (y * y).astype(BF16)
        for t in range(D_MODEL // MXU_DIM):
            ms = _dot(sq[:, t * MXU_DIM:(t + 1) * MXU_DIM], seg_mean)
            for s in range(MXU_DIM // LANES):
                sl = slice(t * MXU_DIM + s * LANES, t * MXU_DIM + (s + 1) * LANES)
                yn = y[:, sl] * lax.rsqrt(ms[:, s * LANES:(s + 1) * LANES] + RMS_EPS) * gain
                partner = jnp.where(first_half,
                                    pltpu.roll(yn, LANES - HEAD_DIM // 2, 1),
                                    pltpu.roll(yn, HEAD_DIM // 2, 1))
                out_ref[:, sl] = ((yn * cos + partner * sin) * scale).astype(BF16)

    norm_rope(_dot(xn, wqk_ref[:, 0:D_MODEL]), qg_ref[...], Q_SCALE, q_ref)
    norm_rope(_dot(xn, wqk_ref[:, D_MODEL:2 * D_MODEL]), kg_ref[...], 1.0, k_ref)
    yvt = lax.dot_general(wvt_ref[...], xn, (((1,), (1,)), ((), ())),
                          preferred_element_type=F32)
    pad_rows = lax.broadcasted_iota(jnp.int32, (V_ROWS - V_HEAD_DIM, tm), 0)
    ones_row = jnp.where(pad_rows == 0, 1.0, 0.0).astype(BF16)
    for t in range(N_HEADS):
        vt_ref[t, 0:V_HEAD_DIM, :] = yvt[t * V_HEAD_DIM:(t + 1) * V_HEAD_DIM, :].astype(BF16)
        vt_ref[t, V_HEAD_DIM:V_ROWS, :] = ones_row


def _qkv(h2, gain, w_qk, w_v_t, q_gain, k_gain, cos_t, sin_t, seq, tm=512):
    tokens = h2.shape[0]
    tm = min(tm, seq)
    assert tokens % tm == 0 and seq % tm == 0
    n_pos_blocks = seq // tm
    tok = lambda i: (i, 0)
    full = lambda i: (0, 0)
    pos = lambda i: (i % n_pos_blocks, 0)
    vt_map = lambda i: (i // n_pos_blocks, 0, i % n_pos_blocks, 0, 0)
    out = jax.ShapeDtypeStruct((tokens, D_MODEL), BF16)
    vt_shape = (tokens // seq, N_HEADS, n_pos_blocks, V_ROWS, tm)
    return pl.pallas_call(
        _qkv_kernel,
        grid=(tokens // tm,),
        in_specs=[
            pl.BlockSpec((tm, D_MODEL), tok),
            pl.BlockSpec((1, D_MODEL), full),
            pl.BlockSpec((D_MODEL, 2 * D_MODEL), full),
            pl.BlockSpec((D_MODEL, D_MODEL), full),
            pl.BlockSpec((1, LANES), full),
            pl.BlockSpec((1, LANES), full),
            pl.BlockSpec((tm, LANES), pos),
            pl.BlockSpec((tm, LANES), pos),
        ],
        out_specs=[pl.BlockSpec((tm, D_MODEL), tok), pl.BlockSpec((tm, D_MODEL), tok),
                   pl.BlockSpec((None, N_HEADS, None, V_ROWS, tm), vt_map)],
        out_shape=[out, out, jax.ShapeDtypeStruct(vt_shape, BF16)],
        compiler_params=_params("parallel"),
        name="qkv_norm_rope",
    )(h2, gain, w_qk, w_v_t, q_gain, k_gain, cos_t, sin_t)


def _stack_q(q_ref, qs_ref):
    tq = q_ref.shape[0]
    q = q_ref[...]
    lane = lax.broadcasted_iota(jnp.int32, q.shape, 1)
    zero = jnp.zeros_like(q)
    qs_ref[0:tq, :] = jnp.where(lane < HEAD_DIM, q, zero)
    qs_ref[tq:2 * tq, :] = jnp.where(lane >= HEAD_DIM, q, zero)


def _attn_kernel(q_ref, k_ref, vt_ref, lam_ref, sg_ref, o_ref, qs_ref, acc_ref, *m_scratch,
                 lam_init, subtract_max):
    tq = q_ref.shape[0]
    n_chunks, _, tk = vt_ref.shape
    _stack_q(q_ref, qs_ref)
    acc_ref[...] = jnp.zeros(acc_ref.shape, F32)
    if subtract_max:
        m_ref, = m_scratch
        m_ref[...] = jnp.full(m_ref.shape, -jnp.inf, F32)

    def chunk(c, carry):
        off = pl.multiple_of(c * tk, tk)
        st = lax.dot_general(k_ref[pl.ds(off, tk), :], qs_ref[...], (((1,), (1,)), ((), ())),
                             preferred_element_type=F32)
        if subtract_max:
            m_prev = m_ref[...]
            m_new = jnp.maximum(m_prev, jnp.max(st, axis=0, keepdims=True))
            acc_ref[...] = acc_ref[...] * jnp.exp2(m_prev - m_new)
            m_ref[...] = m_new
            st = st - m_new
        acc_ref[...] += _dot(vt_ref[c], jnp.exp2(st).astype(BF16))
        return carry

    lax.fori_loop(0, n_chunks, chunk, 0, unroll=1 if subtract_max else 4)

    ot = acc_ref[0:V_HEAD_DIM, :] * (1.0 / acc_ref[V_HEAD_DIM:V_HEAD_DIM + 1, :])
    lv = lam_ref[...]
    lam = (jnp.exp(jnp.sum(lv[0:1] * lv[1:2], axis=1, keepdims=True))
           - jnp.exp(jnp.sum(lv[2:3] * lv[3:4], axis=1, keepdims=True)) + lam_init)
    od = ot[:, 0:tq] - lam * ot[:, tq:2 * tq]
    ms = jnp.mean(od * od, axis=0, keepdims=True)
    on = od * lax.rsqrt(ms + RMS_EPS) * sg_ref[...] * (1.0 - lam_init)
    o_ref[...] = on.T.astype(BF16)


def _attention_call(q, k, vt, lam_vecs, subln_col, lam_init, subtract_max, tq=4096):
    b, s, _ = q.shape
    tq = min(tq, s)
    assert s % tq == 0
    n_chunks, tk = vt.shape[2], vt.shape[4]
    qmap = lambda bi, hi, qi: (bi, qi, hi)
    full = lambda bi, hi, qi: (0, 0)
    scratch = [pltpu.VMEM((2 * tq, LANES), BF16), pltpu.VMEM((V_ROWS, 2 * tq), F32)]
    if subtract_max:
        scratch.append(pltpu.VMEM((1, 2 * tq), F32))
    return pl.pallas_call(
        functools.partial(_attn_kernel, lam_init=lam_init, subtract_max=subtract_max),
        grid=(b, N_HEADS, s // tq),
        in_specs=[
            pl.BlockSpec((None, tq, LANES), qmap),
            pl.BlockSpec((None, s, LANES), lambda bi, hi, qi: (bi, 0, hi)),
            pl.BlockSpec((None, None, n_chunks, V_ROWS, tk), lambda bi, hi, qi: (bi, hi, 0, 0, 0)),
            pl.BlockSpec((4, HEAD_DIM), full),
            pl.BlockSpec((V_HEAD_DIM, 1), full),
        ],
        out_specs=pl.BlockSpec((None, tq, LANES), qmap),
        out_shape=jax.ShapeDtypeStruct((b, s, D_MODEL), BF16),
        scratch_shapes=scratch,
        compiler_params=_params("parallel", "parallel", "parallel"),
        name="diff_attention_online" if subtract_max else "diff_attention_bounded",
    )(q, k, vt, lam_vecs, subln_col)


def _attention(q, k, vt, q_gain, k_gain, lam_vecs, subln_col, lam_init):
    score_bound = (HEAD_DIM * Q_SCALE) * jnp.max(jnp.abs(q_gain)) * jnp.max(jnp.abs(k_gain))
    return lax.cond(score_bound < SAFE_LOG2_SCORE,
                    lambda *a: _attention_call(*a, lam_init, False),
                    lambda *a: _attention_call(*a, lam_init, True),
                    q, k, vt, lam_vecs, subln_col)


MLP_FF_TILE = 1024


def _mlp_body(h, g_ref, wu_ref, wd_ref, out_ref):
    xn = _rms(h, g_ref[...]).astype(BF16)
    acc = h
    for f in range(D_FF // MLP_FF_TILE):
        sl = slice(f * MLP_FF_TILE, (f + 1) * MLP_FF_TILE)
        a = jnp.maximum(_dot(xn, wu_ref[:, sl]), 0.0)
        acc = acc + _dot((a * a).astype(BF16), wd_ref[sl, :])
    out_ref[...] = acc


def _mlp_kernel(h_ref, g_ref, wu_ref, wd_ref, out_ref):
    _mlp_body(h_ref[...], g_ref, wu_ref, wd_ref, out_ref)


def _proj_mlp_kernel(h_ref, o_ref, wo_ref, g_ref, wu_ref, wd_ref, out_ref):
    _mlp_body(h_ref[...] + _dot(o_ref[...], wo_ref[...]), g_ref, wu_ref, wd_ref, out_ref)


def _resident(shape):
    return pl.BlockSpec(shape, lambda i: (0,) * len(shape), pipeline_mode=pl.Buffered(1))


def _mlp(h2, gain, w_up, w_down, attn_out=None, w_o=None, tm=1024):
    tokens = h2.shape[0]
    tok = pl.BlockSpec((tm, D_MODEL), lambda i: (i, 0))
    mlp_specs = [_resident((1, D_MODEL)), _resident((D_MODEL, D_FF)), _resident((D_FF, D_MODEL))]
    if attn_out is None:
        body, name = _mlp_kernel, "relu2_mlp"
        in_specs, args = [tok] + mlp_specs, (h2, gain, w_up, w_down)
    else:
        body, name = _proj_mlp_kernel, "attn_proj_relu2_mlp"
        in_specs = [tok, tok, _resident((D_MODEL, D_MODEL))] + mlp_specs
        args = (h2, attn_out, w_o, gain, w_up, w_down)
    return pl.pallas_call(
        body,
        grid=(tokens // tm,),
        in_specs=in_specs,
        out_specs=tok,
        out_shape=jax.ShapeDtypeStruct((tokens, D_MODEL), F32),
        compiler_params=_params("parallel"),
        name=name,
    )(*args)


def _norm_group_kernel(h_ref, g_ref, u_ref):
    chunk = h_ref.shape[1]
    for b in range(h_ref.shape[0]):
        ut = _rms(h_ref[b], g_ref[...]).T
        u_ref[:, b] = ut.reshape(N_GROUPS, SSM_GROUP, chunk)


def _norm_group(h4, gain):
    batch, n_chunks, chunk, _ = h4.shape
    return pl.pallas_call(
        _norm_group_kernel,
        grid=(n_chunks,),
        in_specs=[pl.BlockSpec((batch, None, chunk, D_MODEL), lambda c: (0, c, 0, 0)),
                  pl.BlockSpec((1, D_MODEL), lambda c: (0, 0))],
        out_specs=pl.BlockSpec((N_GROUPS, None, batch, SSM_GROUP, chunk), lambda c: (0, c, 0, 0, 0)),
        out_shape=jax.ShapeDtypeStruct((N_GROUPS, n_chunks, batch, SSM_GROUP, chunk), F32),
        compiler_params=_params("parallel"),
        name="ssm_pre_norm",
    )(h4, gain)


def _ssm_kernel(u_ref, e_ref, e_next_ref, pin_ref, bb_ref, pout_ref, cc_ref, al_ref, y_ref,
                t_even_ref, t_odd_ref, zs_ref, sfr_ref, sbr_ref, sfi_ref, sbi_ref):
    n_chunks, batch, _, chunk = u_ref.shape
    rows, width = n_chunks * batch, SSM_GROUP * chunk
    p = SSM_STATE
    assert chunk == LANES

    upper = (lax.broadcasted_iota(jnp.int32, (chunk, chunk), 1)
             >= lax.broadcasted_iota(jnp.int32, (chunk, chunk), 0))

    def build_t(lag_ref, t_ref):
        for h in range(SSM_GROUP):
            for ho in range(SSM_GROUP):
                k_bwd = jnp.broadcast_to(lag_ref[h, ho:ho + 1, 0:chunk], (chunk, chunk))
                k_fwd = jnp.broadcast_to(lag_ref[h, ho:ho + 1, chunk:2 * chunk], (chunk, chunk))
                t_fwd = pltpu.roll(k_fwd, 0, 1, stride=1, stride_axis=0)
                t_bwd = pltpu.roll(k_bwd, 0, 1, stride=1, stride_axis=0)
                t_ref[chunk * h:chunk * (h + 1), chunk * ho:chunk * (ho + 1)] = (
                    jnp.where(upper, t_fwd, t_bwd).astype(BF16))

    group = pl.program_id(0)

    @pl.when(group == 0)
    def _():
        build_t(e_ref, t_even_ref)

    def outer(pow_ref, vec_ref):
        ar_, ai_ = pow_ref[0][None], pow_ref[1][None]
        vr_, vi_ = vec_ref[0][:, None, :], vec_ref[1][:, None, :]
        return ((ar_ * vr_ - ai_ * vi_).reshape(width, 2 * p),
                (ar_ * vi_ + ai_ * vr_).reshape(width, 2 * p))

    wi_re, wi_im = outer(pin_ref, bb_ref)
    w_in = jnp.concatenate([wi_re, wi_im], axis=1).astype(BF16)
    wo_re, wo_im = outer(pout_ref, cc_ref)
    w_out_t = jnp.concatenate([wo_re, -wo_im], axis=1).astype(BF16)

    u = jnp.concatenate([u_ref[:, :, h, :].reshape(rows, chunk) for h in range(SSM_GROUP)],
                        axis=1).astype(BF16)
    zs_ref[...] = _dot(u, w_in)

    ar = al_ref[0:1, :]
    ai = al_ref[1:2, :]
    is_fwd = lax.broadcasted_iota(jnp.int32, (batch, 2 * p), 1) < p

    def step(kk, carry):
        xr, xi = carry
        rf = pl.multiple_of(kk * batch, batch)
        rb = pl.multiple_of((n_chunks - 1 - kk) * batch, batch)
        lr = jnp.where(is_fwd, zs_ref[pl.ds(rf, batch), 0:2 * p], zs_ref[pl.ds(rb, batch), 0:2 * p])
        li = jnp.where(is_fwd, zs_ref[pl.ds(rf, batch), 2 * p:4 * p],
                       zs_ref[pl.ds(rb, batch), 2 * p:4 * p])
        sfr_ref[pl.ds(rf, batch), :] = xr
        sbr_ref[pl.ds(rb, batch), :] = xr
        sfi_ref[pl.ds(rf, batch), :] = xi
        sbi_ref[pl.ds(rb, batch), :] = xi
        return ar * xr - ai * xi + lr, ar * xi + ai * xr + li

    zero = jnp.zeros((batch, 2 * p), F32)
    lax.fori_loop(0, n_chunks, step, (zero, zero))

    fwd_lane = lax.broadcasted_iota(jnp.int32, (rows, 2 * p), 1) < p
    xin = jnp.concatenate(
        [jnp.where(fwd_lane, sfr_ref[...], sbr_ref[...]),
         jnp.where(fwd_lane, sfi_ref[...], sbi_ref[...])], axis=1).astype(BF16)
    carry_out = lax.dot_general(xin, w_out_t, (((1,), (1,)), ((), ())), preferred_element_type=F32)

    def finish(t_ref, t_next_ref):
        build_t(e_next_ref, t_next_ref)
        y = _dot(u, t_ref[...]) + carry_out
        for h in range(SSM_GROUP):
            y_ref[h] = y[:, chunk * h:chunk * (h + 1)]

    @pl.when(group % 2 == 0)
    def _():
        finish(t_even_ref, t_odd_ref)

    @pl.when(group % 2 == 1)
    def _():
        finish(t_odd_ref, t_even_ref)


def _ssm(ug, e_rows, pow_in, b_bar, pow_out, c_mat, a_chunk):
    g, n_chunks, batch, _, chunk = ug.shape
    rows, width = n_chunks * batch, SSM_GROUP * chunk
    p = SSM_STATE
    grp = lambda i: (i, 0, 0)
    grp4 = lambda i: (i, 0, 0, 0)
    return pl.pallas_call(
        _ssm_kernel,
        grid=(g,),
        in_specs=[
            pl.BlockSpec((None, n_chunks, batch, SSM_GROUP, chunk), lambda i: (i, 0, 0, 0, 0)),
            pl.BlockSpec((None, SSM_GROUP, SSM_GROUP, 2 * chunk), grp4),
            pl.BlockSpec((None, SSM_GROUP, SSM_GROUP, 2 * chunk),
                         lambda i: (jnp.minimum(i + 1, g - 1), 0, 0, 0)),
            pl.BlockSpec((None, 2, chunk, 2 * p), grp4),
            pl.BlockSpec((None, 2, SSM_GROUP, 2 * p), grp4),
            pl.BlockSpec((None, 2, chunk, 2 * p), grp4),
            pl.BlockSpec((None, 2, SSM_GROUP, 2 * p), grp4),
            pl.BlockSpec((None, 2, 2 * p), grp),
        ],
        out_specs=pl.BlockSpec((None, SSM_GROUP, rows, chunk), grp4),
        out_shape=jax.ShapeDtypeStruct((g, SSM_GROUP, rows, chunk), F32),
        scratch_shapes=[
            pltpu.VMEM((width, width), BF16),
            pltpu.VMEM((width, width), BF16),
            pltpu.VMEM((rows, 4 * p), F32),
            pltpu.VMEM((rows, 2 * p), F32),
            pltpu.VMEM((rows, 2 * p), F32),
            pltpu.VMEM((rows, 2 * p), F32),
            pltpu.VMEM((rows, 2 * p), F32),
        ],
        compiler_params=_params("arbitrary"),
        name="s5_chunked_scan",
    )(ug, e_rows, e_rows, pow_in, b_bar, pow_out, c_mat, a_chunk)


def _glu_kernel(h_ref, y_ref, g_ref, d_ref, w_ref, out_ref):
    batch, chunk, _ = h_ref.shape
    acts = []
    for b in range(batch):
        u = _rms(h_ref[b], g_ref[...])
        y = y_ref[:, :, b, :].reshape(D_MODEL, chunk).T
        acts.append(jax.nn.gelu(y + d_ref[...] * u).astype(BF16))
    z = _dot(jnp.concatenate(acts, axis=0), w_ref[...])
    for b in range(batch):
        zb = z[b * chunk:(b + 1) * chunk]
        out_ref[b] = h_ref[b] + zb[:, 0:D_MODEL] * jax.nn.sigmoid(zb[:, D_MODEL:2 * D_MODEL])


def _glu(h4, y4, gain, d_skip, w_glu):
    batch, n_chunks, chunk, _ = h4.shape
    tok = pl.BlockSpec((batch, None, chunk, D_MODEL), lambda c: (0, c, 0, 0))
    return pl.pallas_call(
        _glu_kernel,
        grid=(n_chunks,),
        in_specs=[
            tok,
            pl.BlockSpec((N_GROUPS, SSM_GROUP, batch, chunk), lambda c: (0, 0, c, 0)),
            _resident((1, D_MODEL)),
            _resident((1, D_MODEL)),
            _resident((D_MODEL, 2 * D_MODEL)),
        ],
        out_specs=tok,
        out_shape=jax.ShapeDtypeStruct(h4.shape, F32),
        compiler_params=_params("parallel"),
        name="ssm_gelu_glu",
    )(h4, y4, gain, d_skip, w_glu)


def _ssm_operands(a_re, a_im, log_dt, b_re, b_im, c_re, c_im, chunk):
    hp = lax.Precision.HIGH
    dt = jnp.exp(log_dt)[..., None]
    zr, zi = a_re * dt, a_im * dt
    t = jnp.arange(chunk + 1, dtype=F32)[None, None, :, None]
    mag = jnp.exp(zr[:, :, None, :] * t)
    ang = zi[:, :, None, :] * t
    pr, pi = mag * jnp.cos(ang), mag * jnp.sin(ang)
    nr, ni = pr[:, :, 1, :] - 1.0, pi[:, :, 1, :]
    den = a_re * a_re + a_im * a_im
    qr = (nr * a_re + ni * a_im) / den
    qi = (ni * a_re - nr * a_im) / den
    bbr = qr[..., None] * b_re - qi[..., None] * b_im
    bbi = qr[..., None] * b_im + qi[..., None] * b_re

    cp_re, cp_im = c_re.transpose(0, 1, 3, 2)[..., None], c_im.transpose(0, 1, 3, 2)[..., None]
    w_re = cp_re * bbr[:, :, :, None, :] - cp_im * bbi[:, :, :, None, :]
    w_im = cp_re * bbi[:, :, :, None, :] + cp_im * bbr[:, :, :, None, :]
    n_pairs = SSM_GROUP * SSM_GROUP
    g = a_re.shape[1]
    kern = jnp.einsum('dgtp,dgpn->dgtn',
                      jnp.concatenate([pr[:, :, :chunk], -pi[:, :, :chunk]], axis=-1),
                      jnp.concatenate([w_re, w_im], axis=2).reshape(2, g, 2 * SSM_STATE, n_pairs),
                      precision=hp).reshape(2, g, chunk, SSM_GROUP, SSM_GROUP)
    center = (kern[0, :, 0] + kern[1, :, 0])[:, None]
    lagged = jnp.concatenate([jnp.zeros_like(center), kern[1, :, :0:-1], center, kern[0, :, 1:]],
                             axis=1)
    e_rows = lagged.transpose(0, 3, 2, 1)

    def fwd_bwd(x):
        return jnp.concatenate([x[0], x[1]], axis=-1)

    def re_im(xr, xi):
        return jnp.stack([fwd_bwd(xr), fwd_bwd(xi)], axis=1)

    pow_in = re_im(jnp.stack([pr[0, :, chunk - 1::-1], pr[1, :, :chunk]]),
                   jnp.stack([pi[0, :, chunk - 1::-1], pi[1, :, :chunk]]))
    b_bar = re_im(bbr.transpose(0, 1, 3, 2), bbi.transpose(0, 1, 3, 2))
    pow_out = re_im(jnp.stack([pr[0, :, 1:], pr[1, :, :0:-1]]),
                    jnp.stack([pi[0, :, 1:], pi[1, :, :0:-1]]))
    c_mat = re_im(c_re, c_im)
    a_chunk = jnp.stack([jnp.concatenate([pr[0, :, chunk], pr[1, :, chunk]], axis=-1),
                         jnp.concatenate([pi[0, :, chunk], pi[1, :, chunk]], axis=-1)], axis=1)
    return e_rows, pow_in, b_bar, pow_out, c_mat, a_chunk


def _ssm_layer(h2, batch, seq, gain, a_re, a_im, log_dt, b_re, b_im, c_re, c_im, d_skip, w_glu):
    chunk = SSM_CHUNK
    n_chunks = seq // chunk
    operands = _ssm_operands(a_re, a_im, log_dt, b_re, b_im, c_re, c_im, chunk)
    h4 = h2.reshape(batch, n_chunks, chunk, D_MODEL)
    ug = _norm_group(h4, gain)
    y4 = _ssm(ug, *operands)
    return _glu(h4, y4, gain, d_skip, w_glu).reshape(batch * seq, D_MODEL)


def _rope_tables(seq):
    pos = jnp.arange(seq, dtype=F32)
    inv_freq = ROPE_THETA ** (-jnp.arange(0, HEAD_DIM, 2, dtype=F32) / HEAD_DIM)
    ang = pos[:, None] * inv_freq[None, :]
    cos, sin = jnp.cos(ang), jnp.sin(ang)
    reps = LANES // HEAD_DIM
    cos_t = jnp.tile(jnp.concatenate([cos, cos], axis=-1), (1, reps))
    sin_t = jnp.tile(jnp.concatenate([-sin, sin], axis=-1), (1, reps))
    return cos_t, sin_t


def _lambda_init(layer_idx):
    return 0.8 - 0.6 * math.exp(-0.3 * layer_idx)


def kernel(x, norm_mix, norm_ffn, attn_w_qkv, attn_q_gain, attn_k_gain, attn_lambda,
           attn_subln, attn_w_o, ssm_a_re, ssm_a_im, ssm_log_dt, ssm_b_re, ssm_b_im,
           ssm_c_re, ssm_c_im, ssm_d, ssm_w_glu, ffn_w_up, ffn_w_down):
    batch, seq, d = x.shape
    assert d == D_MODEL
    depth = norm_mix.shape[0]
    cos_t, sin_t = _rope_tables(seq)
    reps = LANES // HEAD_DIM
    h2 = x.reshape(batch * seq, D_MODEL)
    for i in range(depth):
        j = i // N_MIXERS
        gain = norm_mix[i][None, :]
        if i % N_MIXERS == 0:
            qg, kg = attn_q_gain[j], attn_k_gain[j]
            w_qk = attn_w_qkv[j][:, :2 * D_MODEL].astype(BF16)
            w_v_t = attn_w_qkv[j][:, 2 * D_MODEL:].T.astype(BF16)
            q, k, vt = _qkv(h2, gain, w_qk, w_v_t, jnp.tile(qg, reps)[None, :],
                            jnp.tile(kg, reps)[None, :], cos_t, sin_t, seq)
            o = _attention(q.reshape(batch, seq, D_MODEL), k.reshape(batch, seq, D_MODEL), vt, qg, kg,
                           attn_lambda[j], attn_subln[j][:, None], _lambda_init(i))
            proj = dict(attn_out=o.reshape(batch * seq, D_MODEL), w_o=attn_w_o[j].astype(BF16))
        else:
            h2 = _ssm_layer(h2, batch, seq, gain, ssm_a_re[j], ssm_a_im[j], ssm_log_dt[j],
                            ssm_b_re[j], ssm_b_im[j], ssm_c_re[j], ssm_c_im[j],
                            ssm_d[j][None, :], ssm_w_glu[j].astype(BF16))
            proj = {}
        h2 = _mlp(h2, norm_ffn[i][None, :], ffn_w_up[i].astype(BF16), ffn_w_down[i].astype(BF16),
                  **proj)
    return h2.reshape(batch, seq, D_MODEL)
```

```python
import functools
import math

import jax
import jax.numpy as jnp
from jax import lax
from jax.experimental import pallas as pl
from jax.experimental.pallas import tpu as pltpu

F32 = jnp.float32
BF16 = jnp.bfloat16

D_MODEL = 1024
N_HEADS = 8
HEAD_DIM = 64
V_HEAD_DIM = 2 * HEAD_DIM
V_ROWS = V_HEAD_DIM + 16
ROPE_THETA = 10000.0
SSM_GROUP = 16
N_GROUPS = D_MODEL // SSM_GROUP
SSM_STATE = 64
D_FF = 4 * D_MODEL
RMS_EPS = 1e-6
Q_SCALE = HEAD_DIM ** -0.5 * math.log2(math.e)
SAFE_LOG2_SCORE = 80.0
N_MIXERS = 2

LANES = 128
MXU_DIM = 256
SSM_CHUNK = LANES
VMEM_LIMIT = 52 * 1024 * 1024


def _params(*sem):
    return pltpu.CompilerParams(dimension_semantics=sem, vmem_limit_bytes=VMEM_LIMIT)


def _rms(x, gain):
    ms = jnp.mean(x * x, axis=-1, keepdims=True)
    return x * lax.rsqrt(ms + RMS_EPS) * gain


def _dot(a, b):
    return jnp.dot(a, b, preferred_element_type=F32)


def _qkv_kernel(x_ref, g_ref, wqk_ref, wvt_ref, qg_ref, kg_ref, cos_ref, sin_ref,
                q_ref, k_ref, vt_ref):
    tm = x_ref.shape[0]
    xn = _rms(x_ref[...], g_ref[...]).astype(BF16)
    row = lax.broadcasted_iota(jnp.int32, (MXU_DIM, MXU_DIM), 0)
    col = lax.broadcasted_iota(jnp.int32, (MXU_DIM, MXU_DIM), 1)
    seg_mean = jnp.where(row // HEAD_DIM == col // HEAD_DIM, 1.0 / HEAD_DIM, 0.0).astype(BF16)
    lane = lax.broadcasted_iota(jnp.int32, (tm, LANES), 1)
    first_half = (lane % HEAD_DIM) < (HEAD_DIM // 2)
    cos = cos_ref[...]
    sin = sin_ref[...]

    def norm_rope(y, gain, scale, out_ref):
        sq = (y * y).astype(BF16)
        for t in range(D_MODEL // MXU_DIM):
            ms = _dot(sq[:, t * MXU_DIM:(t + 1) * MXU_DIM], seg_mean)
            for s in range(MXU_DIM // LANES):
                sl = slice(t * MXU_DIM + s * LANES, t * MXU_DIM + (s + 1) * LANES)
                yn = y[:, sl] * lax.rsqrt(ms[:, s * LANES:(s + 1) * LANES] + RMS_EPS) * gain
                partner = jnp.where(first_half,
                                    pltpu.roll(yn, LANES - HEAD_DIM // 2, 1),
                                    pltpu.roll(yn, HEAD_DIM // 2, 1))
                out_ref[:, sl] = ((yn * cos + partner * sin) * scale).astype(BF16)

    norm_rope(_dot(xn, wqk_ref[:, 0:D_MODEL]), qg_ref[...], Q_SCALE, q_ref)
    norm_rope(_dot(xn, wqk_ref[:, D_MODEL:2 * D_MODEL]), kg_ref[...], 1.0, k_ref)
    yvt = lax.dot_general(wvt_ref[...], xn, (((1,), (1,)), ((), ())),
                          preferred_element_type=F32)
    pad_rows = lax.broadcasted_iota(jnp.int32, (V_ROWS - V_HEAD_DIM, tm), 0)
    ones_row = jnp.where(pad_rows == 0, 1.0, 0.0).astype(BF16)
    for t in range(N_HEADS):
        vt_ref[t, 0:V_HEAD_DIM, :] = yvt[t * V_HEAD_DIM:(t + 1) * V_HEAD_DIM, :].astype(BF16)
        vt_ref[t, V_HEAD_DIM:V_ROWS, :] = ones_row


def _qkv(h2, gain, w_qk, w_v_t, q_gain, k_gain, cos_t, sin_t, seq, tm=512):
    tokens = h2.shape[0]
    tm = min(tm, seq)
    assert tokens % tm == 0 and seq % tm == 0
    n_pos_blocks = seq // tm
    tok = lambda i: (i, 0)
    full = lambda i: (0, 0)
    pos = lambda i: (i % n_pos_blocks, 0)
    vt_map = lambda i: (i // n_pos_blocks, 0, i % n_pos_blocks, 0, 0)
    out = jax.ShapeDtypeStruct((tokens, D_MODEL), BF16)
    vt_shape = (tokens // seq, N_HEADS, n_pos_blocks, V_ROWS, tm)
    return pl.pallas_call(
        _qkv_kernel,
        grid=(tokens // tm,),
        in_specs=[
            pl.BlockSpec((tm, D_MODEL), tok),
            pl.BlockSpec((1, D_MODEL), full),
            pl.BlockSpec((D_MODEL, 2 * D_MODEL), full),
            pl.BlockSpec((D_MODEL, D_MODEL), full),
            pl.BlockSpec((1, LANES), full),
            pl.BlockSpec((1, LANES), full),
            pl.BlockSpec((tm, LANES), pos),
            pl.BlockSpec((tm, LANES), pos),
        ],
        out_specs=[pl.BlockSpec((tm, D_MODEL), tok), pl.BlockSpec((tm, D_MODEL), tok),
                   pl.BlockSpec((None, N_HEADS, None, V_ROWS, tm), vt_map)],
        out_shape=[out, out, jax.ShapeDtypeStruct(vt_shape, BF16)],
        compiler_params=_params("parallel"),
        name="qkv_norm_rope",
    )(h2, gain, w_qk, w_v_t, q_gain, k_gain, cos_t, sin_t)


def _stack_q(q_ref, qs_ref):
    tq = q_ref.shape[0]
    q = q_ref[...]
    lane = lax.broadcasted_iota(jnp.int32, q.shape, 1)
    zero = jnp.zeros_like(q)
    qs_ref[0:tq, :] = jnp.where(lane < HEAD_DIM, q, zero)
    qs_ref[tq:2 * tq, :] = jnp.where(lane >= HEAD_DIM, q, zero)


def _attn_kernel(q_ref, k_ref, vt_ref, lam_ref, sg_ref, o_ref, qs_ref, acc_ref, *m_scratch,
                 lam_init, subtract_max):
    tq = q_ref.shape[0]
    n_chunks, _, tk = vt_ref.shape
    _stack_q(q_ref, qs_ref)
    acc_ref[...] = jnp.zeros(acc_ref.shape, F32)
    if subtract_max:
        m_ref, = m_scratch
        m_ref[...] = jnp.full(m_ref.shape, -jnp.inf, F32)

    def chunk(c, carry):
        off = pl.multiple_of(c * tk, tk)
        st = lax.dot_general(k_ref[pl.ds(off, tk), :], qs_ref[...], (((1,), (1,)), ((), ())),
                             preferred_element_type=F32)
        if subtract_max:
            m_prev = m_ref[...]
            m_new = jnp.maximum(m_prev, jnp.max(st, axis=0, keepdims=True))
            acc_ref[...] = acc_ref[...] * jnp.exp2(m_prev - m_new)
            m_ref[...] = m_new
            st = st - m_new
        acc_ref[...] += _dot(vt_ref[c], jnp.exp2(st).astype(BF16))
        return carry

    lax.fori_loop(0, n_chunks, chunk, 0, unroll=1 if subtract_max else 4)

    ot = acc_ref[0:V_HEAD_DIM, :] * (1.0 / acc_ref[V_HEAD_DIM:V_HEAD_DIM + 1, :])
    lv = lam_ref[...]
    lam = (jnp.exp(jnp.sum(lv[0:1] * lv[1:2], axis=1, keepdims=True))
           - jnp.exp(jnp.sum(lv[2:3] * lv[3:4], axis=1, keepdims=True)) + lam_init)
    od = ot[:, 0:tq] - lam * ot[:, tq:2 * tq]
    ms = jnp.mean(od * od, axis=0, keepdims=True)
    on = od * lax.rsqrt(ms + RMS_EPS) * sg_ref[...] * (1.0 - lam_init)
    o_ref[...] = on.T.astype(BF16)


def _attention_call(q, k, vt, lam_vecs, subln_col, lam_init, subtract_max, tq=4096):
    b, s, _ = q.shape
    tq = min(tq, s)
    assert s % tq == 0
    n_chunks, tk = vt.shape[2], vt.shape[4]
    qmap = lambda bi, hi, qi: (bi, qi, hi)
    full = lambda bi, hi, qi: (0, 0)
    scratch = [pltpu.VMEM((2 * tq, LANES), BF16), pltpu.VMEM((V_ROWS, 2 * tq), F32)]
    if subtract_max:
        scratch.append(pltpu.VMEM((1, 2 * tq), F32))
    return pl.pallas_call(
        functools.partial(_attn_kernel, lam_init=lam_init, subtract_max=subtract_max),
        grid=(b, N_HEADS, s // tq),
        in_specs=[
            pl.BlockSpec((None, tq, LANES), qmap),
            pl.BlockSpec((None, s, LANES), lambda bi, hi, qi: (bi, 0, hi)),
            pl.BlockSpec((None, None, n_chunks, V_ROWS, tk), lambda bi, hi, qi: (bi, hi, 0, 0, 0)),
            pl.BlockSpec((4, HEAD_DIM), full),
            pl.BlockSpec((V_HEAD_DIM, 1), full),
        ],
        out_specs=pl.BlockSpec((None, tq, LANES), qmap),
        out_shape=jax.ShapeDtypeStruct((b, s, D_MODEL), BF16),
        scratch_shapes=scratch,
        compiler_params=_params("parallel", "parallel", "parallel"),
        name="diff_attention_online" if subtract_max else "diff_attention_bounded",
    )(q, k, vt, lam_vecs, subln_col)


def _attention(q, k, vt, q_gain, k_gain, lam_vecs, subln_col, lam_init):
    score_bound = (HEAD_DIM * Q_SCALE) * jnp.max(jnp.abs(q_gain)) * jnp.max(jnp.abs(k_gain))
    return lax.cond(score_bound < SAFE_LOG2_SCORE,
                    lambda *a: _attention_call(*a, lam_init, False),
                    lambda *a: _attention_call(*a, lam_init, True),
                    q, k, vt, lam_vecs, subln_col)


MLP_FF_TILE = 1024


def _mlp_body(h, g_ref, wu_ref, wd_ref, out_ref):
    xn = _rms(h, g_ref[...]).astype(BF16)
    acc = h
    for f in range(D_FF // MLP_FF_TILE):
        sl = slice(f * MLP_FF_TILE, (f + 1) * MLP_FF_TILE)
        a = jnp.maximum(_dot(xn, wu_ref[:, sl]), 0.0)
        acc = acc + _dot((a * a).astype(BF16), wd_ref[sl, :])
    out_ref[...] = acc


def _mlp_kernel(h_ref, g_ref, wu_ref, wd_ref, out_ref):
    _mlp_body(h_ref[...], g_ref, wu_ref, wd_ref, out_ref)


def _proj_mlp_kernel(h_ref, o_ref, wo_ref, g_ref, wu_ref, wd_ref, out_ref):
    _mlp_body(h_ref[...] + _dot(o_ref[...], wo_ref[...]), g_ref, wu_ref, wd_ref, out_ref)


def _resident(shape):
    return pl.BlockSpec(shape, lambda i: (0,) * len(shape), pipeline_mode=pl.Buffered(1))


def _mlp(h2, gain, w_up, w_down, attn_out=None, w_o=None, tm=1024):
    tokens = h2.shape[0]
    tok = pl.BlockSpec((tm, D_MODEL), lambda i: (i, 0))
    mlp_specs = [_resident((1, D_MODEL)), _resident((D_MODEL, D_FF)), _resident((D_FF, D_MODEL))]
    if attn_out is None:
        body, name = _mlp_kernel, "relu2_mlp"
        in_specs, args = [tok] + mlp_specs, (h2, gain, w_up, w_down)
    else:
        body, name = _proj_mlp_kernel, "attn_proj_relu2_mlp"
        in_specs = [tok, tok, _resident((D_MODEL, D_MODEL))] + mlp_specs
        args = (h2, attn_out, w_o, gain, w_up, w_down)
    return pl.pallas_call(
        body,
        grid=(tokens // tm,),
        in_specs=in_specs,
        out_specs=tok,
        out_shape=jax.ShapeDtypeStruct((tokens, D_MODEL), F32),
        compiler_params=_params("parallel"),
        name=name,
    )(*args)


def _norm_group_kernel(h_ref, g_ref, u_ref):
    chunk = h_ref.shape[1]
    for b in range(h_ref.shape[0]):
        ut = _rms(h_ref[b], g_ref[...]).T
        u_ref[:, b] = ut.reshape(N_GROUPS, SSM_GROUP, chunk)


def _norm_group(h4, gain):
    batch, n_chunks, chunk, _ = h4.shape
    return pl.pallas_call(
        _norm_group_kernel,
        grid=(n_chunks,),
        in_specs=[pl.BlockSpec((batch, None, chunk, D_MODEL), lambda c: (0, c, 0, 0)),
                  pl.BlockSpec((1, D_MODEL), lambda c: (0, 0))],
        out_specs=pl.BlockSpec((N_GROUPS, None, batch, SSM_GROUP, chunk), lambda c: (0, c, 0, 0, 0)),
        out_shape=jax.ShapeDtypeStruct((N_GROUPS, n_chunks, batch, SSM_GROUP, chunk), F32),
        compiler_params=_params("parallel"),
        name="ssm_pre_norm",
    )(h4, gain)


def _ssm_kernel(u_ref, e_ref, e_next_ref, pin_ref, bb_ref, pout_ref, cc_ref, al_ref, y_ref,
                t_even_ref, t_odd_ref, zs_ref, sfr_ref, sbr_ref, sfi_ref, sbi_ref):
    n_chunks, batch, _, chunk = u_ref.shape
    rows, width = n_chunks * batch, SSM_GROUP * chunk
    p = SSM_STATE
    assert chunk == LANES

    upper = (lax.broadcasted_iota(jnp.int32, (chunk, chunk), 1)
             >= lax.broadcasted_iota(jnp.int32, (chunk, chunk), 0))

    def build_t(lag_ref, t_ref):
        for h in range(SSM_GROUP):
            for ho in range(SSM_GROUP):
                k_bwd = jnp.broadcast_to(lag_ref[h, ho:ho + 1, 0:chunk], (chunk, chunk))
                k_fwd = jnp.broadcast_to(lag_ref[h, ho:ho + 1, chunk:2 * chunk], (chunk, chunk))
                t_fwd = pltpu.roll(k_fwd, 0, 1, stride=1, stride_axis=0)
                t_bwd = pltpu.roll(k_bwd, 0, 1, stride=1, stride_axis=0)
                t_ref[chunk * h:chunk * (h + 1), chunk * ho:chunk * (ho + 1)] = (
                    jnp.where(upper, t_fwd, t_bwd).astype(BF16))

    group = pl.program_id(0)

    @pl.when(group == 0)
    def _():
        build_t(e_ref, t_even_ref)

    def outer(pow_ref, vec_ref):
        ar_, ai_ = pow_ref[0][None], pow_ref[1][None]
        vr_, vi_ = vec_ref[0][:, None, :], vec_ref[1][:, None, :]
        return ((ar_ * vr_ - ai_ * vi_).reshape(width, 2 * p),
                (ar_ * vi_ + ai_ * vr_).reshape(width, 2 * p))

    wi_re, wi_im = outer(pin_ref, bb_ref)
    w_in = jnp.concatenate([wi_re, wi_im], axis=1).astype(BF16)
    wo_re, wo_im = outer(pout_ref, cc_ref)
    w_out_t = jnp.concatenate([wo_re, -wo_im], axis=1).astype(BF16)

    u = jnp.concatenate([u_ref[:, :, h, :].reshape(rows, chunk) for h in range(SSM_GROUP)],
                        axis=1).astype(BF16)
    zs_ref[...] = _dot(u, w_in)

    ar = al_ref[0:1, :]
    ai = al_ref[1:2, :]
    is_fwd = lax.broadcasted_iota(jnp.int32, (batch, 2 * p), 1) < p

    def step(kk, carry):
        xr, xi = carry
        rf = pl.multiple_of(kk * batch, batch)
        rb = pl.multiple_of((n_chunks - 1 - kk) * batch, batch)
        lr = jnp.where(is_fwd, zs_ref[pl.ds(rf, batch), 0:2 * p], zs_ref[pl.ds(rb, batch), 0:2 * p])
        li = jnp.where(is_fwd, zs_ref[pl.ds(rf, batch), 2 * p:4 * p],
                       zs_ref[pl.ds(rb, batch), 2 * p:4 * p])
        sfr_ref[pl.ds(rf, batch), :] = xr
        sbr_ref[pl.ds(rb, batch), :] = xr
        sfi_ref[pl.ds(rf, batch), :] = xi
        sbi_ref[pl.ds(rb, batch), :] = xi
        return ar * xr - ai * xi + lr, ar * xi + ai * xr + li

    zero = jnp.zeros((batch, 2 * p), F32)
    lax.fori_loop(0, n_chunks, step, (zero, zero))

    fwd_lane = lax.broadcasted_iota(jnp.int32, (rows, 2 * p), 1) < p
    xin = jnp.concatenate(
        [jnp.where(fwd_lane, sfr_ref[...], sbr_ref[...]),
         jnp.where(fwd_lane, sfi_ref[...], sbi_ref[...])], axis=1).astype(BF16)
    carry_out = lax.dot_general(xin, w_out_t, (((1,), (1,)), ((), ())), preferred_element_type=F32)

    def finish(t_ref, t_next_ref):
        build_t(e_next_ref, t_next_ref)
        y = _dot(u, t_ref[...]) + carry_out
        for h in range(SSM_GROUP):
            y_ref[h] = y[:, chunk * h:chunk * (h + 1)]

    @pl.when(group % 2 == 0)
    def _():
        finish(t_even_ref, t_odd_ref)

    @pl.when(group % 2 == 1)
    def _():
        finish(t_odd_ref, t_even_ref)


def _ssm(ug, e_rows, pow_in, b_bar, pow_out, c_mat, a_chunk):
    g, n_chunks, batch, _, chunk = ug.shape
    rows, width = n_chunks * batch, SSM_GROUP * chunk
    p = SSM_STATE
    grp = lambda i: (i, 0, 0)
    grp4 = lambda i: (i, 0, 0, 0)
    return pl.pallas_call(
        _ssm_kernel,
        grid=(g,),
        in_specs=[
            pl.BlockSpec((None, n_chunks, batch, SSM_GROUP, chunk), lambda i: (i, 0, 0, 0, 0)),
            pl.BlockSpec((None, SSM_GROUP, SSM_GROUP, 2 * chunk), grp4),
            pl.BlockSpec((None, SSM_GROUP, SSM_GROUP, 2 * chunk),
                         lambda i: (jnp.minimum(i + 1, g - 1), 0, 0, 0)),
            pl.BlockSpec((None, 2, chunk, 2 * p), grp4),
            pl.BlockSpec((None, 2, SSM_GROUP, 2 * p), grp4),
            pl.BlockSpec((None, 2, chunk, 2 * p), grp4),
            pl.BlockSpec((None, 2, SSM_GROUP, 2 * p), grp4),
            pl.BlockSpec((None, 2, 2 * p), grp),
        ],
        out_specs=pl.BlockSpec((None, SSM_GROUP, rows, chunk), grp4),
        out_shape=jax.ShapeDtypeStruct((g, SSM_GROUP, rows, chunk), F32),
        scratch_shapes=[
            pltpu.VMEM((width, width), BF16),
            pltpu.VMEM((width, width), BF16),
            pltpu.VMEM((rows, 4 * p), F32),
            pltpu.VMEM((rows, 2 * p), F32),
            pltpu.VMEM((rows, 2 * p), F32),
            pltpu.VMEM((rows, 2 * p), F32),
            pltpu.VMEM((rows, 2 * p), F32),
        ],
        compiler_params=_params("arbitrary"),
        name="s5_chunked_scan",
    )(ug, e_rows, e_rows, pow_in, b_bar, pow_out, c_mat, a_chunk)


def _glu_kernel(h_ref, y_ref, g_ref, d_ref, w_ref, out_ref):
    batch, chunk, _ = h_ref.shape
    acts = []
    for b in range(batch):
        u = _rms(h_ref[b], g_ref[...])
        y = y_ref[:, :, b, :].reshape(D_MODEL, chunk).T
        acts.append(jax.nn.gelu(y + d_ref[...] * u).astype(BF16))
    z = _dot(jnp.concatenate(acts, axis=0), w_ref[...])
    for b in range(batch):
        zb = z[b * chunk:(b + 1) * chunk]
        gate = 0.5 + 0.5 * jnp.tanh(0.5 * zb[:, D_MODEL:2 * D_MODEL])
        out_ref[b] = h_ref[b] + zb[:, 0:D_MODEL] * gate


def _glu(h4, y4, gain, d_skip, w_glu):
    batch, n_chunks, chunk, _ = h4.shape
    tok = pl.BlockSpec((batch, None, chunk, D_MODEL), lambda c: (0, c, 0, 0))
    return pl.pallas_call(
        _glu_kernel,
        grid=(n_chunks,),
        in_specs=[
            tok,
            pl.BlockSpec((N_GROUPS, SSM_GROUP, batch, chunk), lambda c: (0, 0, c, 0)),
            _resident((1, D_MODEL)),
            _resident((1, D_MODEL)),
            _resident((D_MODEL, 2 * D_MODEL)),
        ],
        out_specs=tok,
        out_shape=jax.ShapeDtypeStruct(h4.shape, F32),
        compiler_params=_params("parallel"),
        name="ssm_gelu_glu",
    )(h4, y4, gain, d_skip, w_glu)


def _ssm_operands(a_re, a_im, log_dt, b_re, b_im, c_re, c_im, chunk):
    hp = lax.Precision.HIGH
    dt = jnp.exp(log_dt)[..., None]
    zr, zi = a_re * dt, a_im * dt
    t = jnp.arange(chunk + 1, dtype=F32)[None, None, :, None]
    mag = jnp.exp(zr[:, :, None, :] * t)
    ang = zi[:, :, None, :] * t
    pr, pi = mag * jnp.cos(ang), mag * jnp.sin(ang)
    nr, ni = pr[:, :, 1, :] - 1.0, pi[:, :, 1, :]
    den = a_re * a_re + a_im * a_im
    qr = (nr * a_re + ni * a_im) / den
    qi = (ni * a_re - nr * a_im) / den
    bbr = qr[..., None] * b_re - qi[..., None] * b_im
    bbi = qr[..., None] * b_im + qi[..., None] * b_re

    cp_re, cp_im = c_re.transpose(0, 1, 3, 2)[..., None], c_im.transpose(0, 1, 3, 2)[..., None]
    w_re = cp_re * bbr[:, :, :, None, :] - cp_im * bbi[:, :, :, None, :]
    w_im = cp_re * bbi[:, :, :, None, :] + cp_im * bbr[:, :, :, None, :]
    n_pairs = SSM_GROUP * SSM_GROUP
    g = a_re.shape[1]
    kern = jnp.einsum('dgtp,dgpn->dgtn',
                      jnp.concatenate([pr[:, :, :chunk], -pi[:, :, :chunk]], axis=-1),
                      jnp.concatenate([w_re, w_im], axis=2).reshape(2, g, 2 * SSM_STATE, n_pairs),
                      precision=hp).reshape(2, g, chunk, SSM_GROUP, SSM_GROUP)
    center = (kern[0, :, 0] + kern[1, :, 0])[:, None]
    lagged = jnp.concatenate([jnp.zeros_like(center), kern[1, :, :0:-1], center, kern[0, :, 1:]],
                             axis=1)
    e_rows = lagged.transpose(0, 3, 2, 1)

    def fwd_bwd(x):
        return jnp.concatenate([x[0], x[1]], axis=-1)

    def re_im(xr, xi):
        return jnp.stack([fwd_bwd(xr), fwd_bwd(xi)], axis=1)

    pow_in = re_im(jnp.stack([pr[0, :, chunk - 1::-1], pr[1, :, :chunk]]),
                   jnp.stack([pi[0, :, chunk - 1::-1], pi[1, :, :chunk]]))
    b_bar = re_im(bbr.transpose(0, 1, 3, 2), bbi.transpose(0, 1, 3, 2))
    pow_out = re_im(jnp.stack([pr[0, :, 1:], pr[1, :, :0:-1]]),
                    jnp.stack([pi[0, :, 1:], pi[1, :, :0:-1]]))
    c_mat = re_im(c_re, c_im)
    a_chunk = jnp.stack([jnp.concatenate([pr[0, :, chunk], pr[1, :, chunk]], axis=-1),
                         jnp.concatenate([pi[0, :, chunk], pi[1, :, chunk]], axis=-1)], axis=1)
    return e_rows, pow_in, b_bar, pow_out, c_mat, a_chunk


def _ssm_layer(h2, batch, seq, gain, operands, d_skip, w_glu):
    chunk = SSM_CHUNK
    n_chunks = seq // chunk
    h4 = h2.reshape(batch, n_chunks, chunk, D_MODEL)
    ug = _norm_group(h4, gain)
    y4 = _ssm(ug, *operands)
    return _glu(h4, y4, gain, d_skip, w_glu).reshape(batch * seq, D_MODEL)


def _rope_tables(seq):
    pos = jnp.arange(seq, dtype=F32)
    inv_freq = ROPE_THETA ** (-jnp.arange(0, HEAD_DIM, 2, dtype=F32) / HEAD_DIM)
    ang = pos[:, None] * inv_freq[None, :]
    cos, sin = jnp.cos(ang), jnp.sin(ang)
    reps = LANES // HEAD_DIM
    cos_t = jnp.tile(jnp.concatenate([cos, cos], axis=-1), (1, reps))
    sin_t = jnp.tile(jnp.concatenate([-sin, sin], axis=-1), (1, reps))
    return cos_t, sin_t


def _lambda_init(layer_idx):
    return 0.8 - 0.6 * math.exp(-0.3 * layer_idx)


def kernel(x, norm_mix, norm_ffn, attn_w_qkv, attn_q_gain, attn_k_gain, attn_lambda,
           attn_subln, attn_w_o, ssm_a_re, ssm_a_im, ssm_log_dt, ssm_b_re, ssm_b_im,
           ssm_c_re, ssm_c_im, ssm_d, ssm_w_glu, ffn_w_up, ffn_w_down):
    batch, seq, d = x.shape
    assert d == D_MODEL
    depth = norm_mix.shape[0]
    cos_t, sin_t = _rope_tables(seq)
    ssm_operands = jax.vmap(functools.partial(_ssm_operands, chunk=SSM_CHUNK))(
        ssm_a_re, ssm_a_im, ssm_log_dt, ssm_b_re, ssm_b_im, ssm_c_re, ssm_c_im)
    reps = LANES // HEAD_DIM
    h2 = x.reshape(batch * seq, D_MODEL)
    for i in range(depth):
        j = i // N_MIXERS
        gain = norm_mix[i][None, :]
        if i % N_MIXERS == 0:
            qg, kg = attn_q_gain[j], attn_k_gain[j]
            w_qk = attn_w_qkv[j][:, :2 * D_MODEL].astype(BF16)
            w_v_t = attn_w_qkv[j][:, 2 * D_MODEL:].T.astype(BF16)
            q, k, vt = _qkv(h2, gain, w_qk, w_v_t, jnp.tile(qg, reps)[None, :],
                            jnp.tile(kg, reps)[None, :], cos_t, sin_t, seq)
            o = _attention(q.reshape(batch, seq, D_MODEL), k.reshape(batch, seq, D_MODEL), vt, qg, kg,
                           attn_lambda[j], attn_subln[j][:, None], _lambda_init(i))
            proj = dict(attn_out=o.reshape(batch * seq, D_MODEL), w_o=attn_w_o[j].astype(BF16))
        else:
            h2 = _ssm_layer(h2, batch, seq, gain, [op[j] for op in ssm_operands],
                            ssm_d[j][None, :], ssm_w_glu[j].astype(BF16))
            proj = {}
        h2 = _mlp(h2, norm_ffn[i][None, :], ffn_w_up[i].astype(BF16), ffn_w_down[i].astype(BF16),
                  **proj)
    return h2.reshape(batch, seq, D_MODEL)
```

```python
import functools
import math

import jax
import jax.numpy as jnp
from jax import lax
from jax.experimental import pallas as pl
from jax.experimental.pallas import tpu as pltpu

F32 = jnp.float32
BF16 = jnp.bfloat16

D_MODEL = 1024
N_HEADS = 8
HEAD_DIM = 64
V_HEAD_DIM = 2 * HEAD_DIM
V_ROWS = V_HEAD_DIM + 16
ROPE_THETA = 10000.0
SSM_GROUP = 16
N_GROUPS = D_MODEL // SSM_GROUP
SSM_STATE = 64
D_FF = 4 * D_MODEL
RMS_EPS = 1e-6
Q_SCALE = HEAD_DIM ** -0.5 * math.log2(math.e)
SAFE_LOG2_SCORE = 80.0
N_MIXERS = 2

LANES = 128
MXU_DIM = 256
SSM_CHUNK = LANES
VMEM_LIMIT = 52 * 1024 * 1024


def _params(*sem):
    return pltpu.CompilerParams(dimension_semantics=sem, vmem_limit_bytes=VMEM_LIMIT)


def _rms(x, gain):
    ms = jnp.mean(x * x, axis=-1, keepdims=True)
    return x * lax.rsqrt(ms + RMS_EPS) * gain


def _dot(a, b):
    return jnp.dot(a, b, preferred_element_type=F32)


def _qkv_kernel(x_ref, g_ref, wqk_ref, wvt_ref, qg_ref, kg_ref, cos_ref, sin_ref,
                q_ref, k_ref, vt_ref):
    tm = x_ref.shape[0]
    xn = _rms(x_ref[...], g_ref[...]).astype(BF16)
    row = lax.broadcasted_iota(jnp.int32, (MXU_DIM, MXU_DIM), 0)
    col = lax.broadcasted_iota(jnp.int32, (MXU_DIM, MXU_DIM), 1)
    seg_mean = jnp.where(row // HEAD_DIM == col // HEAD_DIM, 1.0 / HEAD_DIM, 0.0).astype(BF16)
    lane = lax.broadcasted_iota(jnp.int32, (tm, LANES), 1)
    first_half = (lane % HEAD_DIM) < (HEAD_DIM // 2)
    cos = cos_ref[...]
    sin = sin_ref[...]

    def norm_rope(y, gain, scale, out_ref):
        sq = (y * y).astype(BF16)
        for t in range(D_MODEL // MXU_DIM):
            ms = _dot(sq[:, t * MXU_DIM:(t + 1) * MXU_DIM], seg_mean)
            for s in range(MXU_DIM // LANES):
                sl = slice(t * MXU_DIM + s * LANES, t * MXU_DIM + (s + 1) * LANES)
                yn = y[:, sl] * lax.rsqrt(ms[:, s * LANES:(s + 1) * LANES] + RMS_EPS) * gain
                partner = jnp.where(first_half,
                                    pltpu.roll(yn, LANES - HEAD_DIM // 2, 1),
                                    pltpu.roll(yn, HEAD_DIM // 2, 1))
                out_ref[:, sl] = ((yn * cos + partner * sin) * scale).astype(BF16)

    norm_rope(_dot(xn, wqk_ref[:, 0:D_MODEL]), qg_ref[...], Q_SCALE, q_ref)
    norm_rope(_dot(xn, wqk_ref[:, D_MODEL:2 * D_MODEL]), kg_ref[...], 1.0, k_ref)
    yvt = lax.dot_general(wvt_ref[...], xn, (((1,), (1,)), ((), ())),
                          preferred_element_type=F32)
    pad_rows = lax.broadcasted_iota(jnp.int32, (V_ROWS - V_HEAD_DIM, tm), 0)
    ones_row = jnp.where(pad_rows == 0, 1.0, 0.0).astype(BF16)
    for t in range(N_HEADS):
        vt_ref[t, 0:V_HEAD_DIM, :] = yvt[t * V_HEAD_DIM:(t + 1) * V_HEAD_DIM, :].astype(BF16)
        vt_ref[t, V_HEAD_DIM:V_ROWS, :] = ones_row


def _qkv(h2, gain, w_qk, w_v_t, q_gain, k_gain, cos_t, sin_t, seq, tm=512):
    tokens = h2.shape[0]
    tm = min(tm, seq)
    assert tokens % tm == 0 and seq % tm == 0
    n_pos_blocks = seq // tm
    tok = lambda i: (i, 0)
    full = lambda i: (0, 0)
    pos = lambda i: (i % n_pos_blocks, 0)
    vt_map = lambda i: (i // n_pos_blocks, 0, i % n_pos_blocks, 0, 0)
    out = jax.ShapeDtypeStruct((tokens, D_MODEL), BF16)
    vt_shape = (tokens // seq, N_HEADS, n_pos_blocks, V_ROWS, tm)
    return pl.pallas_call(
        _qkv_kernel,
        grid=(tokens // tm,),
        in_specs=[
            pl.BlockSpec((tm, D_MODEL), tok),
            pl.BlockSpec((1, D_MODEL), full),
            pl.BlockSpec((D_MODEL, 2 * D_MODEL), full),
            pl.BlockSpec((D_MODEL, D_MODEL), full),
            pl.BlockSpec((1, LANES), full),
            pl.BlockSpec((1, LANES), full),
            pl.BlockSpec((tm, LANES), pos),
            pl.BlockSpec((tm, LANES), pos),
        ],
        out_specs=[pl.BlockSpec((tm, D_MODEL), tok), pl.BlockSpec((tm, D_MODEL), tok),
                   pl.BlockSpec((None, N_HEADS, None, V_ROWS, tm), vt_map)],
        out_shape=[out, out, jax.ShapeDtypeStruct(vt_shape, BF16)],
        compiler_params=_params("parallel"),
        name="qkv_norm_rope",
    )(h2, gain, w_qk, w_v_t, q_gain, k_gain, cos_t, sin_t)


def _stack_q(q_ref, qs_ref):
    tq = q_ref.shape[0]
    q = q_ref[...]
    lane = lax.broadcasted_iota(jnp.int32, q.shape, 1)
    zero = jnp.zeros_like(q)
    qs_ref[0:tq, :] = jnp.where(lane < HEAD_DIM, q, zero)
    qs_ref[tq:2 * tq, :] = jnp.where(lane >= HEAD_DIM, q, zero)


def _attn_kernel(q_ref, k_ref, vt_ref, lam_ref, sg_ref, o_ref, qs_ref, acc_ref, *m_scratch,
                 lam_init, subtract_max):
    tq = q_ref.shape[0]
    n_chunks, _, tk = vt_ref.shape
    _stack_q(q_ref, qs_ref)

    def chunk(c, first):
        off = pl.multiple_of(c * tk, tk)
        st = lax.dot_general(k_ref[pl.ds(off, tk), :], qs_ref[...], (((1,), (1,)), ((), ())),
                             preferred_element_type=F32)
        if subtract_max:
            m_ref, = m_scratch
            m_new = jnp.max(st, axis=0, keepdims=True)
            if not first:
                m_prev = m_ref[...]
                m_new = jnp.maximum(m_prev, m_new)
                acc_ref[...] = acc_ref[...] * jnp.exp2(m_prev - m_new)
            m_ref[...] = m_new
            st = st - m_new
        contrib = _dot(vt_ref[c], jnp.exp2(st).astype(BF16))
        if first:
            acc_ref[...] = contrib
        else:
            acc_ref[...] += contrib

    def later_chunk(c, carry):
        chunk(c, False)
        return carry

    chunk(0, True)
    if n_chunks > 1:
        unroll = 1 if subtract_max else max(u for u in (1, 2, 3, 4, 5) if (n_chunks - 1) % u == 0)
        lax.fori_loop(1, n_chunks, later_chunk, 0, unroll=unroll)

    ot = acc_ref[0:V_HEAD_DIM, :] * (1.0 / acc_ref[V_HEAD_DIM:V_HEAD_DIM + 1, :])
    lv = lam_ref[...]
    lam = (jnp.exp(jnp.sum(lv[0:1] * lv[1:2], axis=1, keepdims=True))
           - jnp.exp(jnp.sum(lv[2:3] * lv[3:4], axis=1, keepdims=True)) + lam_init)
    od = ot[:, 0:tq] - lam * ot[:, tq:2 * tq]
    ms = jnp.mean(od * od, axis=0, keepdims=True)
    on = od * lax.rsqrt(ms + RMS_EPS) * sg_ref[...] * (1.0 - lam_init)
    o_ref[...] = on.T.astype(BF16)


def _attention_call(q, k, vt, lam_vecs, subln_col, lam_init, subtract_max, tq=4096):
    b, s, _ = q.shape
    tq = min(tq, s)
    assert s % tq == 0
    n_chunks, tk = vt.shape[2], vt.shape[4]
    qmap = lambda bi, hi, qi: (bi, qi, hi)
    full = lambda bi, hi, qi: (0, 0)
    scratch = [pltpu.VMEM((2 * tq, LANES), BF16), pltpu.VMEM((V_ROWS, 2 * tq), F32)]
    if subtract_max:
        scratch.append(pltpu.VMEM((1, 2 * tq), F32))
    return pl.pallas_call(
        functools.partial(_attn_kernel, lam_init=lam_init, subtract_max=subtract_max),
        grid=(b, N_HEADS, s // tq),
        in_specs=[
            pl.BlockSpec((None, tq, LANES), qmap),
            pl.BlockSpec((None, s, LANES), lambda bi, hi, qi: (bi, 0, hi)),
            pl.BlockSpec((None, None, n_chunks, V_ROWS, tk), lambda bi, hi, qi: (bi, hi, 0, 0, 0)),
            pl.BlockSpec((4, HEAD_DIM), full),
            pl.BlockSpec((V_HEAD_DIM, 1), full),
        ],
        out_specs=pl.BlockSpec((None, tq, LANES), qmap),
        out_shape=jax.ShapeDtypeStruct((b, s, D_MODEL), BF16),
        scratch_shapes=scratch,
        compiler_params=_params("parallel", "parallel", "parallel"),
        name="diff_attention_online" if subtract_max else "diff_attention_bounded",
    )(q, k, vt, lam_vecs, subln_col)


def _attention(q, k, vt, q_gain, k_gain, lam_vecs, subln_col, lam_init):
    score_bound = (HEAD_DIM * Q_SCALE) * jnp.max(jnp.abs(q_gain)) * jnp.max(jnp.abs(k_gain))
    return lax.cond(score_bound < SAFE_LOG2_SCORE,
                    lambda *a: _attention_call(*a, lam_init, False),
                    lambda *a: _attention_call(*a, lam_init, True),
                    q, k, vt, lam_vecs, subln_col)


MLP_FF_TILE = 1024


def _mlp_body(h, g_ref, wu_ref, wd_ref, out_ref):
    xn = _rms(h, g_ref[...]).astype(BF16)
    acc = h
    for f in range(D_FF // MLP_FF_TILE):
        sl = slice(f * MLP_FF_TILE, (f + 1) * MLP_FF_TILE)
        a = jnp.maximum(_dot(xn, wu_ref[:, sl]), 0.0)
        acc = acc + _dot((a * a).astype(BF16), wd_ref[sl, :])
    out_ref[...] = acc


def _mlp_kernel(h_ref, g_ref, wu_ref, wd_ref, out_ref):
    _mlp_body(h_ref[...], g_ref, wu_ref, wd_ref, out_ref)


def _proj_mlp_kernel(h_ref, o_ref, wo_ref, g_ref, wu_ref, wd_ref, out_ref):
    _mlp_body(h_ref[...] + _dot(o_ref[...], wo_ref[...]), g_ref, wu_ref, wd_ref, out_ref)


def _resident(shape):
    return pl.BlockSpec(shape, lambda i: (0,) * len(shape), pipeline_mode=pl.Buffered(1))


def _mlp(h2, gain, w_up, w_down, attn_out=None, w_o=None, tm=1024):
    tokens = h2.shape[0]
    tok = pl.BlockSpec((tm, D_MODEL), lambda i: (i, 0))
    mlp_specs = [_resident((1, D_MODEL)), _resident((D_MODEL, D_FF)), _resident((D_FF, D_MODEL))]
    if attn_out is None:
        body, name = _mlp_kernel, "relu2_mlp"
        in_specs, args = [tok] + mlp_specs, (h2, gain, w_up, w_down)
    else:
        body, name = _proj_mlp_kernel, "attn_proj_relu2_mlp"
        in_specs = [tok, tok, _resident((D_MODEL, D_MODEL))] + mlp_specs
        args = (h2, attn_out, w_o, gain, w_up, w_down)
    return pl.pallas_call(
        body,
        grid=(tokens // tm,),
        in_specs=in_specs,
        out_specs=tok,
        out_shape=jax.ShapeDtypeStruct((tokens, D_MODEL), F32),
        compiler_params=_params("parallel"),
        name=name,
    )(*args)


def _norm_group_kernel(h_ref, g_ref, u_ref):
    chunk = h_ref.shape[1]
    for b in range(h_ref.shape[0]):
        ut = _rms(h_ref[b], g_ref[...]).T
        u_ref[:, b] = ut.reshape(N_GROUPS, SSM_GROUP, chunk)


def _norm_group(h4, gain):
    batch, n_chunks, chunk, _ = h4.shape
    return pl.pallas_call(
        _norm_group_kernel,
        grid=(n_chunks,),
        in_specs=[pl.BlockSpec((batch, None, chunk, D_MODEL), lambda c: (0, c, 0, 0)),
                  pl.BlockSpec((1, D_MODEL), lambda c: (0, 0))],
        out_specs=pl.BlockSpec((N_GROUPS, None, batch, SSM_GROUP, chunk), lambda c: (0, c, 0, 0, 0)),
        out_shape=jax.ShapeDtypeStruct((N_GROUPS, n_chunks, batch, SSM_GROUP, chunk), F32),
        compiler_params=_params("parallel"),
        name="ssm_pre_norm",
    )(h4, gain)


def _ssm_kernel(u_ref, e_ref, e_next_ref, pin_ref, bb_ref, pout_ref, cc_ref, al_ref, y_ref,
                t_even_ref, t_odd_ref, zs_ref, sfr_ref, sbr_ref, sfi_ref, sbi_ref):
    n_chunks, batch, _, chunk = u_ref.shape
    rows, width = n_chunks * batch, SSM_GROUP * chunk
    p = SSM_STATE
    assert chunk == LANES

    upper = (lax.broadcasted_iota(jnp.int32, (chunk, chunk), 1)
             >= lax.broadcasted_iota(jnp.int32, (chunk, chunk), 0))

    def build_t(lag_ref, t_ref):
        for h in range(SSM_GROUP):
            for ho in range(SSM_GROUP):
                k_bwd = jnp.broadcast_to(lag_ref[h, ho:ho + 1, 0:chunk], (chunk, chunk))
                k_fwd = jnp.broadcast_to(lag_ref[h, ho:ho + 1, chunk:2 * chunk], (chunk, chunk))
                t_fwd = pltpu.roll(k_fwd, 0, 1, stride=1, stride_axis=0)
                t_bwd = pltpu.roll(k_bwd, 0, 1, stride=1, stride_axis=0)
                t_ref[chunk * h:chunk * (h + 1), chunk * ho:chunk * (ho + 1)] = (
                    jnp.where(upper, t_fwd, t_bwd).astype(BF16))

    group = pl.program_id(0)

    @pl.when(group == 0)
    def _():
        build_t(e_ref, t_even_ref)

    def outer(pow_ref, vec_ref):
        ar_, ai_ = pow_ref[0][None], pow_ref[1][None]
        vr_, vi_ = vec_ref[0][:, None, :], vec_ref[1][:, None, :]
        return ((ar_ * vr_ - ai_ * vi_).reshape(width, 2 * p),
                (ar_ * vi_ + ai_ * vr_).reshape(width, 2 * p))

    wi_re, wi_im = outer(pin_ref, bb_ref)
    w_in = jnp.concatenate([wi_re, wi_im], axis=1).astype(BF16)
    wo_re, wo_im = outer(pout_ref, cc_ref)
    w_out_t = jnp.concatenate([wo_re, -wo_im], axis=1).astype(BF16)

    u = jnp.concatenate([u_ref[:, :, h, :].reshape(rows, chunk) for h in range(SSM_GROUP)],
                        axis=1).astype(BF16)
    zs_ref[...] = _dot(u, w_in)

    ar = al_ref[0:1, :]
    ai = al_ref[1:2, :]
    is_fwd = lax.broadcasted_iota(jnp.int32, (batch, 2 * p), 1) < p

    def step(kk, carry):
        xr, xi = carry
        rf = pl.multiple_of(kk * batch, batch)
        rb = pl.multiple_of((n_chunks - 1 - kk) * batch, batch)
        lr = jnp.where(is_fwd, zs_ref[pl.ds(rf, batch), 0:2 * p], zs_ref[pl.ds(rb, batch), 0:2 * p])
        li = jnp.where(is_fwd, zs_ref[pl.ds(rf, batch), 2 * p:4 * p],
                       zs_ref[pl.ds(rb, batch), 2 * p:4 * p])
        sfr_ref[pl.ds(rf, batch), :] = xr
        sbr_ref[pl.ds(rb, batch), :] = xr
        sfi_ref[pl.ds(rf, batch), :] = xi
        sbi_ref[pl.ds(rb, batch), :] = xi
        return ar * xr - ai * xi + lr, ar * xi + ai * xr + li

    zero = jnp.zeros((batch, 2 * p), F32)
    lax.fori_loop(0, n_chunks, step, (zero, zero))

    fwd_lane = lax.broadcasted_iota(jnp.int32, (rows, 2 * p), 1) < p
    xin = jnp.concatenate(
        [jnp.where(fwd_lane, sfr_ref[...], sbr_ref[...]),
         jnp.where(fwd_lane, sfi_ref[...], sbi_ref[...])], axis=1).astype(BF16)
    carry_out = lax.dot_general(xin, w_out_t, (((1,), (1,)), ((), ())), preferred_element_type=F32)

    def finish(t_ref, t_next_ref):
        build_t(e_next_ref, t_next_ref)
        y = _dot(u, t_ref[...]) + carry_out
        for h in range(SSM_GROUP):
            y_ref[h] = y[:, chunk * h:chunk * (h + 1)]

    @pl.when(group % 2 == 0)
    def _():
        finish(t_even_ref, t_odd_ref)

    @pl.when(group % 2 == 1)
    def _():
        finish(t_odd_ref, t_even_ref)


def _ssm(ug, e_rows, pow_in, b_bar, pow_out, c_mat, a_chunk):
    g, n_chunks, batch, _, chunk = ug.shape
    rows, width = n_chunks * batch, SSM_GROUP * chunk
    p = SSM_STATE
    grp = lambda i: (i, 0, 0)
    grp4 = lambda i: (i, 0, 0, 0)
    return pl.pallas_call(
        _ssm_kernel,
        grid=(g,),
        in_specs=[
            pl.BlockSpec((None, n_chunks, batch, SSM_GROUP, chunk), lambda i: (i, 0, 0, 0, 0)),
            pl.BlockSpec((None, SSM_GROUP, SSM_GROUP, 2 * chunk), grp4),
            pl.BlockSpec((None, SSM_GROUP, SSM_GROUP, 2 * chunk),
                         lambda i: (jnp.minimum(i + 1, g - 1), 0, 0, 0)),
            pl.BlockSpec((None, 2, chunk, 2 * p), grp4),
            pl.BlockSpec((None, 2, SSM_GROUP, 2 * p), grp4),
            pl.BlockSpec((None, 2, chunk, 2 * p), grp4),
            pl.BlockSpec((None, 2, SSM_GROUP, 2 * p), grp4),
            pl.BlockSpec((None, 2, 2 * p), grp),
        ],
        out_specs=pl.BlockSpec((None, SSM_GROUP, rows, chunk), grp4),
        out_shape=jax.ShapeDtypeStruct((g, SSM_GROUP, rows, chunk), F32),
        scratch_shapes=[
            pltpu.VMEM((width, width), BF16),
            pltpu.VMEM((width, width), BF16),
            pltpu.VMEM((rows, 4 * p), F32),
            pltpu.VMEM((rows, 2 * p), F32),
            pltpu.VMEM((rows, 2 * p), F32),
            pltpu.VMEM((rows, 2 * p), F32),
            pltpu.VMEM((rows, 2 * p), F32),
        ],
        compiler_params=_params("arbitrary"),
        name="s5_chunked_scan",
    )(ug, e_rows, e_rows, pow_in, b_bar, pow_out, c_mat, a_chunk)


def _glu_kernel(h_ref, y_ref, g_ref, d_ref, w_ref, out_ref):
    batch, chunk, _ = h_ref.shape
    acts = []
    for b in range(batch):
        u = _rms(h_ref[b], g_ref[...])
        y = y_ref[:, :, b, :].reshape(D_MODEL, chunk).T
        acts.append(jax.nn.gelu(y + d_ref[...] * u).astype(BF16))
    z = _dot(jnp.concatenate(acts, axis=0), w_ref[...])
    for b in range(batch):
        zb = z[b * chunk:(b + 1) * chunk]
        gate = 0.5 + 0.5 * jnp.tanh(0.5 * zb[:, D_MODEL:2 * D_MODEL])
        out_ref[b] = h_ref[b] + zb[:, 0:D_MODEL] * gate


def _glu(h4, y4, gain, d_skip, w_glu):
    batch, n_chunks, chunk, _ = h4.shape
    tok = pl.BlockSpec((batch, None, chunk, D_MODEL), lambda c: (0, c, 0, 0))
    return pl.pallas_call(
        _glu_kernel,
        grid=(n_chunks,),
        in_specs=[
            tok,
            pl.BlockSpec((N_GROUPS, SSM_GROUP, batch, chunk), lambda c: (0, 0, c, 0)),
            _resident((1, D_MODEL)),
            _resident((1, D_MODEL)),
            _resident((D_MODEL, 2 * D_MODEL)),
        ],
        out_specs=tok,
        out_shape=jax.ShapeDtypeStruct(h4.shape, F32),
        compiler_params=_params("parallel"),
        name="ssm_gelu_glu",
    )(h4, y4, gain, d_skip, w_glu)


def _ssm_operands(a_re, a_im, log_dt, b_re, b_im, c_re, c_im, chunk):
    hp = lax.Precision.HIGH
    dt = jnp.exp(log_dt)[..., None]
    zr, zi = a_re * dt, a_im * dt
    t = jnp.arange(chunk + 1, dtype=F32)[None, None, :, None]
    mag = jnp.exp(zr[:, :, None, :] * t)
    ang = zi[:, :, None, :] * t
    pr, pi = mag * jnp.cos(ang), mag * jnp.sin(ang)
    nr, ni = pr[:, :, 1, :] - 1.0, pi[:, :, 1, :]
    den = a_re * a_re + a_im * a_im
    qr = (nr * a_re + ni * a_im) / den
    qi = (ni * a_re - nr * a_im) / den
    bbr = qr[..., None] * b_re - qi[..., None] * b_im
    bbi = qr[..., None] * b_im + qi[..., None] * b_re

    cp_re, cp_im = c_re.transpose(0, 1, 3, 2)[..., None], c_im.transpose(0, 1, 3, 2)[..., None]
    w_re = cp_re * bbr[:, :, :, None, :] - cp_im * bbi[:, :, :, None, :]
    w_im = cp_re * bbi[:, :, :, None, :] + cp_im * bbr[:, :, :, None, :]
    n_pairs = SSM_GROUP * SSM_GROUP
    g = a_re.shape[1]
    kern = jnp.einsum('dgtp,dgpn->dgtn',
                      jnp.concatenate([pr[:, :, :chunk], -pi[:, :, :chunk]], axis=-1),
                      jnp.concatenate([w_re, w_im], axis=2).reshape(2, g, 2 * SSM_STATE, n_pairs),
                      precision=hp).reshape(2, g, chunk, SSM_GROUP, SSM_GROUP)
    center = (kern[0, :, 0] + kern[1, :, 0])[:, None]
    lagged = jnp.concatenate([jnp.zeros_like(center), kern[1, :, :0:-1], center, kern[0, :, 1:]],
                             axis=1)
    e_rows = lagged.transpose(0, 3, 2, 1)

    def fwd_bwd(x):
        return jnp.concatenate([x[0], x[1]], axis=-1)

    def re_im(xr, xi):
        return jnp.stack([fwd_bwd(xr), fwd_bwd(xi)], axis=1)

    pow_in = re_im(jnp.stack([pr[0, :, chunk - 1::-1], pr[1, :, :chunk]]),
                   jnp.stack([pi[0, :, chunk - 1::-1], pi[1, :, :chunk]]))
    b_bar = re_im(bbr.transpose(0, 1, 3, 2), bbi.transpose(0, 1, 3, 2))
    pow_out = re_im(jnp.stack([pr[0, :, 1:], pr[1, :, :0:-1]]),
                    jnp.stack([pi[0, :, 1:], pi[1, :, :0:-1]]))
    c_mat = re_im(c_re, c_im)
    a_chunk = jnp.stack([jnp.concatenate([pr[0, :, chunk], pr[1, :, chunk]], axis=-1),
                         jnp.concatenate([pi[0, :, chunk], pi[1, :, chunk]], axis=-1)], axis=1)
    return e_rows, pow_in, b_bar, pow_out, c_mat, a_chunk


def _ssm_layer(h2, batch, seq, gain, operands, d_skip, w_glu):
    chunk = SSM_CHUNK
    n_chunks = seq // chunk
    h4 = h2.reshape(batch, n_chunks, chunk, D_MODEL)
    ug = _norm_group(h4, gain)
    y4 = _ssm(ug, *operands)
    return _glu(h4, y4, gain, d_skip, w_glu).reshape(batch * seq, D_MODEL)


def _rope_tables(seq):
    pos = jnp.arange(seq, dtype=F32)
    inv_freq = ROPE_THETA ** (-jnp.arange(0, HEAD_DIM, 2, dtype=F32) / HEAD_DIM)
    ang = pos[:, None] * inv_freq[None, :]
    cos, sin = jnp.cos(ang), jnp.sin(ang)
    reps = LANES // HEAD_DIM
    cos_t = jnp.tile(jnp.concatenate([cos, cos], axis=-1), (1, reps))
    sin_t = jnp.tile(jnp.concatenate([-sin, sin], axis=-1), (1, reps))
    return cos_t, sin_t


def _lambda_init(layer_idx):
    return 0.8 - 0.6 * math.exp(-0.3 * layer_idx)


def kernel(x, norm_mix, norm_ffn, attn_w_qkv, attn_q_gain, attn_k_gain, attn_lambda,
           attn_subln, attn_w_o, ssm_a_re, ssm_a_im, ssm_log_dt, ssm_b_re, ssm_b_im,
           ssm_c_re, ssm_c_im, ssm_d, ssm_w_glu, ffn_w_up, ffn_w_down):
    batch, seq, d = x.shape
    assert d == D_MODEL
    depth = norm_mix.shape[0]
    cos_t, sin_t = _rope_tables(seq)
    ssm_operands = jax.vmap(functools.partial(_ssm_operands, chunk=SSM_CHUNK))(
        ssm_a_re, ssm_a_im, ssm_log_dt, ssm_b_re, ssm_b_im, ssm_c_re, ssm_c_im)
    reps = LANES // HEAD_DIM
    h2 = x.reshape(batch * seq, D_MODEL)
    for i in range(depth):
        j = i // N_MIXERS
        gain = norm_mix[i][None, :]
        if i % N_MIXERS == 0:
            qg, kg = attn_q_gain[j], attn_k_gain[j]
            w_qk = attn_w_qkv[j][:, :2 * D_MODEL].astype(BF16)
            w_v_t = attn_w_qkv[j][:, 2 * D_MODEL:].T.astype(BF16)
            q, k, vt = _qkv(h2, gain, w_qk, w_v_t, jnp.tile(qg, reps)[None, :],
                            jnp.tile(kg, reps)[None, :], cos_t, sin_t, seq)
            o = _attention(q.reshape(batch, seq, D_MODEL), k.reshape(batch, seq, D_MODEL), vt, qg, kg,
                           attn_lambda[j], attn_subln[j][:, None], _lambda_init(i))
            proj = dict(attn_out=o.reshape(batch * seq, D_MODEL), w_o=attn_w_o[j].astype(BF16))
        else:
            h2 = _ssm_layer(h2, batch, seq, gain, [op[j] for op in ssm_operands],
                            ssm_d[j][None, :], ssm_w_glu[j].astype(BF16))
            proj = {}
        h2 = _mlp(h2, norm_ffn[i][None, :], ffn_w_up[i].astype(BF16), ffn_w_down[i].astype(BF16),
                  **proj)
    return h2.reshape(batch, seq, D_MODEL)
```

```python
import functools
import math

import jax
import jax.numpy as jnp
from jax import lax
from jax.experimental import pallas as pl
from jax.experimental.pallas import tpu as pltpu

F32 = jnp.float32
BF16 = jnp.bfloat16

D_MODEL = 1024
N_HEADS = 8
HEAD_DIM = 64
V_HEAD_DIM = 2 * HEAD_DIM
V_ROWS = V_HEAD_DIM + 16
ROPE_THETA = 10000.0
SSM_GROUP = 16
N_GROUPS = D_MODEL // SSM_GROUP
SSM_STATE = 64
D_FF = 4 * D_MODEL
RMS_EPS = 1e-6
Q_SCALE = HEAD_DIM ** -0.5 * math.log2(math.e)
SAFE_LOG2_SCORE = 80.0
N_MIXERS = 2

LANES = 128
MXU_DIM = 256
SSM_CHUNK = LANES
VMEM_LIMIT = 52 * 1024 * 1024


def _params(*sem):
    return pltpu.CompilerParams(dimension_semantics=sem, vmem_limit_bytes=VMEM_LIMIT)


def _rms(x, gain):
    ms = jnp.mean(x * x, axis=-1, keepdims=True)
    return x * lax.rsqrt(ms + RMS_EPS) * gain


def _dot(a, b):
    return jnp.dot(a, b, preferred_element_type=F32)


def _qkv_kernel(x_ref, g_ref, wqk_ref, wvt_ref, qg_ref, kg_ref, cos_ref, sin_ref,
                q_ref, k_ref, vt_ref):
    tm = x_ref.shape[0]
    xn = _rms(x_ref[...], g_ref[...]).astype(BF16)
    row = lax.broadcasted_iota(jnp.int32, (MXU_DIM, MXU_DIM), 0)
    col = lax.broadcasted_iota(jnp.int32, (MXU_DIM, MXU_DIM), 1)
    seg_mean = jnp.where(row // HEAD_DIM == col // HEAD_DIM, 1.0 / HEAD_DIM, 0.0).astype(BF16)
    lane = lax.broadcasted_iota(jnp.int32, (tm, LANES), 1)
    first_half = (lane % HEAD_DIM) < (HEAD_DIM // 2)
    cos = cos_ref[...]
    sin = sin_ref[...]

    def norm_rope(y, gain, scale, out_ref):
        sq = (y * y).astype(BF16)
        for t in range(D_MODEL // MXU_DIM):
            ms = _dot(sq[:, t * MXU_DIM:(t + 1) * MXU_DIM], seg_mean)
            for s in range(MXU_DIM // LANES):
                sl = slice(t * MXU_DIM + s * LANES, t * MXU_DIM + (s + 1) * LANES)
                yn = y[:, sl] * lax.rsqrt(ms[:, s * LANES:(s + 1) * LANES] + RMS_EPS) * gain
                partner = jnp.where(first_half,
                                    pltpu.roll(yn, LANES - HEAD_DIM // 2, 1),
                                    pltpu.roll(yn, HEAD_DIM // 2, 1))
                out_ref[:, sl] = ((yn * cos + partner * sin) * scale).astype(BF16)

    norm_rope(_dot(xn, wqk_ref[:, 0:D_MODEL]), qg_ref[...], Q_SCALE, q_ref)
    norm_rope(_dot(xn, wqk_ref[:, D_MODEL:2 * D_MODEL]), kg_ref[...], 1.0, k_ref)
    yvt = lax.dot_general(wvt_ref[...], xn, (((1,), (1,)), ((), ())),
                          preferred_element_type=F32)
    pad_rows = lax.broadcasted_iota(jnp.int32, (V_ROWS - V_HEAD_DIM, tm), 0)
    ones_row = jnp.where(pad_rows == 0, 1.0, 0.0).astype(BF16)
    for t in range(N_HEADS):
        vt_ref[t, 0:V_HEAD_DIM, :] = yvt[t * V_HEAD_DIM:(t + 1) * V_HEAD_DIM, :].astype(BF16)
        vt_ref[t, V_HEAD_DIM:V_ROWS, :] = ones_row


def _qkv(h2, gain, w_qk, w_v_t, q_gain, k_gain, cos_t, sin_t, seq, tm=512):
    tokens = h2.shape[0]
    tm = min(tm, seq)
    assert tokens % tm == 0 and seq % tm == 0
    n_pos_blocks = seq // tm
    tok = lambda i: (i, 0)
    full = lambda i: (0, 0)
    pos = lambda i: (i % n_pos_blocks, 0)
    vt_map = lambda i: (i // n_pos_blocks, 0, i % n_pos_blocks, 0, 0)
    out = jax.ShapeDtypeStruct((tokens, D_MODEL), BF16)
    vt_shape = (tokens // seq, N_HEADS, n_pos_blocks, V_ROWS, tm)
    return pl.pallas_call(
        _qkv_kernel,
        grid=(tokens // tm,),
        in_specs=[
            pl.BlockSpec((tm, D_MODEL), tok),
            pl.BlockSpec((1, D_MODEL), full),
            pl.BlockSpec((D_MODEL, 2 * D_MODEL), full),
            pl.BlockSpec((D_MODEL, D_MODEL), full),
            pl.BlockSpec((1, LANES), full),
            pl.BlockSpec((1, LANES), full),
            pl.BlockSpec((tm, LANES), pos),
            pl.BlockSpec((tm, LANES), pos),
        ],
        out_specs=[pl.BlockSpec((tm, D_MODEL), tok), pl.BlockSpec((tm, D_MODEL), tok),
                   pl.BlockSpec((None, N_HEADS, None, V_ROWS, tm), vt_map)],
        out_shape=[out, out, jax.ShapeDtypeStruct(vt_shape, BF16)],
        compiler_params=_params("parallel"),
        name="qkv_norm_rope",
    )(h2, gain, w_qk, w_v_t, q_gain, k_gain, cos_t, sin_t)


def _stack_q(q_ref, qs_ref):
    tq = q_ref.shape[0]
    q = q_ref[...]
    lane = lax.broadcasted_iota(jnp.int32, q.shape, 1)
    zero = jnp.zeros_like(q)
    qs_ref[0:tq, :] = jnp.where(lane < HEAD_DIM, q, zero)
    qs_ref[tq:2 * tq, :] = jnp.where(lane >= HEAD_DIM, q, zero)


def _attn_kernel(q_ref, k_ref, vt_ref, lam_ref, sg_ref, o_ref, qs_ref, acc_ref, *m_scratch,
                 lam_init, subtract_max):
    tq = q_ref.shape[0]
    n_chunks, _, tk = vt_ref.shape
    _stack_q(q_ref, qs_ref)

    def chunk(c, first):
        off = pl.multiple_of(c * tk, tk)
        st = lax.dot_general(k_ref[pl.ds(off, tk), :], qs_ref[...], (((1,), (1,)), ((), ())),
                             preferred_element_type=F32)
        if subtract_max:
            m_ref, = m_scratch
            m_new = jnp.max(st, axis=0, keepdims=True)
            if not first:
                m_prev = m_ref[...]
                m_new = jnp.maximum(m_prev, m_new)
                acc_ref[...] = acc_ref[...] * jnp.exp2(m_prev - m_new)
            m_ref[...] = m_new
            st = st - m_new
        contrib = _dot(vt_ref[c], jnp.exp2(st).astype(BF16))
        if first:
            acc_ref[...] = contrib
        else:
            acc_ref[...] += contrib

    def later_chunk(c, carry):
        chunk(c, False)
        return carry

    chunk(0, True)
    if n_chunks > 1:
        unroll = 1 if subtract_max else max(u for u in (1, 2, 3, 4, 5) if (n_chunks - 1) % u == 0)
        lax.fori_loop(1, n_chunks, later_chunk, 0, unroll=unroll)

    ot = acc_ref[0:V_HEAD_DIM, :] * (1.0 / acc_ref[V_HEAD_DIM:V_HEAD_DIM + 1, :])
    lv = lam_ref[...]
    lam = (jnp.exp(jnp.sum(lv[0:1] * lv[1:2], axis=1, keepdims=True))
           - jnp.exp(jnp.sum(lv[2:3] * lv[3:4], axis=1, keepdims=True)) + lam_init)
    od = ot[:, 0:tq] - lam * ot[:, tq:2 * tq]
    ms = jnp.mean(od * od, axis=0, keepdims=True)
    on = od * lax.rsqrt(ms + RMS_EPS) * sg_ref[...] * (1.0 - lam_init)
    o_ref[...] = on.T.astype(BF16)


def _attention_call(q, k, vt, lam_vecs, subln_col, lam_init, subtract_max, tq=4096):
    b, s, _ = q.shape
    tq = min(tq, s)
    assert s % tq == 0
    n_chunks, tk = vt.shape[2], vt.shape[4]
    qmap = lambda bi, hi, qi: (bi, qi, hi)
    full = lambda bi, hi, qi: (0, 0)
    scratch = [pltpu.VMEM((2 * tq, LANES), BF16), pltpu.VMEM((V_ROWS, 2 * tq), F32)]
    if subtract_max:
        scratch.append(pltpu.VMEM((1, 2 * tq), F32))
    return pl.pallas_call(
        functools.partial(_attn_kernel, lam_init=lam_init, subtract_max=subtract_max),
        grid=(b, N_HEADS, s // tq),
        in_specs=[
            pl.BlockSpec((None, tq, LANES), qmap),
            pl.BlockSpec((None, s, LANES), lambda bi, hi, qi: (bi, 0, hi)),
            pl.BlockSpec((None, None, n_chunks, V_ROWS, tk), lambda bi, hi, qi: (bi, hi, 0, 0, 0)),
            pl.BlockSpec((4, HEAD_DIM), full),
            pl.BlockSpec((V_HEAD_DIM, 1), full),
        ],
        out_specs=pl.BlockSpec((None, tq, LANES), qmap),
        out_shape=jax.ShapeDtypeStruct((b, s, D_MODEL), BF16),
        scratch_shapes=scratch,
        compiler_params=_params("parallel", "parallel", "parallel"),
        name="diff_attention_online" if subtract_max else "diff_attention_bounded",
    )(q, k, vt, lam_vecs, subln_col)


def _attention(q, k, vt, q_gain, k_gain, lam_vecs, subln_col, lam_init):
    score_bound = (HEAD_DIM * Q_SCALE) * jnp.max(jnp.abs(q_gain)) * jnp.max(jnp.abs(k_gain))
    return lax.cond(score_bound < SAFE_LOG2_SCORE,
                    lambda *a: _attention_call(*a, lam_init, False),
                    lambda *a: _attention_call(*a, lam_init, True),
                    q, k, vt, lam_vecs, subln_col)


MLP_FF_TILE = 1024


def _mlp_body(h, g_ref, wu_ref, wd_ref, out_ref):
    xn = _rms(h, g_ref[...]).astype(BF16)
    acc = h
    for f in range(D_FF // MLP_FF_TILE):
        sl = slice(f * MLP_FF_TILE, (f + 1) * MLP_FF_TILE)
        a = jnp.maximum(_dot(xn, wu_ref[:, sl]), 0.0)
        acc = acc + _dot((a * a).astype(BF16), wd_ref[sl, :])
    out_ref[...] = acc


def _mlp_kernel(h_ref, g_ref, wu_ref, wd_ref, out_ref):
    _mlp_body(h_ref[...], g_ref, wu_ref, wd_ref, out_ref)


def _proj_mlp_kernel(h_ref, o_ref, wo_ref, g_ref, wu_ref, wd_ref, out_ref):
    _mlp_body(h_ref[...] + _dot(o_ref[...], wo_ref[...]), g_ref, wu_ref, wd_ref, out_ref)


def _resident(shape):
    return pl.BlockSpec(shape, lambda i: (0,) * len(shape), pipeline_mode=pl.Buffered(1))


def _mlp(h2, gain, w_up, w_down, attn_out=None, w_o=None, tm=1024):
    tokens = h2.shape[0]
    tok = pl.BlockSpec((tm, D_MODEL), lambda i: (i, 0))
    mlp_specs = [_resident((1, D_MODEL)), _resident((D_MODEL, D_FF)), _resident((D_FF, D_MODEL))]
    if attn_out is None:
        body, name = _mlp_kernel, "relu2_mlp"
        in_specs, args = [tok] + mlp_specs, (h2, gain, w_up, w_down)
    else:
        body, name = _proj_mlp_kernel, "attn_proj_relu2_mlp"
        in_specs = [tok, tok, _resident((D_MODEL, D_MODEL))] + mlp_specs
        args = (h2, attn_out, w_o, gain, w_up, w_down)
    return pl.pallas_call(
        body,
        grid=(tokens // tm,),
        in_specs=in_specs,
        out_specs=tok,
        out_shape=jax.ShapeDtypeStruct((tokens, D_MODEL), F32),
        compiler_params=_params("parallel"),
        name=name,
    )(*args)


def _norm_group_kernel(h_ref, g_ref, u_ref):
    chunk = h_ref.shape[1]
    for b in range(h_ref.shape[0]):
        ut = _rms(h_ref[b], g_ref[...]).T
        u_ref[:, b] = ut.reshape(N_GROUPS, SSM_GROUP, chunk)


def _norm_group(h4, gain):
    batch, n_chunks, chunk, _ = h4.shape
    return pl.pallas_call(
        _norm_group_kernel,
        grid=(n_chunks,),
        in_specs=[pl.BlockSpec((batch, None, chunk, D_MODEL), lambda c: (0, c, 0, 0)),
                  pl.BlockSpec((1, D_MODEL), lambda c: (0, 0))],
        out_specs=pl.BlockSpec((N_GROUPS, None, batch, SSM_GROUP, chunk), lambda c: (0, c, 0, 0, 0)),
        out_shape=jax.ShapeDtypeStruct((N_GROUPS, n_chunks, batch, SSM_GROUP, chunk), F32),
        compiler_params=_params("parallel"),
        name="ssm_pre_norm",
    )(h4, gain)


def _ssm_kernel(u_ref, e_ref, e_next_ref, pin_ref, bb_ref, pout_ref, cc_ref, al_ref, y_ref,
                t_even_ref, t_odd_ref, zs_ref, sfr_ref, sbr_ref, sfi_ref, sbi_ref):
    n_chunks, batch, _, chunk = u_ref.shape
    rows, width = n_chunks * batch, SSM_GROUP * chunk
    p = SSM_STATE
    assert chunk == LANES

    upper = (lax.broadcasted_iota(jnp.int32, (chunk, chunk), 1)
             >= lax.broadcasted_iota(jnp.int32, (chunk, chunk), 0))

    def build_t(lag_ref, t_ref):
        for h in range(SSM_GROUP):
            for ho in range(SSM_GROUP):
                k_bwd = jnp.broadcast_to(lag_ref[h, ho:ho + 1, 0:chunk], (chunk, chunk))
                k_fwd = jnp.broadcast_to(lag_ref[h, ho:ho + 1, chunk:2 * chunk], (chunk, chunk))
                t_fwd = pltpu.roll(k_fwd, 0, 1, stride=1, stride_axis=0)
                t_bwd = pltpu.roll(k_bwd, 0, 1, stride=1, stride_axis=0)
                t_ref[chunk * h:chunk * (h + 1), chunk * ho:chunk * (ho + 1)] = (
                    jnp.where(upper, t_fwd, t_bwd).astype(BF16))

    group = pl.program_id(0)

    @pl.when(group == 0)
    def _():
        build_t(e_ref, t_even_ref)

    def outer(pow_ref, vec_ref):
        ar_, ai_ = pow_ref[0][None], pow_ref[1][None]
        vr_, vi_ = vec_ref[0][:, None, :], vec_ref[1][:, None, :]
        return ((ar_ * vr_ - ai_ * vi_).reshape(width, 2 * p),
                (ar_ * vi_ + ai_ * vr_).reshape(width, 2 * p))

    wi_re, wi_im = outer(pin_ref, bb_ref)
    w_in = jnp.concatenate([wi_re, wi_im], axis=1).astype(BF16)
    wo_re, wo_im = outer(pout_ref, cc_ref)
    w_out_t = jnp.concatenate([wo_re, -wo_im], axis=1).astype(BF16)

    u = jnp.concatenate([u_ref[:, :, h, :].reshape(rows, chunk) for h in range(SSM_GROUP)],
                        axis=1).astype(BF16)
    zs_ref[...] = _dot(u, w_in)

    ar = al_ref[0:1, :]
    ai = al_ref[1:2, :]
    is_fwd = lax.broadcasted_iota(jnp.int32, (batch, 2 * p), 1) < p

    def step(kk, carry):
        xr, xi = carry
        rf = pl.multiple_of(kk * batch, batch)
        rb = pl.multiple_of((n_chunks - 1 - kk) * batch, batch)
        lr = jnp.where(is_fwd, zs_ref[pl.ds(rf, batch), 0:2 * p], zs_ref[pl.ds(rb, batch), 0:2 * p])
        li = jnp.where(is_fwd, zs_ref[pl.ds(rf, batch), 2 * p:4 * p],
                       zs_ref[pl.ds(rb, batch), 2 * p:4 * p])
        sfr_ref[pl.ds(rf, batch), :] = xr
        sbr_ref[pl.ds(rb, batch), :] = xr
        sfi_ref[pl.ds(rf, batch), :] = xi
        sbi_ref[pl.ds(rb, batch), :] = xi
        return ar * xr - ai * xi + lr, ar * xi + ai * xr + li

    zero = jnp.zeros((batch, 2 * p), F32)
    lax.fori_loop(0, n_chunks, step, (zero, zero))

    fwd_lane = lax.broadcasted_iota(jnp.int32, (rows, 2 * p), 1) < p
    xin = jnp.concatenate(
        [jnp.where(fwd_lane, sfr_ref[...], sbr_ref[...]),
         jnp.where(fwd_lane, sfi_ref[...], sbi_ref[...])], axis=1).astype(BF16)
    carry_out = lax.dot_general(xin, w_out_t, (((1,), (1,)), ((), ())), preferred_element_type=F32)

    def finish(t_ref, t_next_ref):
        build_t(e_next_ref, t_next_ref)
        y = _dot(u, t_ref[...]) + carry_out
        for h in range(SSM_GROUP):
            y_ref[h] = y[:, chunk * h:chunk * (h + 1)]

    @pl.when(group % 2 == 0)
    def _():
        finish(t_even_ref, t_odd_ref)

    @pl.when(group % 2 == 1)
    def _():
        finish(t_odd_ref, t_even_ref)


def _ssm(ug, e_rows, pow_in, b_bar, pow_out, c_mat, a_chunk):
    g, n_chunks, batch, _, chunk = ug.shape
    rows, width = n_chunks * batch, SSM_GROUP * chunk
    p = SSM_STATE
    grp = lambda i: (i, 0, 0)
    grp4 = lambda i: (i, 0, 0, 0)
    return pl.pallas_call(
        _ssm_kernel,
        grid=(g,),
        in_specs=[
            pl.BlockSpec((None, n_chunks, batch, SSM_GROUP, chunk), lambda i: (i, 0, 0, 0, 0)),
            pl.BlockSpec((None, SSM_GROUP, SSM_GROUP, 2 * chunk), grp4),
            pl.BlockSpec((None, SSM_GROUP, SSM_GROUP, 2 * chunk),
                         lambda i: (jnp.minimum(i + 1, g - 1), 0, 0, 0)),
            pl.BlockSpec((None, 2, chunk, 2 * p), grp4),
            pl.BlockSpec((None, 2, SSM_GROUP, 2 * p), grp4),
            pl.BlockSpec((None, 2, chunk, 2 * p), grp4),
            pl.BlockSpec((None, 2, SSM_GROUP, 2 * p), grp4),
            pl.BlockSpec((None, 2, 2 * p), grp),
        ],
        out_specs=pl.BlockSpec((None, SSM_GROUP, rows, chunk), grp4),
        out_shape=jax.ShapeDtypeStruct((g, SSM_GROUP, rows, chunk), F32),
        scratch_shapes=[
            pltpu.VMEM((width, width), BF16),
            pltpu.VMEM((width, width), BF16),
            pltpu.VMEM((rows, 4 * p), F32),
            pltpu.VMEM((rows, 2 * p), F32),
            pltpu.VMEM((rows, 2 * p), F32),
            pltpu.VMEM((rows, 2 * p), F32),
            pltpu.VMEM((rows, 2 * p), F32),
        ],
        compiler_params=_params("arbitrary"),
        name="s5_chunked_scan",
    )(ug, e_rows, e_rows, pow_in, b_bar, pow_out, c_mat, a_chunk)


def _glu_kernel(h_ref, y_ref, g_ref, d_ref, w_ref, out_ref):
    batch, chunk, _ = h_ref.shape
    acts = []
    y_by_batch = jnp.swapaxes(y_ref[...].reshape(D_MODEL, batch, chunk), 0, 1)
    for b in range(batch):
        u = _rms(h_ref[b], g_ref[...])
        y = y_by_batch[b].T
        acts.append(jax.nn.gelu(y + d_ref[...] * u).astype(BF16))
    z = _dot(jnp.concatenate(acts, axis=0), w_ref[...])
    for b in range(batch):
        zb = z[b * chunk:(b + 1) * chunk]
        gate = 0.5 + 0.5 * jnp.tanh(0.5 * zb[:, D_MODEL:2 * D_MODEL])
        out_ref[b] = h_ref[b] + zb[:, 0:D_MODEL] * gate


def _glu(h4, y4, gain, d_skip, w_glu):
    batch, n_chunks, chunk, _ = h4.shape
    tok = pl.BlockSpec((batch, None, chunk, D_MODEL), lambda c: (0, c, 0, 0))
    return pl.pallas_call(
        _glu_kernel,
        grid=(n_chunks,),
        in_specs=[
            tok,
            pl.BlockSpec((N_GROUPS, SSM_GROUP, batch, chunk), lambda c: (0, 0, c, 0)),
            _resident((1, D_MODEL)),
            _resident((1, D_MODEL)),
            _resident((D_MODEL, 2 * D_MODEL)),
        ],
        out_specs=tok,
        out_shape=jax.ShapeDtypeStruct(h4.shape, F32),
        compiler_params=_params("parallel"),
        name="ssm_gelu_glu",
    )(h4, y4, gain, d_skip, w_glu)


def _ssm_operands(a_re, a_im, log_dt, b_re, b_im, c_re, c_im, chunk):
    hp = lax.Precision.HIGH
    dt = jnp.exp(log_dt)[..., None]
    zr, zi = a_re * dt, a_im * dt
    t = jnp.arange(chunk + 1, dtype=F32)[None, None, :, None]
    mag = jnp.exp(zr[:, :, None, :] * t)
    ang = zi[:, :, None, :] * t
    pr, pi = mag * jnp.cos(ang), mag * jnp.sin(ang)
    nr, ni = pr[:, :, 1, :] - 1.0, pi[:, :, 1, :]
    den = a_re * a_re + a_im * a_im
    qr = (nr * a_re + ni * a_im) / den
    qi = (ni * a_re - nr * a_im) / den
    bbr = qr[..., None] * b_re - qi[..., None] * b_im
    bbi = qr[..., None] * b_im + qi[..., None] * b_re

    cp_re, cp_im = c_re.transpose(0, 1, 3, 2)[..., None], c_im.transpose(0, 1, 3, 2)[..., None]
    w_re = cp_re * bbr[:, :, :, None, :] - cp_im * bbi[:, :, :, None, :]
    w_im = cp_re * bbi[:, :, :, None, :] + cp_im * bbr[:, :, :, None, :]
    n_pairs = SSM_GROUP * SSM_GROUP
    g = a_re.shape[1]
    kern = jnp.einsum('dgtp,dgpn->dgtn',
                      jnp.concatenate([pr[:, :, :chunk], -pi[:, :, :chunk]], axis=-1),
                      jnp.concatenate([w_re, w_im], axis=2).reshape(2, g, 2 * SSM_STATE, n_pairs),
                      precision=hp).reshape(2, g, chunk, SSM_GROUP, SSM_GROUP)
    center = (kern[0, :, 0] + kern[1, :, 0])[:, None]
    lagged = jnp.concatenate([jnp.zeros_like(center), kern[1, :, :0:-1], center, kern[0, :, 1:]],
                             axis=1)
    e_rows = lagged.transpose(0, 3, 2, 1)

    def fwd_bwd(x):
        return jnp.concatenate([x[0], x[1]], axis=-1)

    def re_im(xr, xi):
        return jnp.stack([fwd_bwd(xr), fwd_bwd(xi)], axis=1)

    pow_in = re_im(jnp.stack([pr[0, :, chunk - 1::-1], pr[1, :, :chunk]]),
                   jnp.stack([pi[0, :, chunk - 1::-1], pi[1, :, :chunk]]))
    b_bar = re_im(bbr.transpose(0, 1, 3, 2), bbi.transpose(0, 1, 3, 2))
    pow_out = re_im(jnp.stack([pr[0, :, 1:], pr[1, :, :0:-1]]),
                    jnp.stack([pi[0, :, 1:], pi[1, :, :0:-1]]))
    c_mat = re_im(c_re, c_im)
    a_chunk = jnp.stack([jnp.concatenate([pr[0, :, chunk], pr[1, :, chunk]], axis=-1),
                         jnp.concatenate([pi[0, :, chunk], pi[1, :, chunk]], axis=-1)], axis=1)
    return e_rows, pow_in, b_bar, pow_out, c_mat, a_chunk


def _ssm_layer(h2, batch, seq, gain, operands, d_skip, w_glu):
    chunk = SSM_CHUNK
    n_chunks = seq // chunk
    h4 = h2.reshape(batch, n_chunks, chunk, D_MODEL)
    ug = _norm_group(h4, gain)
    y4 = _ssm(ug, *operands)
    return _glu(h4, y4, gain, d_skip, w_glu).reshape(batch * seq, D_MODEL)


def _rope_tables(seq):
    pos = jnp.arange(seq, dtype=F32)
    inv_freq = ROPE_THETA ** (-jnp.arange(0, HEAD_DIM, 2, dtype=F32) / HEAD_DIM)
    ang = pos[:, None] * inv_freq[None, :]
    cos, sin = jnp.cos(ang), jnp.sin(ang)
    reps = LANES // HEAD_DIM
    cos_t = jnp.tile(jnp.concatenate([cos, cos], axis=-1), (1, reps))
    sin_t = jnp.tile(jnp.concatenate([-sin, sin], axis=-1), (1, reps))
    return cos_t, sin_t


def _lambda_init(layer_idx):
    return 0.8 - 0.6 * math.exp(-0.3 * layer_idx)


def kernel(x, norm_mix, norm_ffn, attn_w_qkv, attn_q_gain, attn_k_gain, attn_lambda,
           attn_subln, attn_w_o, ssm_a_re, ssm_a_im, ssm_log_dt, ssm_b_re, ssm_b_im,
           ssm_c_re, ssm_c_im, ssm_d, ssm_w_glu, ffn_w_up, ffn_w_down):
    batch, seq, d = x.shape
    assert d == D_MODEL
    depth = norm_mix.shape[0]
    cos_t, sin_t = _rope_tables(seq)
    ssm_operands = jax.vmap(functools.partial(_ssm_operands, chunk=SSM_CHUNK))(
        ssm_a_re, ssm_a_im, ssm_log_dt, ssm_b_re, ssm_b_im, ssm_c_re, ssm_c_im)
    reps = LANES // HEAD_DIM
    h2 = x.reshape(batch * seq, D_MODEL)
    for i in range(depth):
        j = i // N_MIXERS
        gain = norm_mix[i][None, :]
        if i % N_MIXERS == 0:
            qg, kg = attn_q_gain[j], attn_k_gain[j]
            w_qk = attn_w_qkv[j][:, :2 * D_MODEL].astype(BF16)
            w_v_t = attn_w_qkv[j][:, 2 * D_MODEL:].T.astype(BF16)
            q, k, vt = _qkv(h2, gain, w_qk, w_v_t, jnp.tile(qg, reps)[None, :],
                            jnp.tile(kg, reps)[None, :], cos_t, sin_t, seq)
            o = _attention(q.reshape(batch, seq, D_MODEL), k.reshape(batch, seq, D_MODEL), vt, qg, kg,
                           attn_lambda[j], attn_subln[j][:, None], _lambda_init(i))
            proj = dict(attn_out=o.reshape(batch * seq, D_MODEL), w_o=attn_w_o[j].astype(BF16))
        else:
            h2 = _ssm_layer(h2, batch, seq, gain, [op[j] for op in ssm_operands],
                            ssm_d[j][None, :], ssm_w_glu[j].astype(BF16))
            proj = {}
        h2 = _mlp(h2, norm_ffn[i][None, :], ffn_w_up[i].astype(BF16), ffn_w_down[i].astype(BF16),
                  **proj)
    return h2.reshape(batch, seq, D_MODEL)
```

```python
import functools
import math

import jax
import jax.numpy as jnp
from jax import lax
from jax.experimental import pallas as pl
from jax.experimental.pallas import tpu as pltpu

F32 = jnp.float32
BF16 = jnp.bfloat16

D_MODEL = 1024
N_HEADS = 8
HEAD_DIM = 64
V_HEAD_DIM = 2 * HEAD_DIM
V_ROWS = V_HEAD_DIM + 16
ROPE_THETA = 10000.0
SSM_GROUP = 16
N_GROUPS = D_MODEL // SSM_GROUP
SSM_STATE = 64
D_FF = 4 * D_MODEL
RMS_EPS = 1e-6
Q_SCALE = HEAD_DIM ** -0.5 * math.log2(math.e)
SAFE_LOG2_SCORE = 80.0
N_MIXERS = 2

LANES = 128
MXU_DIM = 256
SSM_CHUNK = LANES
VMEM_LIMIT = 52 * 1024 * 1024


def _params(*sem):
    return pltpu.CompilerParams(dimension_semantics=sem, vmem_limit_bytes=VMEM_LIMIT)


def _rms(x, gain):
    ms = jnp.mean(x * x, axis=-1, keepdims=True)
    return x * lax.rsqrt(ms + RMS_EPS) * gain


def _dot(a, b):
    return jnp.dot(a, b, preferred_element_type=F32)


def _qkv_kernel(x_ref, g_ref, wqk_ref, wvt_ref, qg_ref, kg_ref, cos_ref, sin_ref,
                q_ref, k_ref, vt_ref):
    tm = x_ref.shape[0]
    xn = _rms(x_ref[...], g_ref[...]).astype(BF16)
    row = lax.broadcasted_iota(jnp.int32, (MXU_DIM, MXU_DIM), 0)
    col = lax.broadcasted_iota(jnp.int32, (MXU_DIM, MXU_DIM), 1)
    seg_mean = jnp.where(row // HEAD_DIM == col // HEAD_DIM, 1.0 / HEAD_DIM, 0.0).astype(BF16)
    lane = lax.broadcasted_iota(jnp.int32, (tm, LANES), 1)
    first_half = (lane % HEAD_DIM) < (HEAD_DIM // 2)
    cos = cos_ref[...]
    sin = sin_ref[...]

    def norm_rope(y, gain, scale, out_ref):
        sq = (y * y).astype(BF16)
        for t in range(D_MODEL // MXU_DIM):
            ms = _dot(sq[:, t * MXU_DIM:(t + 1) * MXU_DIM], seg_mean)
            for s in range(MXU_DIM // LANES):
                sl = slice(t * MXU_DIM + s * LANES, t * MXU_DIM + (s + 1) * LANES)
                yn = y[:, sl] * lax.rsqrt(ms[:, s * LANES:(s + 1) * LANES] + RMS_EPS) * gain
                partner = jnp.where(first_half,
                                    pltpu.roll(yn, LANES - HEAD_DIM // 2, 1),
                                    pltpu.roll(yn, HEAD_DIM // 2, 1))
                out_ref[:, sl] = ((yn * cos + partner * sin) * scale).astype(BF16)

    norm_rope(_dot(xn, wqk_ref[:, 0:D_MODEL]), qg_ref[...], Q_SCALE, q_ref)
    norm_rope(_dot(xn, wqk_ref[:, D_MODEL:2 * D_MODEL]), kg_ref[...], 1.0, k_ref)
    yvt = lax.dot_general(wvt_ref[...], xn, (((1,), (1,)), ((), ())),
                          preferred_element_type=F32)
    pad_rows = lax.broadcasted_iota(jnp.int32, (V_ROWS - V_HEAD_DIM, tm), 0)
    ones_row = jnp.where(pad_rows == 0, 1.0, 0.0).astype(BF16)
    for t in range(N_HEADS):
        vt_ref[t, 0:V_HEAD_DIM, :] = yvt[t * V_HEAD_DIM:(t + 1) * V_HEAD_DIM, :].astype(BF16)
        vt_ref[t, V_HEAD_DIM:V_ROWS, :] = ones_row


def _qkv(h2, gain, w_qk, w_v_t, q_gain, k_gain, cos_t, sin_t, seq, tm=512):
    tokens = h2.shape[0]
    tm = min(tm, seq)
    assert tokens % tm == 0 and seq % tm == 0
    n_pos_blocks = seq // tm
    tok = lambda i: (i, 0)
    full = lambda i: (0, 0)
    pos = lambda i: (i % n_pos_blocks, 0)
    vt_map = lambda i: (i // n_pos_blocks, 0, i % n_pos_blocks, 0, 0)
    out = jax.ShapeDtypeStruct((tokens, D_MODEL), BF16)
    vt_shape = (tokens // seq, N_HEADS, n_pos_blocks, V_ROWS, tm)
    return pl.pallas_call(
        _qkv_kernel,
        grid=(tokens // tm,),
        in_specs=[
            pl.BlockSpec((tm, D_MODEL), tok),
            pl.BlockSpec((1, D_MODEL), full),
            pl.BlockSpec((D_MODEL, 2 * D_MODEL), full),
            pl.BlockSpec((D_MODEL, D_MODEL), full),
            pl.BlockSpec((1, LANES), full),
            pl.BlockSpec((1, LANES), full),
            pl.BlockSpec((tm, LANES), pos),
            pl.BlockSpec((tm, LANES), pos),
        ],
        out_specs=[pl.BlockSpec((tm, D_MODEL), tok), pl.BlockSpec((tm, D_MODEL), tok),
                   pl.BlockSpec((None, N_HEADS, None, V_ROWS, tm), vt_map)],
        out_shape=[out, out, jax.ShapeDtypeStruct(vt_shape, BF16)],
        compiler_params=_params("parallel"),
        name="qkv_norm_rope",
    )(h2, gain, w_qk, w_v_t, q_gain, k_gain, cos_t, sin_t)


def _stack_q(q_ref, qs_ref):
    tq = q_ref.shape[0]
    q = q_ref[...]
    lane = lax.broadcasted_iota(jnp.int32, q.shape, 1)
    zero = jnp.zeros_like(q)
    qs_ref[0:tq, :] = jnp.where(lane < HEAD_DIM, q, zero)
    qs_ref[tq:2 * tq, :] = jnp.where(lane >= HEAD_DIM, q, zero)


def _attn_kernel(q_ref, k_ref, vt_ref, lam_ref, sg_ref, o_ref, qs_ref, acc_ref, *m_scratch,
                 lam_init, subtract_max):
    tq = q_ref.shape[0]
    n_chunks, _, tk = vt_ref.shape
    _stack_q(q_ref, qs_ref)

    def chunk(c, first):
        off = pl.multiple_of(c * tk, tk)
        st = lax.dot_general(k_ref[pl.ds(off, tk), :], qs_ref[...], (((1,), (1,)), ((), ())),
                             preferred_element_type=F32)
        if subtract_max:
            m_ref, = m_scratch
            m_new = jnp.max(st, axis=0, keepdims=True)
            if not first:
                m_prev = m_ref[...]
                m_new = jnp.maximum(m_prev, m_new)
                acc_ref[...] = acc_ref[...] * jnp.exp2(m_prev - m_new)
            m_ref[...] = m_new
            st = st - m_new
        contrib = _dot(vt_ref[c], jnp.exp2(st).astype(BF16))
        if first:
            acc_ref[...] = contrib
        else:
            acc_ref[...] += contrib

    def later_chunk(c, carry):
        chunk(c, False)
        return carry

    chunk(0, True)
    if n_chunks > 1:
        unroll = 1 if subtract_max else max(u for u in (1, 2, 3, 4, 5) if (n_chunks - 1) % u == 0)
        lax.fori_loop(1, n_chunks, later_chunk, 0, unroll=unroll)

    ot = acc_ref[0:V_HEAD_DIM, :] * (1.0 / acc_ref[V_HEAD_DIM:V_HEAD_DIM + 1, :])
    lv = lam_ref[...]
    lam = (jnp.exp(jnp.sum(lv[0:1] * lv[1:2], axis=1, keepdims=True))
           - jnp.exp(jnp.sum(lv[2:3] * lv[3:4], axis=1, keepdims=True)) + lam_init)
    od = ot[:, 0:tq] - lam * ot[:, tq:2 * tq]
    ms = jnp.mean(od * od, axis=0, keepdims=True)
    on = od * lax.rsqrt(ms + RMS_EPS) * sg_ref[...] * (1.0 - lam_init)
    o_ref[...] = on.T.astype(BF16)


def _attention_call(q, k, vt, lam_vecs, subln_col, lam_init, subtract_max, tq=4096):
    b, s, _ = q.shape
    tq = min(tq, s)
    assert s % tq == 0
    n_chunks, tk = vt.shape[2], vt.shape[4]
    qmap = lambda bi, hi, qi: (bi, qi, hi)
    full = lambda bi, hi, qi: (0, 0)
    scratch = [pltpu.VMEM((2 * tq, LANES), BF16), pltpu.VMEM((V_ROWS, 2 * tq), F32)]
    if subtract_max:
        scratch.append(pltpu.VMEM((1, 2 * tq), F32))
    return pl.pallas_call(
        functools.partial(_attn_kernel, lam_init=lam_init, subtract_max=subtract_max),
        grid=(b, N_HEADS, s // tq),
        in_specs=[
            pl.BlockSpec((None, tq, LANES), qmap),
            pl.BlockSpec((None, s, LANES), lambda bi, hi, qi: (bi, 0, hi)),
            pl.BlockSpec((None, None, n_chunks, V_ROWS, tk), lambda bi, hi, qi: (bi, hi, 0, 0, 0)),
            pl.BlockSpec((4, HEAD_DIM), full),
            pl.BlockSpec((V_HEAD_DIM, 1), full),
        ],
        out_specs=pl.BlockSpec((None, tq, LANES), qmap),
        out_shape=jax.ShapeDtypeStruct((b, s, D_MODEL), BF16),
        scratch_shapes=scratch,
        compiler_params=_params("parallel", "parallel", "parallel"),
        name="diff_attention_online" if subtract_max else "diff_attention_bounded",
    )(q, k, vt, lam_vecs, subln_col)


def _attention(q, k, vt, q_gain, k_gain, lam_vecs, subln_col, lam_init):
    score_bound = (HEAD_DIM * Q_SCALE) * jnp.max(jnp.abs(q_gain)) * jnp.max(jnp.abs(k_gain))
    return lax.cond(score_bound < SAFE_LOG2_SCORE,
                    lambda *a: _attention_call(*a, lam_init, False),
                    lambda *a: _attention_call(*a, lam_init, True),
                    q, k, vt, lam_vecs, subln_col)


MLP_FF_TILE = 1024


def _mlp_body(h, g_ref, wu_ref, wd_ref, out_ref):
    xn = _rms(h, g_ref[...]).astype(BF16)
    acc = h
    for f in range(D_FF // MLP_FF_TILE):
        sl = slice(f * MLP_FF_TILE, (f + 1) * MLP_FF_TILE)
        a = jnp.maximum(_dot(xn, wu_ref[:, sl]), 0.0)
        acc = acc + _dot((a * a).astype(BF16), wd_ref[sl, :])
    out_ref[...] = acc


def _mlp_kernel(h_ref, g_ref, wu_ref, wd_ref, out_ref):
    _mlp_body(h_ref[...], g_ref, wu_ref, wd_ref, out_ref)


def _proj_mlp_kernel(h_ref, o_ref, wo_ref, g_ref, wu_ref, wd_ref, out_ref):
    _mlp_body(h_ref[...] + _dot(o_ref[...], wo_ref[...]), g_ref, wu_ref, wd_ref, out_ref)


def _resident(shape):
    return pl.BlockSpec(shape, lambda i: (0,) * len(shape), pipeline_mode=pl.Buffered(1))


def _mlp(h2, gain, w_up, w_down, attn_out=None, w_o=None, tm=1024):
    tokens = h2.shape[0]
    tok = pl.BlockSpec((tm, D_MODEL), lambda i: (i, 0))
    mlp_specs = [_resident((1, D_MODEL)), _resident((D_MODEL, D_FF)), _resident((D_FF, D_MODEL))]
    if attn_out is None:
        body, name = _mlp_kernel, "relu2_mlp"
        in_specs, args = [tok] + mlp_specs, (h2, gain, w_up, w_down)
    else:
        body, name = _proj_mlp_kernel, "attn_proj_relu2_mlp"
        in_specs = [tok, tok, _resident((D_MODEL, D_MODEL))] + mlp_specs
        args = (h2, attn_out, w_o, gain, w_up, w_down)
    return pl.pallas_call(
        body,
        grid=(tokens // tm,),
        in_specs=in_specs,
        out_specs=tok,
        out_shape=jax.ShapeDtypeStruct((tokens, D_MODEL), F32),
        compiler_params=_params("parallel"),
        name=name,
    )(*args)


def _norm_group_kernel(h_ref, g_ref, u_ref):
    chunk = h_ref.shape[1]
    for b in range(h_ref.shape[0]):
        ut = _rms(h_ref[b], g_ref[...]).T
        u_ref[:, b] = ut.reshape(N_GROUPS, SSM_GROUP, chunk)


def _norm_group(h4, gain):
    batch, n_chunks, chunk, _ = h4.shape
    return pl.pallas_call(
        _norm_group_kernel,
        grid=(n_chunks,),
        in_specs=[pl.BlockSpec((batch, None, chunk, D_MODEL), lambda c: (0, c, 0, 0)),
                  pl.BlockSpec((1, D_MODEL), lambda c: (0, 0))],
        out_specs=pl.BlockSpec((N_GROUPS, None, batch, SSM_GROUP, chunk), lambda c: (0, c, 0, 0, 0)),
        out_shape=jax.ShapeDtypeStruct((N_GROUPS, n_chunks, batch, SSM_GROUP, chunk), F32),
        compiler_params=_params("parallel"),
        name="ssm_pre_norm",
    )(h4, gain)


def _ssm_kernel(u_ref, e_ref, e_next_ref, pin_ref, bb_ref, pout_ref, cc_ref, al_ref, y_ref,
                t_even_ref, t_odd_ref, zs_ref, sfr_ref, sbr_ref, sfi_ref, sbi_ref):
    n_chunks, batch, _, chunk = u_ref.shape
    rows, width = n_chunks * batch, SSM_GROUP * chunk
    p = SSM_STATE
    assert chunk == LANES

    upper = (lax.broadcasted_iota(jnp.int32, (chunk, chunk), 1)
             >= lax.broadcasted_iota(jnp.int32, (chunk, chunk), 0))

    def build_t(lag_ref, t_ref):
        for h in range(SSM_GROUP):
            for ho in range(SSM_GROUP):
                k_bwd = jnp.broadcast_to(lag_ref[h, ho:ho + 1, 0:chunk], (chunk, chunk))
                k_fwd = jnp.broadcast_to(lag_ref[h, ho:ho + 1, chunk:2 * chunk], (chunk, chunk))
                t_fwd = pltpu.roll(k_fwd, 0, 1, stride=1, stride_axis=0)
                t_bwd = pltpu.roll(k_bwd, 0, 1, stride=1, stride_axis=0)
                t_ref[chunk * h:chunk * (h + 1), chunk * ho:chunk * (ho + 1)] = (
                    jnp.where(upper, t_fwd, t_bwd).astype(BF16))

    group = pl.program_id(0)

    @pl.when(group == 0)
    def _():
        build_t(e_ref, t_even_ref)

    def outer(pow_ref, vec_ref):
        ar_, ai_ = pow_ref[0][None], pow_ref[1][None]
        vr_, vi_ = vec_ref[0][:, None, :], vec_ref[1][:, None, :]
        return ((ar_ * vr_ - ai_ * vi_).reshape(width, 2 * p),
                (ar_ * vi_ + ai_ * vr_).reshape(width, 2 * p))

    wi_re, wi_im = outer(pin_ref, bb_ref)
    w_in = jnp.concatenate([wi_re, wi_im], axis=1).astype(BF16)
    wo_re, wo_im = outer(pout_ref, cc_ref)
    w_out_t = jnp.concatenate([wo_re, -wo_im], axis=1).astype(BF16)

    u_by_channel = jnp.swapaxes(u_ref[...].reshape(rows, SSM_GROUP, chunk), 0, 1)
    u = jnp.concatenate([u_by_channel[h] for h in range(SSM_GROUP)],
                        axis=1).astype(BF16)
    zs_ref[...] = _dot(u, w_in)

    ar = al_ref[0:1, :]
    ai = al_ref[1:2, :]
    is_fwd = lax.broadcasted_iota(jnp.int32, (batch, 2 * p), 1) < p

    def step(kk, carry):
        xr, xi = carry
        rf = pl.multiple_of(kk * batch, batch)
        rb = pl.multiple_of((n_chunks - 1 - kk) * batch, batch)
        lr = jnp.where(is_fwd, zs_ref[pl.ds(rf, batch), 0:2 * p], zs_ref[pl.ds(rb, batch), 0:2 * p])
        li = jnp.where(is_fwd, zs_ref[pl.ds(rf, batch), 2 * p:4 * p],
                       zs_ref[pl.ds(rb, batch), 2 * p:4 * p])
        sfr_ref[pl.ds(rf, batch), :] = xr
        sbr_ref[pl.ds(rb, batch), :] = xr
        sfi_ref[pl.ds(rf, batch), :] = xi
        sbi_ref[pl.ds(rb, batch), :] = xi
        return ar * xr - ai * xi + lr, ar * xi + ai * xr + li

    zero = jnp.zeros((batch, 2 * p), F32)
    lax.fori_loop(0, n_chunks, step, (zero, zero))

    fwd_lane = lax.broadcasted_iota(jnp.int32, (rows, 2 * p), 1) < p
    xin = jnp.concatenate(
        [jnp.where(fwd_lane, sfr_ref[...], sbr_ref[...]),
         jnp.where(fwd_lane, sfi_ref[...], sbi_ref[...])], axis=1).astype(BF16)
    carry_out = lax.dot_general(xin, w_out_t, (((1,), (1,)), ((), ())), preferred_element_type=F32)

    def finish(t_ref, t_next_ref):
        build_t(e_next_ref, t_next_ref)
        y = _dot(u, t_ref[...]) + carry_out
        for h in range(SSM_GROUP):
            y_ref[h] = y[:, chunk * h:chunk * (h + 1)]

    @pl.when(group % 2 == 0)
    def _():
        finish(t_even_ref, t_odd_ref)

    @pl.when(group % 2 == 1)
    def _():
        finish(t_odd_ref, t_even_ref)


def _ssm(ug, e_rows, pow_in, b_bar, pow_out, c_mat, a_chunk):
    g, n_chunks, batch, _, chunk = ug.shape
    rows, width = n_chunks * batch, SSM_GROUP * chunk
    p = SSM_STATE
    grp = lambda i: (i, 0, 0)
    grp4 = lambda i: (i, 0, 0, 0)
    return pl.pallas_call(
        _ssm_kernel,
        grid=(g,),
        in_specs=[
            pl.BlockSpec((None, n_chunks, batch, SSM_GROUP, chunk), lambda i: (i, 0, 0, 0, 0)),
            pl.BlockSpec((None, SSM_GROUP, SSM_GROUP, 2 * chunk), grp4),
            pl.BlockSpec((None, SSM_GROUP, SSM_GROUP, 2 * chunk),
                         lambda i: (jnp.minimum(i + 1, g - 1), 0, 0, 0)),
            pl.BlockSpec((None, 2, chunk, 2 * p), grp4),
            pl.BlockSpec((None, 2, SSM_GROUP, 2 * p), grp4),
            pl.BlockSpec((None, 2, chunk, 2 * p), grp4),
            pl.BlockSpec((None, 2, SSM_GROUP, 2 * p), grp4),
            pl.BlockSpec((None, 2, 2 * p), grp),
        ],
        out_specs=pl.BlockSpec((None, SSM_GROUP, rows, chunk), grp4),
        out_shape=jax.ShapeDtypeStruct((g, SSM_GROUP, rows, chunk), F32),
        scratch_shapes=[
            pltpu.VMEM((width, width), BF16),
            pltpu.VMEM((width, width), BF16),
            pltpu.VMEM((rows, 4 * p), F32),
            pltpu.VMEM((rows, 2 * p), F32),
            pltpu.VMEM((rows, 2 * p), F32),
            pltpu.VMEM((rows, 2 * p), F32),
            pltpu.VMEM((rows, 2 * p), F32),
        ],
        compiler_params=_params("arbitrary"),
        name="s5_chunked_scan",
    )(ug, e_rows, e_rows, pow_in, b_bar, pow_out, c_mat, a_chunk)


def _glu_kernel(h_ref, y_ref, g_ref, d_ref, w_ref, out_ref):
    batch, chunk, _ = h_ref.shape
    acts = []
    y_by_batch = jnp.swapaxes(y_ref[...].reshape(D_MODEL, batch, chunk), 0, 1)
    for b in range(batch):
        u = _rms(h_ref[b], g_ref[...])
        y = y_by_batch[b].T
        acts.append(jax.nn.gelu(y + d_ref[...] * u).astype(BF16))
    z = _dot(jnp.concatenate(acts, axis=0), w_ref[...])
    for b in range(batch):
        zb = z[b * chunk:(b + 1) * chunk]
        gate = 0.5 + 0.5 * jnp.tanh(0.5 * zb[:, D_MODEL:2 * D_MODEL])
        out_ref[b] = h_ref[b] + zb[:, 0:D_MODEL] * gate


def _glu(h4, y4, gain, d_skip, w_glu):
    batch, n_chunks, chunk, _ = h4.shape
    tok = pl.BlockSpec((batch, None, chunk, D_MODEL), lambda c: (0, c, 0, 0))
    return pl.pallas_call(
        _glu_kernel,
        grid=(n_chunks,),
        in_specs=[
            tok,
            pl.BlockSpec((N_GROUPS, SSM_GROUP, batch, chunk), lambda c: (0, 0, c, 0)),
            _resident((1, D_MODEL)),
            _resident((1, D_MODEL)),
            _resident((D_MODEL, 2 * D_MODEL)),
        ],
        out_specs=tok,
        out_shape=jax.ShapeDtypeStruct(h4.shape, F32),
        compiler_params=_params("parallel"),
        name="ssm_gelu_glu",
    )(h4, y4, gain, d_skip, w_glu)


def _ssm_operands(a_re, a_im, log_dt, b_re, b_im, c_re, c_im, chunk):
    hp = lax.Precision.HIGH
    dt = jnp.exp(log_dt)[..., None]
    zr, zi = a_re * dt, a_im * dt
    t = jnp.arange(chunk + 1, dtype=F32)[None, None, :, None]
    mag = jnp.exp(zr[:, :, None, :] * t)
    ang = zi[:, :, None, :] * t
    pr, pi = mag * jnp.cos(ang), mag * jnp.sin(ang)
    nr, ni = pr[:, :, 1, :] - 1.0, pi[:, :, 1, :]
    den = a_re * a_re + a_im * a_im
    qr = (nr * a_re + ni * a_im) / den
    qi = (ni * a_re - nr * a_im) / den
    bbr = qr[..., None] * b_re - qi[..., None] * b_im
    bbi = qr[..., None] * b_im + qi[..., None] * b_re

    cp_re, cp_im = c_re.transpose(0, 1, 3, 2)[..., None], c_im.transpose(0, 1, 3, 2)[..., None]
    w_re = cp_re * bbr[:, :, :, None, :] - cp_im * bbi[:, :, :, None, :]
    w_im = cp_re * bbi[:, :, :, None, :] + cp_im * bbr[:, :, :, None, :]
    n_pairs = SSM_GROUP * SSM_GROUP
    g = a_re.shape[1]
    kern = jnp.einsum('dgtp,dgpn->dgtn',
                      jnp.concatenate([pr[:, :, :chunk], -pi[:, :, :chunk]], axis=-1),
                      jnp.concatenate([w_re, w_im], axis=2).reshape(2, g, 2 * SSM_STATE, n_pairs),
                      precision=hp).reshape(2, g, chunk, SSM_GROUP, SSM_GROUP)
    center = (kern[0, :, 0] + kern[1, :, 0])[:, None]
    lagged = jnp.concatenate([jnp.zeros_like(center), kern[1, :, :0:-1], center, kern[0, :, 1:]],
                             axis=1)
    e_rows = lagged.transpose(0, 3, 2, 1)

    def fwd_bwd(x):
        return jnp.concatenate([x[0], x[1]], axis=-1)

    def re_im(xr, xi):
        return jnp.stack([fwd_bwd(xr), fwd_bwd(xi)], axis=1)

    pow_in = re_im(jnp.stack([pr[0, :, chunk - 1::-1], pr[1, :, :chunk]]),
                   jnp.stack([pi[0, :, chunk - 1::-1], pi[1, :, :chunk]]))
    b_bar = re_im(bbr.transpose(0, 1, 3, 2), bbi.transpose(0, 1, 3, 2))
    pow_out = re_im(jnp.stack([pr[0, :, 1:], pr[1, :, :0:-1]]),
                    jnp.stack([pi[0, :, 1:], pi[1, :, :0:-1]]))
    c_mat = re_im(c_re, c_im)
    a_chunk = jnp.stack([jnp.concatenate([pr[0, :, chunk], pr[1, :, chunk]], axis=-1),
                         jnp.concatenate([pi[0, :, chunk], pi[1, :, chunk]], axis=-1)], axis=1)
    return e_rows, pow_in, b_bar, pow_out, c_mat, a_chunk


def _ssm_layer(h2, batch, seq, gain, operands, d_skip, w_glu):
    chunk = SSM_CHUNK
    n_chunks = seq // chunk
    h4 = h2.reshape(batch, n_chunks, chunk, D_MODEL)
    ug = _norm_group(h4, gain)
    y4 = _ssm(ug, *operands)
    return _glu(h4, y4, gain, d_skip, w_glu).reshape(batch * seq, D_MODEL)


def _rope_tables(seq):
    pos = jnp.arange(seq, dtype=F32)
    inv_freq = ROPE_THETA ** (-jnp.arange(0, HEAD_DIM, 2, dtype=F32) / HEAD_DIM)
    ang = pos[:, None] * inv_freq[None, :]
    cos, sin = jnp.cos(ang), jnp.sin(ang)
    reps = LANES // HEAD_DIM
    cos_t = jnp.tile(jnp.concatenate([cos, cos], axis=-1), (1, reps))
    sin_t = jnp.tile(jnp.concatenate([-sin, sin], axis=-1), (1, reps))
    return cos_t, sin_t


def _lambda_init(layer_idx):
    return 0.8 - 0.6 * math.exp(-0.3 * layer_idx)


def kernel(x, norm_mix, norm_ffn, attn_w_qkv, attn_q_gain, attn_k_gain, attn_lambda,
           attn_subln, attn_w_o, ssm_a_re, ssm_a_im, ssm_log_dt, ssm_b_re, ssm_b_im,
           ssm_c_re, ssm_c_im, ssm_d, ssm_w_glu, ffn_w_up, ffn_w_down):
    batch, seq, d = x.shape
    assert d == D_MODEL
    depth = norm_mix.shape[0]
    cos_t, sin_t = _rope_tables(seq)
    ssm_operands = jax.vmap(functools.partial(_ssm_operands, chunk=SSM_CHUNK))(
        ssm_a_re, ssm_a_im, ssm_log_dt, ssm_b_re, ssm_b_im, ssm_c_re, ssm_c_im)
    reps = LANES // HEAD_DIM
    h2 = x.reshape(batch * seq, D_MODEL)
    for i in range(depth):
        j = i // N_MIXERS
        gain = norm_mix[i][None, :]
        if i % N_MIXERS == 0:
            qg, kg = attn_q_gain[j], attn_k_gain[j]
            w_qk = attn_w_qkv[j][:, :2 * D_MODEL].astype(BF16)
            w_v_t = attn_w_qkv[j][:, 2 * D_MODEL:].T.astype(BF16)
            q, k, vt = _qkv(h2, gain, w_qk, w_v_t, jnp.tile(qg, reps)[None, :],
                            jnp.tile(kg, reps)[None, :], cos_t, sin_t, seq)
            o = _attention(q.reshape(batch, seq, D_MODEL), k.reshape(batch, seq, D_MODEL), vt, qg, kg,
                           attn_lambda[j], attn_subln[j][:, None], _lambda_init(i))
            proj = dict(attn_out=o.reshape(batch * seq, D_MODEL), w_o=attn_w_o[j].astype(BF16))
        else:
            h2 = _ssm_layer(h2, batch, seq, gain, [op[j] for op in ssm_operands],
                            ssm_d[j][None, :], ssm_w_glu[j].astype(BF16))
            proj = {}
        h2 = _mlp(h2, norm_ffn[i][None, :], ffn_w_up[i].astype(BF16), ffn_w_down[i].astype(BF16),
                  **proj)
    return h2.reshape(batch, seq, D_MODEL)
```
